```python
import math
import jax
import jax.numpy as jnp
from jax import lax
import numpy as np

D_MODEL = 1024
BATCH = 8
SEQ = 4096
DEPTH = 2

CHUNK = 64
N_EVEN = (DEPTH + 1) // 2
N_ODD = DEPTH // 2
RMS_EPS = 1e-6

S5_WIDTH = D_MODEL // 2
S5_GROUP = 16
S5_GROUPS = S5_WIDTH // S5_GROUP
S5_STATE = 64
SB_HEAD_DIM = 64
SB_WIDTH = D_MODEL // 2
SB_HEADS = SB_WIDTH // SB_HEAD_DIM
SB_QBLOCK = 128
EVEN_IN = S5_WIDTH + 3 * SB_WIDTH
EVEN_MIX = S5_WIDTH + SB_WIDTH

RW_WIDTH = D_MODEL // 2
RW_HEAD = 64
RW_HEADS = RW_WIDTH // RW_HEAD
RW_DECAY_LORA = 64
RW_AAA_LORA = 64
RW_GATE_LORA = 128
RW_IN = 3 * RW_WIDTH + RW_DECAY_LORA + RW_AAA_LORA + RW_GATE_LORA
RW_LN_EPS = 64e-5

M2_INNER = D_MODEL // 2
M2_HEADDIM = 64
M2_HEADS = M2_INNER // M2_HEADDIM
M2_GROUPS = 2
M2_STATE = 128
M2_CONV = 4
M2_CONV_DIM = M2_INNER + 2 * M2_GROUPS * M2_STATE
M2_IN = M2_INNER + M2_CONV_DIM + M2_HEADS
ODD_IN = RW_IN + M2_IN
ODD_MIX = RW_WIDTH + M2_INNER

MOE_GROUPS = 4
MOE_PER_GROUP = 8
MOE_EXPERTS = MOE_GROUPS * MOE_PER_GROUP
MOE_TOPK = 2
MOE_HIDDEN = 512
MOE_BLOCK = 128

kernel_name = 'hybrid_s5_stickbreak_rwkv7_ssd_hmoe'


def rmsnorm(x, w):
    xf = x.astype(jnp.float32)
    xf = xf * lax.rsqrt(jnp.mean(xf * xf, axis=-1, keepdims=True) + RMS_EPS)
    return xf.astype(x.dtype) * w


def modulate(h, shift, scale):
    return h * (1.0 + scale[:, None, :]) + shift[:, None, :]


def token_shift(z):
    return jnp.pad(z[:, :-1], ((0, 0), (1, 0), (0, 0)))


def _cmul(ar, ai, br, bi):
    return ar * br - ai * bi, ar * bi + ai * br


def s5_mixer(u, a_re, a_im, log_dt, b_re, b_im, c_re, c_im, d_skip, w_glu):
    f32 = jnp.float32
    bsz, seq, _ = u.shape
    a_re = jnp.minimum(a_re.astype(f32), -1e-4)
    a_im = a_im.astype(f32)
    dt = jnp.exp(log_dt.astype(f32))[:, None]
    mag = jnp.exp(dt * a_re)
    abar_re, abar_im = mag * jnp.cos(dt * a_im), mag * jnp.sin(dt * a_im)
    den = a_re * a_re + a_im * a_im
    num_re, num_im = abar_re - 1.0, abar_im
    coef_re = (num_re * a_re + num_im * a_im) / den
    coef_im = (num_im * a_re - num_re * a_im) / den
    bb_re, bb_im = _cmul(coef_re[..., None], coef_im[..., None], b_re.astype(f32), b_im.astype(f32))
    c_re = c_re.astype(f32)
    c_im = c_im.astype(f32)
    d_skip = d_skip.astype(f32)
    n_chunks = seq // CHUNK
    uc = u.astype(f32).reshape(bsz, n_chunks, CHUNK, S5_GROUPS, S5_GROUP).transpose(1, 0, 2, 3, 4)

    def combine(e1, e2):
        ar1, ai1, br1, bi1 = e1
        ar2, ai2, br2, bi2 = e2
        ar, ai = _cmul(ar2, ai2, ar1, ai1)
        tr, ti = _cmul(ar2, ai2, br1, bi1)
        return ar, ai, tr + br2, ti + bi2

    def chunk_step(carry, u_c):
        h_re, h_im = carry
        bu_re = jnp.einsum('blgp,gnp->blgn', u_c, bb_re)
        bu_im = jnp.einsum('blgp,gnp->blgn', u_c, bb_im)
        ar = jnp.broadcast_to(abar_re, bu_re.shape)
        ai = jnp.broadcast_to(abar_im, bu_re.shape)
        pa_re, pa_im, s_re, s_im = lax.associative_scan(combine, (ar, ai, bu_re, bu_im), axis=1)
        cr, ci = _cmul(pa_re, pa_im, h_re[:, None], h_im[:, None])
        s_re = s_re + cr
        s_im = s_im + ci
        y = (jnp.einsum('blgn,gpn->blgp', s_re, c_re)
             - jnp.einsum('blgn,gpn->blgp', s_im, c_im)
             + d_skip * u_c)
        return (s_re[:, -1], s_im[:, -1]), y

    zeros = jnp.zeros((bsz, S5_GROUPS, S5_STATE), f32)
    _, ys = lax.scan(chunk_step, (zeros, zeros), uc)
    y = jax.nn.gelu(ys.transpose(1, 0, 2, 3, 4).reshape(bsz, seq, S5_WIDTH)).astype(u.dtype)
    g = y @ w_glu
    return g[..., :S5_WIDTH] * jax.nn.sigmoid(g[..., S5_WIDTH:])


def stick_breaking_attention(q, k, v):
    seq = q.shape[1]
    q, k, v = (jnp.swapaxes(z, 1, 2) for z in (q, k, v))
    scale = SB_HEAD_DIM ** -0.5
    outs = []
    for blk in range(seq // SB_QBLOCK):
        s0 = blk * SB_QBLOCK
        s1 = s0 + SB_QBLOCK
        z = jnp.einsum('bhqd,bhkd->bhqk', q[:, :, s0:s1], k[:, :, :s1]).astype(jnp.float32) * scale
        mask = jnp.arange(s1)[None, :] < jnp.arange(s0, s1)[:, None]
        log_keep = jnp.where(mask, jax.nn.log_sigmoid(-z), 0.0)
        rc = lax.cumsum(log_keep, axis=3, reverse=True)
        suffix = jnp.concatenate([rc[..., 1:], jnp.zeros_like(rc[..., :1])], axis=-1)
        w = jnp.where(mask, jnp.exp(jax.nn.log_sigmoid(z) + suffix), 0.0)
        outs.append(jnp.einsum('bhqk,bhkd->bhqd', w.astype(v.dtype), v[:, :, :s1]))
    return jnp.swapaxes(jnp.concatenate(outs, axis=2), 1, 2)


def even_mixer(h, w_in, w_out, a_re, a_im, log_dt, b_re, b_im, c_re, c_im, d_skip, w_glu):
    bsz, seq, _ = h.shape
    proj = h @ w_in
    u = proj[..., :S5_WIDTH]
    q, k, v = (proj[..., S5_WIDTH + i * SB_WIDTH:S5_WIDTH + (i + 1) * SB_WIDTH]
               .reshape(bsz, seq, SB_HEADS, SB_HEAD_DIM) for i in range(3))
    y_a = s5_mixer(u, a_re, a_im, log_dt, b_re, b_im, c_re, c_im, d_skip, w_glu)
    y_b = stick_breaking_attention(q, k, v).reshape(bsz, seq, SB_WIDTH)
    return jnp.concatenate([y_a, y_b], axis=-1) @ w_out


def rwkv7_recurrence(r, decay, k, v, a, b):
    bsz, _, heads, n = r.shape

    def step(state, inp):
        r_t, d_t, k_t, v_t, a_t, b_t = inp
        sa = jnp.einsum('bhvk,bhk->bhv', state, a_t)
        state = (state * d_t[:, :, None, :] + sa[..., None] * b_t[:, :, None, :]
                 + v_t[..., None] * k_t[:, :, None, :])
        return state, jnp.einsum('bhvk,bhk->bhv', state, r_t)

    xs = tuple(jnp.swapaxes(z, 0, 1) for z in (r, decay, k, v, a, b))
    _, ys = lax.scan(step, jnp.zeros((bsz, heads, n, n), jnp.float32), xs)
    return jnp.swapaxes(ys, 0, 1)


def rwkv7_mixer(p, mu, w0, w2, a0, a2, g2, k_k, k_a, r_k, ln_w, ln_b):
    bsz, seq, _ = p.shape
    p = p + (token_shift(p) - p) * mu
    i1, i2, i3 = RW_WIDTH, 2 * RW_WIDTH, 3 * RW_WIDTH
    i4 = i3 + RW_DECAY_LORA
    i5 = i4 + RW_AAA_LORA
    r, k, v, xw, xa, xg = jnp.split(p.astype(jnp.float32), [i1, i2, i3, i4, i5], axis=-1)
    w = -jax.nn.softplus(-(w0 + jnp.tanh(xw) @ w2)) - 0.5
    decay = jnp.exp(-jnp.exp(w))
    a = jax.nn.sigmoid(a0 + xa @ a2)
    g = jax.nn.sigmoid(xg) @ g2

    def heads(z):
        return z.reshape(bsz, seq, RW_HEADS, RW_HEAD)

    kk = heads(k * k_k)
    kk = kk / jnp.maximum(jnp.linalg.norm(kk, axis=-1, keepdims=True), 1e-12)
    k = k * (1.0 + (a - 1.0) * k_a)
    r_h, k_h, v_h, a_h = heads(r), heads(k), heads(v), heads(a)
    y = rwkv7_recurrence(r_h, heads(decay), k_h, v_h, -kk, kk * a_h)
    mean = jnp.mean(y, axis=-1, keepdims=True)
    var = jnp.mean(jnp.square(y - mean), axis=-1, keepdims=True)
    y = ((y - mean) * lax.rsqrt(var + RW_LN_EPS)).reshape(bsz, seq, RW_WIDTH) * ln_w + ln_b
    bonus = jnp.sum(r_h * k_h * r_k, axis=-1, keepdims=True) * v_h
    y = (y + bonus.reshape(bsz, seq, RW_WIDTH)) * g
    return y.astype(p.dtype)


def causal_dwconv(x, w, b):
    width, ch = w.shape
    y = lax.conv_general_dilated(x, w[:, None, :].astype(x.dtype), window_strides=(1,),
                                 padding=[(width - 1, 0)],
                                 dimension_numbers=('NWC', 'WIO', 'NWC'),
                                 feature_group_count=ch)
    return y + b


def segsum(z):
    n = z.shape[-1]
    cs = jnp.cumsum(z, axis=-1)
    diff = cs[..., :, None] - cs[..., None, :]
    return jnp.where(jnp.tril(jnp.ones((n, n), bool)), diff, -jnp.inf)


def ssd_chunked(x, dt, a, bm, cm):
    bsz, seq, nh, hp = x.shape
    n_chunks = seq // CHUNK
    xd = (x * dt[..., None]).reshape(bsz, n_chunks, CHUNK, nh, hp)
    a_dt = (a * dt).reshape(bsz, n_chunks, CHUNK, nh).transpose(0, 3, 1, 2)
    bc = bm.reshape(bsz, n_chunks, CHUNK, nh, -1)
    cc = cm.reshape(bsz, n_chunks, CHUNK, nh, -1)
    a_cs = jnp.cumsum(a_dt, axis=-1)
    l_mat = jnp.exp(segsum(a_dt))
    y_diag = jnp.einsum('bclhn,bcshn,bhcls,bcshp->bclhp', cc, bc, l_mat, xd)
    decay_states = jnp.exp(a_cs[..., -1:] - a_cs)
    states = jnp.einsum('bclhn,bhcl,bclhp->bchpn', bc, decay_states, xd)
    states = jnp.concatenate([jnp.zeros_like(states[:, :1]), states], axis=1)
    decay_chunk = jnp.exp(segsum(jnp.pad(a_cs[..., -1], ((0, 0), (0, 0), (1, 0)))))
    states = jnp.einsum('bhzc,bchpn->bzhpn', decay_chunk, states)[:, :-1]
    y_off = jnp.einsum('bclhn,bchpn,bhcl->bclhp', cc, states, jnp.exp(a_cs))
    return (y_diag + y_off).reshape(bsz, seq, nh, hp)


def mamba2_mixer(p, conv_w, conv_b, dt_bias, a_log, d_skip, norm_w):
    f32 = jnp.float32
    bsz, seq, _ = p.shape
    z = p[..., :M2_INNER]
    xbc = jax.nn.silu(causal_dwconv(p[..., M2_INNER:M2_INNER + M2_CONV_DIM], conv_w, conv_b))
    dt_raw = p[..., M2_INNER + M2_CONV_DIM:]
    xs = xbc[..., :M2_INNER].astype(f32).reshape(bsz, seq, M2_HEADS, M2_HEADDIM)
    heads_per_group = M2_HEADS // M2_GROUPS
    bm = xbc[..., M2_INNER:M2_INNER + M2_GROUPS * M2_STATE].astype(f32).reshape(bsz, seq, M2_GROUPS, M2_STATE)
    cm = xbc[..., M2_INNER + M2_GROUPS * M2_STATE:].astype(f32).reshape(bsz, seq, M2_GROUPS, M2_STATE)
    bm = jnp.repeat(bm, heads_per_group, axis=2)
    cm = jnp.repeat(cm, heads_per_group, axis=2)
    dt = jax.nn.softplus(dt_raw.astype(f32) + dt_bias.astype(f32))
    a = -jnp.exp(a_log.astype(f32))
    y = ssd_chunked(xs, dt, a, bm, cm) + d_skip.astype(f32)[:, None] * xs
    y = y.reshape(bsz, seq, M2_INNER) * jax.nn.silu(z.astype(f32))
    yg = y.reshape(bsz, seq, M2_GROUPS, M2_INNER // M2_GROUPS)
    yg = yg * lax.rsqrt(jnp.mean(yg * yg, axis=-1, keepdims=True) + RMS_EPS)
    return (yg.reshape(bsz, seq, M2_INNER) * norm_w).astype(p.dtype)


def odd_mixer(h, w_in, w_out, rw_mu, rw_w0, rw_w2, rw_a0, rw_a2, rw_g2, rw_k_k, rw_k_a,
              rw_r_k, rw_ln_w, rw_ln_b, m2_conv_w, m2_conv_b, m2_dt_bias, m2_a_log, m2_d, m2_norm_w):
    proj = h @ w_in
    y_c = rwkv7_mixer(proj[..., :RW_IN], rw_mu, rw_w0, rw_w2, rw_a0, rw_a2, rw_g2,
                      rw_k_k, rw_k_a, rw_r_k, rw_ln_w, rw_ln_b)
    y_d = mamba2_mixer(proj[..., RW_IN:], m2_conv_w, m2_conv_b, m2_dt_bias, m2_a_log, m2_d, m2_norm_w)
    return jnp.concatenate([y_c, y_d], axis=-1) @ w_out


def routed_experts(t, expert_idx, gates, w_gate, w_up, w_down):
    n_tok, d = t.shape
    n_assign = n_tok * MOE_TOPK
    flat_e = expert_idx.reshape(-1)
    flat_tok = jnp.repeat(jnp.arange(n_tok, dtype=jnp.int32), MOE_TOPK)
    flat_g = gates.reshape(-1)
    order = jnp.argsort(flat_e)
    sorted_e = flat_e[order]
    counts = jnp.bincount(flat_e, length=MOE_EXPERTS)
    padded = ((counts + MOE_BLOCK - 1) // MOE_BLOCK) * MOE_BLOCK
    pad_end = jnp.cumsum(padded)
    pad_start = pad_end - padded
    start = jnp.cumsum(counts) - counts
    dest = pad_start[sorted_e] + jnp.arange(n_assign) - start[sorted_e]
    n_blocks = (n_assign + MOE_EXPERTS * (MOE_BLOCK - 1) + MOE_BLOCK - 1) // MOE_BLOCK
    n_rows = n_blocks * MOE_BLOCK
    row_tok = jnp.full((n_rows,), n_tok, jnp.int32).at[dest].set(flat_tok[order])
    row_gate = jnp.zeros((n_rows,), jnp.float32).at[dest].set(flat_g[order])
    blk_expert = jnp.minimum(jnp.searchsorted(pad_end, jnp.arange(n_blocks) * MOE_BLOCK, side='right'),
                             MOE_EXPERTS - 1)
    t_pad = jnp.concatenate([t, jnp.zeros((1, d), t.dtype)], axis=0)
    xs = t_pad[row_tok].reshape(n_blocks, MOE_BLOCK, d)

    def expert_block(args):
        xb, e = args
        return (jax.nn.silu(xb @ w_gate[e]) * (xb @ w_up[e])) @ w_down[e]

    ys = lax.map(expert_block, (xs, blk_expert)).reshape(n_rows, d)
    out = jnp.zeros((n_tok + 1, d), ys.dtype).at[row_tok].add(ys * row_gate[:, None].astype(ys.dtype))
    return out[:n_tok]


def hierarchical_moe(h, w_grp, b_grp, w_exp, b_exp, w_gate, w_up, w_down):
    bsz, seq, d = h.shape
    t = h.reshape(-1, d)
    n_tok = t.shape[0]
    grp_prob = jax.nn.softmax((t @ w_grp + b_grp).astype(jnp.float32), axis=-1)
    grp_p, grp_idx = lax.top_k(grp_prob, 1)
    exp_logits = (t @ w_exp + b_exp).astype(jnp.float32).reshape(n_tok, MOE_GROUPS, MOE_PER_GROUP)
    in_grp = exp_logits[jnp.arange(n_tok), grp_idx[:, 0]]
    top_logit, top_local = lax.top_k(in_grp, MOE_TOPK)
    gates = grp_p * jax.nn.softmax(top_logit, axis=-1)
    expert_idx = grp_idx * MOE_PER_GROUP + top_local
    return routed_experts(t, expert_idx, gates, w_gate, w_up, w_down).reshape(bsz, seq, d)


def setup_inputs(seed: int = 0) -> dict:
    key = jax.random.key(seed)
    ks = iter(jax.random.split(key, 64))
    f32 = jnp.float32
    D = D_MODEL

    def nrm(shape, scale):
        return scale * jax.random.normal(next(ks), shape, f32)

    def uni(shape, lo, hi):
        return jax.random.uniform(next(ks), shape, f32, lo, hi)

    log_dt_lo, log_dt_hi = math.log(1e-3), math.log(1e-1)
    m2_dt = jnp.exp(uni((N_ODD, M2_HEADS), log_dt_lo, log_dt_hi))
    return {
        'x': nrm((BATCH, SEQ, D), 1.0),
        'c': nrm((BATCH, D), 1.0),
        'ada_w': nrm((DEPTH, D, 6 * D), 0.5 * D ** -0.5),
        'ada_b': nrm((DEPTH, 6 * D), 0.02),
        'norm_mix_w': 1.0 + nrm((DEPTH, D), 0.05),
        'norm_ffn_w': 1.0 + nrm((DEPTH, D), 0.05),
        'even_w_in': nrm((N_EVEN, D, EVEN_IN), D ** -0.5),
        'even_w_out': nrm((N_EVEN, EVEN_MIX, D), EVEN_MIX ** -0.5),
        's5_a_re': -0.5 + nrm((N_EVEN, S5_GROUPS, S5_STATE), 0.01),
        's5_a_im': math.pi * jnp.arange(S5_STATE, dtype=f32)[None, None, :] + nrm((N_EVEN, S5_GROUPS, S5_STATE), 0.01),
        's5_log_dt': uni((N_EVEN, S5_GROUPS), log_dt_lo, log_dt_hi),
        's5_b_re': nrm((N_EVEN, S5_GROUPS, S5_STATE, S5_GROUP), (2 * S5_GROUP) ** -0.5),
        's5_b_im': nrm((N_EVEN, S5_GROUPS, S5_STATE, S5_GROUP), (2 * S5_GROUP) ** -0.5),
        's5_c_re': nrm((N_EVEN, S5_GROUPS, S5_GROUP, S5_STATE), S5_STATE ** -0.5),
        's5_c_im': nrm((N_EVEN, S5_GROUPS, S5_GROUP, S5_STATE), S5_STATE ** -0.5),
        's5_d': nrm((N_EVEN, S5_GROUPS, S5_GROUP), 1.0),
        's5_w_glu': nrm((N_EVEN, S5_WIDTH, 2 * S5_WIDTH), S5_WIDTH ** -0.5),
        'odd_w_in': nrm((N_ODD, D, ODD_IN), D ** -0.5),
        'odd_w_out': nrm((N_ODD, ODD_MIX, D), ODD_MIX ** -0.5),
        'rw_mu': uni((N_ODD, RW_IN), 0.0, 1.0),
        'rw_w0': uni((N_ODD, RW_WIDTH), -5.0, 0.0),
        'rw_w2': nrm((N_ODD, RW_DECAY_LORA, RW_WIDTH), 0.1),
        'rw_a0': nrm((N_ODD, RW_WIDTH), 0.1),
        'rw_a2': nrm((N_ODD, RW_AAA_LORA, RW_WIDTH), 0.1),
        'rw_g2': nrm((N_ODD, RW_GATE_LORA, RW_WIDTH), RW_GATE_LORA ** -0.5),
        'rw_k_k': 0.85 + nrm((N_ODD, RW_WIDTH), 0.02),
        'rw_k_a': 1.0 + nrm((N_ODD, RW_WIDTH), 0.02),
        'rw_r_k': nrm((N_ODD, RW_HEADS, RW_HEAD), 0.1),
        'rw_ln_w': 1.0 + nrm((N_ODD, RW_WIDTH), 0.05),
        'rw_ln_b': nrm((N_ODD, RW_WIDTH), 0.02),
        'm2_conv_w': nrm((N_ODD, M2_CONV, M2_CONV_DIM), M2_CONV ** -0.5),
        'm2_conv_b': nrm((N_ODD, M2_CONV_DIM), 0.02),
        'm2_dt_bias': m2_dt + jnp.log(-jnp.expm1(-m2_dt)),
        'm2_a_log': jnp.log(uni((N_ODD, M2_HEADS), 1.0, 16.0)),
        'm2_d': 1.0 + nrm((N_ODD, M2_HEADS), 0.1),
        'm2_norm_w': 1.0 + nrm((N_ODD, M2_INNER), 0.05),
        'moe_w_grp': nrm((DEPTH, D, MOE_GROUPS), D ** -0.5),
        'moe_b_grp': nrm((DEPTH, MOE_GROUPS), 0.01),
        'moe_w_exp': nrm((DEPTH, D, MOE_EXPERTS), D ** -0.5),
        'moe_b_exp': nrm((DEPTH, MOE_EXPERTS), 0.01),
        'moe_w_gate': nrm((DEPTH, MOE_EXPERTS, D, MOE_HIDDEN), D ** -0.5),
        'moe_w_up': nrm((DEPTH, MOE_EXPERTS, D, MOE_HIDDEN), D ** -0.5),
        'moe_w_down': nrm((DEPTH, MOE_EXPERTS, MOE_HIDDEN, D), MOE_HIDDEN ** -0.5),
        'final_norm_w': 1.0 + nrm((D,), 0.05),
    }


def reference(x, c, ada_w, ada_b, norm_mix_w, norm_ffn_w, even_w_in, even_w_out,
              s5_a_re, s5_a_im, s5_log_dt, s5_b_re, s5_b_im, s5_c_re, s5_c_im, s5_d, s5_w_glu,
              odd_w_in, odd_w_out, rw_mu, rw_w0, rw_w2, rw_a0, rw_a2, rw_g2, rw_k_k, rw_k_a,
              rw_r_k, rw_ln_w, rw_ln_b, m2_conv_w, m2_conv_b, m2_dt_bias, m2_a_log, m2_d, m2_norm_w,
              moe_w_grp, moe_b_grp, moe_w_exp, moe_b_exp, moe_w_gate, moe_w_up, moe_w_down,
              final_norm_w):
    cond = jax.nn.silu(c)
    for i in range(DEPTH):
        j = i // 2
        shift1, scale1, gate1, shift2, scale2, gate2 = jnp.split(cond @ ada_w[i] + ada_b[i], 6, axis=-1)
        h = modulate(rmsnorm(x, norm_mix_w[i]), shift1, scale1)
        if i % 2 == 0:
            y = even_mixer(h, even_w_in[j], even_w_out[j], s5_a_re[j], s5_a_im[j], s5_log_dt[j],
                           s5_b_re[j], s5_b_im[j], s5_c_re[j], s5_c_im[j], s5_d[j], s5_w_glu[j])
        else:
            y = odd_mixer(h, odd_w_in[j], odd_w_out[j], rw_mu[j], rw_w0[j], rw_w2[j], rw_a0[j],
                          rw_a2[j], rw_g2[j], rw_k_k[j], rw_k_a[j], rw_r_k[j], rw_ln_w[j], rw_ln_b[j],
                          m2_conv_w[j], m2_conv_b[j], m2_dt_bias[j], m2_a_log[j], m2_d[j], m2_norm_w[j])
        x = x + gate1[:, None, :] * y
        h = modulate(rmsnorm(x, norm_ffn_w[i]), shift2, scale2)
        x = x + gate2[:, None, :] * hierarchical_moe(h, moe_w_grp[i], moe_b_grp[i], moe_w_exp[i], moe_b_exp[i],
                                                     moe_w_gate[i], moe_w_up[i], moe_w_down[i])
    return rmsnorm(x, final_norm_w)
```

```python
import functools
import math

import jax
import jax.numpy as jnp
from jax import lax
from jax.experimental import pallas as pl
from jax.experimental.pallas import tpu as pltpu

F32 = jnp.float32
BF16 = jnp.bfloat16

D_MODEL = 1024
RMS_EPS = 1e-6
LANES = 128
VMEM_LIMIT = 56 * 1024 * 1024

S5_WIDTH = 512
S5_P = 16
S5_G = 32
S5_N = 64
S5_L = 16
SB_WIDTH = 512
SB_HEAD = 64
SB_TQ = 256
SB_TK = 128

RW_WIDTH = 512
RW_HEAD = 64
RW_HEADS = 8
RW_IN = 1792
RW_LN_EPS = 64e-5
RW_L = 64
RW_TT = 256

M2_INNER = 512
M2_HEADS = 8
M2_STATE = 128
M2_CONV = 4
M2_CONV_DIM = 1024
M2_IN = 1544
M2_IN_PAD = 1664
M2_L = 256

MOE_GROUPS = 4
MOE_PER_GROUP = 8
MOE_EXPERTS = 32
MOE_TOPK = 2
MOE_HIDDEN = 512
MOE_BLK = 256


def _cparams(*sem):
    return pltpu.CompilerParams(dimension_semantics=sem, vmem_limit_bytes=VMEM_LIMIT)


def _bdot(a, b):
    return jnp.dot(a.astype(BF16), b.astype(BF16), preferred_element_type=F32)


def _bdot_nt(a, b):
    return lax.dot_general(a.astype(BF16), b.astype(BF16), (((1,), (1,)), ((), ())),
                           preferred_element_type=F32)


def _bdot_tn(a, b):
    return lax.dot_general(a.astype(BF16), b.astype(BF16), (((0,), (0,)), ((), ())),
                           preferred_element_type=F32)


def _split2(a):
    hi = a.astype(BF16)
    lo = (a - hi.astype(F32)).astype(BF16)
    return hi, lo


def _dot3(a, b):
    ah, al = _split2(a)
    bh, bl = _split2(b)
    d = functools.partial(jnp.dot, preferred_element_type=F32)
    return d(ah, bh) + d(ah, bl) + d(al, bh)


def _dot_exact_rhs(a, b01):
    ah, al = _split2(a)
    d = functools.partial(jnp.dot, preferred_element_type=F32)
    return d(ah, b01) + d(al, b01)


def _dot_exact_lhs(a01, b):
    bh, bl = _split2(b)
    d = functools.partial(jnp.dot, preferred_element_type=F32)
    return d(a01, bh) + d(a01, bl)


def _sigmoid(x):
    return 1.0 / (1.0 + jnp.exp(-x))


def _softplus(x):
    return jnp.maximum(x, 0.0) + jnp.log(1.0 + jnp.exp(-jnp.abs(x)))


def _silu(x):
    return x * _sigmoid(x)


def _gelu_tanh(x):
    c = math.sqrt(2.0 / math.pi)
    return 0.5 * x * (1.0 + jnp.tanh(c * (x + 0.044715 * (x * x * x))))


def _rms_mod(x, nw, shift, scale):
    ms = jnp.mean(x * x, axis=-1, keepdims=True)
    return (x * lax.rsqrt(ms + RMS_EPS)) * nw * (1.0 + scale) + shift


def _ada_kernel(c_ref, w_ref, b_ref, o_ref):
    cond = _silu(c_ref[...])
    o_ref[0] = _dot3(cond, w_ref[0]) + b_ref[0]


def _ada_params(c, ada_w, ada_b):
    depth, d, n = ada_w.shape
    bsz = c.shape[0]
    tn = 1024
    return pl.pallas_call(
        _ada_kernel,
        grid=(depth, n // tn),
        in_specs=[pl.BlockSpec((bsz, d), lambda i, j: (0, 0)),
                  pl.BlockSpec((1, d, tn), lambda i, j: (i, 0, j)),
                  pl.BlockSpec((1, 1, tn), lambda i, j: (i, 0, j))],
        out_specs=pl.BlockSpec((1, bsz, tn), lambda i, j: (i, 0, j)),
        out_shape=jax.ShapeDtypeStruct((depth, bsz, n), F32),
        compiler_params=_cparams("arbitrary", "arbitrary"),
        name="ada_params",
    )(c, ada_w, ada_b.reshape(depth, 1, n))


def _in_proj_kernel(*refs, has_add, splits):
    if has_add:
        x_ref, ya_ref, yb_ref, g_ref, gate_ref, nw_ref, sh_ref, sc_ref, w_ref = refs[:9]
        outs = refs[9:]
        x = x_ref[...]
        g = g_ref[...]
        moe = g[:, 0:1] * ya_ref[...].astype(F32) + g[:, 1:2] * yb_ref[...].astype(F32)
        x = x + gate_ref[0] * moe
        outs[0][...] = x
        outs = outs[1:]
    else:
        x_ref, nw_ref, sh_ref, sc_ref, w_ref = refs[:5]
        outs = refs[5:]
        x = x_ref[...]
    h = _rms_mod(x, nw_ref[...], sh_ref[0], sc_ref[0]).astype(BF16)
    for o_ref, (c0, c1) in zip(outs, splits):
        o_ref[...] = jnp.dot(h, w_ref[:, c0:c1], preferred_element_type=F32).astype(o_ref.dtype)


def _in_proj(x, nw, shift, scale, w, splits, dtypes, seq, tm, add=None):
    n_tok, d = x.shape
    tpb = seq // tm
    row = lambda i: (i, 0)
    bat = lambda i: (i // tpb, 0, 0)
    fix = lambda i: (0, 0)
    in_specs = [pl.BlockSpec((tm, d), row)]
    args = [x]
    out_shape = []
    out_specs = []
    if add is not None:
        ya, yb, g, gate = add
        in_specs += [pl.BlockSpec((tm, d), row), pl.BlockSpec((tm, d), row),
                     pl.BlockSpec((tm, g.shape[1]), row), pl.BlockSpec((1, 1, d), bat)]
        args += [ya, yb, g, gate]
        out_shape.append(jax.ShapeDtypeStruct((n_tok, d), F32))
        out_specs.append(pl.BlockSpec((tm, d), row))
    in_specs += [pl.BlockSpec((1, d), fix), pl.BlockSpec((1, 1, d), bat), pl.BlockSpec((1, 1, d), bat),
                 pl.BlockSpec(w.shape, fix)]
    args += [nw, shift, scale, w]
    for (c0, c1), dt in zip(splits, dtypes):
        out_shape.append(jax.ShapeDtypeStruct((n_tok, c1 - c0), dt))
        out_specs.append(pl.BlockSpec((tm, c1 - c0), row))
    return pl.pallas_call(
        functools.partial(_in_proj_kernel, has_add=add is not None, splits=splits),
        grid=(n_tok // tm,),
        in_specs=in_specs, out_specs=out_specs, out_shape=out_shape,
        compiler_params=_cparams("arbitrary"),
        name="in_proj",
    )(*args)


def _s5_tables(a_re, a_im, log_dt, b_re, b_im, c_re, c_im, d_skip):
    hp = lax.Precision.HIGHEST
    L = S5_L
    a_re = jnp.minimum(a_re.astype(F32), -1e-4)
    a_im = a_im.astype(F32)
    dt = jnp.exp(log_dt.astype(F32))[:, None]
    mag = jnp.exp(dt * a_re)
    abar_re, abar_im = mag * jnp.cos(dt * a_im), mag * jnp.sin(dt * a_im)
    den = a_re * a_re + a_im * a_im
    num_re, num_im = abar_re - 1.0, abar_im
    coef_re = (num_re * a_re + num_im * a_im) / den
    coef_im = (num_im * a_re - num_re * a_im) / den
    b_re = b_re.astype(F32)
    b_im = b_im.astype(F32)
    bb_re = coef_re[..., None] * b_re - coef_im[..., None] * b_im
    bb_im = coef_re[..., None] * b_im + coef_im[..., None] * b_re
    c_re = c_re.astype(F32)
    c_im = c_im.astype(F32)
    tau = jnp.arange(L + 1, dtype=F32)[None, :, None]
    pmag = jnp.exp(tau * (dt * a_re)[:, None, :])
    pw_re = pmag * jnp.cos(tau * (dt * a_im)[:, None, :])
    pw_im = pmag * jnp.sin(tau * (dt * a_im)[:, None, :])
    cl_re = c_re[:, None] * pw_re[:, :, None, :] - c_im[:, None] * pw_im[:, :, None, :]
    cl_im = c_re[:, None] * pw_im[:, :, None, :] + c_im[:, None] * pw_re[:, :, None, :]
    taps = (jnp.einsum('gtpn,gnq->gtpq', cl_re[:, :L], bb_re, precision=hp)
            - jnp.einsum('gtpn,gnq->gtpq', cl_im[:, :L], bb_im, precision=hp))
    s_idx = jnp.arange(L)[:, None]
    t_idx = jnp.arange(L)[None, :]
    lag = jnp.clip(t_idx - s_idx, 0, L - 1)
    sel = taps[:, lag]
    sel = jnp.where((t_idx >= s_idx)[None, :, :, None, None], sel, 0.0)
    toep = sel.transpose(0, 1, 4, 2, 3).reshape(S5_G, L * S5_P, L * S5_P)
    rev = L - 1 - jnp.arange(L)
    e_re = pw_re[:, rev][:, :, :, None] * bb_re[:, None] - pw_im[:, rev][:, :, :, None] * bb_im[:, None]
    e_im = pw_re[:, rev][:, :, :, None] * bb_im[:, None] + pw_im[:, rev][:, :, :, None] * bb_re[:, None]
    emap = jnp.concatenate([e_re.transpose(0, 1, 3, 2), e_im.transpose(0, 1, 3, 2)], axis=-1)
    emap = emap.reshape(S5_G, L * S5_P, 2 * S5_N)
    q_re = cl_re[:, 1:L + 1].transpose(0, 3, 1, 2).reshape(S5_G, S5_N, L * S5_P)
    q_im = -cl_im[:, 1:L + 1].transpose(0, 3, 1, 2).reshape(S5_G, S5_N, L * S5_P)
    qmap = jnp.concatenate([q_re, q_im], axis=1)
    lam_a = jnp.concatenate([pw_re[:, L], pw_re[:, L]], axis=-1)[:, None, :]
    lam_b = jnp.concatenate([-pw_im[:, L], pw_im[:, L]], axis=-1)[:, None, :]
    dvec = jnp.tile(d_skip.astype(F32), (1, L))[:, None, :]
    return toep.astype(BF16), emap.astype(BF16), qmap, lam_a, lam_b, dvec


def _s5_kernel(u_ref, toep_ref, emap_ref, qmap_ref, la_ref, lb_ref, dv_ref, o_ref, h_ref, *, n_chunks, bsz):
    u = u_ref[0]
    h_ref[...] = jnp.dot(u, emap_ref[0], preferred_element_type=F32)
    la = la_ref[0]
    lb = lb_ref[0]

    def step(c, h):
        rows = pl.ds(pl.multiple_of(c * bsz, bsz), bsz)
        e = h_ref[rows, :]
        h_ref[rows, :] = h
        return h * la + pltpu.roll(h, S5_N, axis=1) * lb + e

    lax.fori_loop(0, n_chunks, step, jnp.zeros((bsz, 2 * S5_N), F32))
    y = jnp.dot(u, toep_ref[0], preferred_element_type=F32)
    y = y + _dot3(h_ref[...], qmap_ref[0])
    y = y + u.astype(F32) * dv_ref[0]
    o_ref[0] = y.astype(o_ref.dtype)


def _s5_scan(u, tables, bsz, seq):
    toep, emap, qmap, lam_a, lam_b, dvec = tables
    L = S5_L
    nc = seq // L
    rows = nc * bsz
    lp = L * S5_P
    ug = u.reshape(bsz, nc, L, S5_G, S5_P).transpose(3, 1, 0, 2, 4).reshape(S5_G, rows, lp)
    grp = lambda g: (g, 0, 0)
    yg = pl.pallas_call(
        functools.partial(_s5_kernel, n_chunks=nc, bsz=bsz),
        grid=(S5_G,),
        in_specs=[pl.BlockSpec((1, rows, lp), grp), pl.BlockSpec((1, lp, lp), grp),
                  pl.BlockSpec((1, lp, 2 * S5_N), grp), pl.BlockSpec((1, 2 * S5_N, lp), grp),
                  pl.BlockSpec((1, 1, 2 * S5_N), grp), pl.BlockSpec((1, 1, 2 * S5_N), grp),
                  pl.BlockSpec((1, 1, lp), grp)],
        out_specs=pl.BlockSpec((1, rows, lp), grp),
        out_shape=jax.ShapeDtypeStruct((S5_G, rows, lp), BF16),
        scratch_shapes=[pltpu.VMEM((rows, 2 * S5_N), F32)],
        compiler_params=_cparams("arbitrary"),
        name="s5_scan",
    )(ug, toep, emap, qmap, lam_a, lam_b, dvec)
    return yg.reshape(S5_G, nc, bsz, L, S5_P).transpose(2, 1, 3, 0, 4).reshape(bsz * seq, S5_WIDTH)


def _sb_kernel(q_ref, k_ref, v_ref, to_ref, o_ref, acc_ref, car_ref, *, tq, tk):
    qb = pl.program_id(2)
    q = q_ref[...] * 0.125
    lane = lax.broadcasted_iota(jnp.int32, (tq, LANES), 1)
    zero = jnp.zeros_like(q)
    qh = (jnp.where(lane < SB_HEAD, q, zero), jnp.where(lane >= SB_HEAD, q, zero))
    to = to_ref[...]
    acc_ref[...] = jnp.zeros_like(acc_ref)
    car_ref[...] = jnp.zeros_like(car_ref)
    ratio = tq // tk
    t_idx = lax.broadcasted_iota(jnp.int32, (tq, tk), 0)
    s_idx = lax.broadcasted_iota(jnp.int32, (tq, tk), 1)

    def block(kb, masked):
        rows = pl.ds(pl.multiple_of(kb * tk, tk), tk)
        kblk = k_ref[rows, :]
        vblk = v_ref[rows, :]
        if masked:
            mask = (s_idx + kb * tk) < (t_idx + qb * tq)
        for h in range(2):
            z = lax.dot_general(qh[h], kblk, (((1,), (1,)), ((), ())), preferred_element_type=F32)
            lsp = jnp.minimum(z, 0.0) - jnp.log(1.0 + jnp.exp(-jnp.abs(z)))
            lsn = lsp - z
            if masked:
                lsn = jnp.where(mask, lsn, 0.0)
            cs = _dot_exact_rhs(lsn, to)
            w = jnp.exp(lsp + cs[:, :tk] + car_ref[h])
            if masked:
                w = jnp.where(mask, w, 0.0)
            acc_ref[h] += jnp.dot(w.astype(BF16), vblk, preferred_element_type=F32)
            car_ref[h] += cs[:, tk:]

    for j in range(ratio):
        block(qb * ratio + (ratio - 1 - j), True)

    def body(i, carry):
        block(qb * ratio - 1 - i, False)
        return carry

    lax.fori_loop(0, qb * ratio, body, 0)
    o_ref[...] = jnp.where(lane < SB_HEAD, acc_ref[0], acc_ref[1]).astype(o_ref.dtype)


def _sb_attention(q, k, v, bsz, seq):
    tq, tk = SB_TQ, SB_TK
    nq = seq // tq
    pairs = SB_WIDTH // LANES
    tri = (jnp.arange(tk)[:, None] > jnp.arange(tk)[None, :]).astype(BF16)
    to = jnp.concatenate([tri, jnp.ones((tk, tk), BF16)], axis=1)
    return pl.pallas_call(
        functools.partial(_sb_kernel, tq=tq, tk=tk),
        grid=(bsz, pairs, nq),
        in_specs=[pl.BlockSpec((tq, LANES), lambda b, p, i: (b * nq + i, p)),
                  pl.BlockSpec((seq, LANES), lambda b, p, i: (b, p)),
                  pl.BlockSpec((seq, LANES), lambda b, p, i: (b, p)),
                  pl.BlockSpec((tk, 2 * tk), lambda b, p, i: (0, 0))],
        out_specs=pl.BlockSpec((tq, LANES), lambda b, p, i: (b * nq + i, p)),
        out_shape=jax.ShapeDtypeStruct((bsz * seq, SB_WIDTH), BF16),
        scratch_shapes=[pltpu.VMEM((2, tq, LANES), F32), pltpu.VMEM((2, tq, tk), F32)],
        compiler_params=_cparams("arbitrary", "arbitrary", "arbitrary"),
        name="sb_attention",
    )(q, k, v, to)


def _mix_out_kernel(*refs, glu):
    if glu:
        ya_ref, yb_ref, x_ref, gate_ref, wglu_ref, wo_ref, nw_ref, sh_ref, sc_ref, wr_ref, br_ref = refs[:11]
        outs = refs[11:]
        a = _gelu_tanh(ya_ref[...].astype(F32)).astype(BF16)
        g = jnp.dot(a, wglu_ref[...], preferred_element_type=F32)
        half = g.shape[1] // 2
        ya = (g[:, :half] * _sigmoid(g[:, half:])).astype(BF16)
    else:
        ya_ref, yb_ref, x_ref, gate_ref, wo_ref, nw_ref, sh_ref, sc_ref, wr_ref, br_ref = refs[:10]
        outs = refs[10:]
        ya = ya_ref[...]
    x1_ref, h2_ref, lg_ref = outs
    ka = ya.shape[1]
    mix = (jnp.dot(ya, wo_ref[:ka, :], preferred_element_type=F32)
           + jnp.dot(yb_ref[...], wo_ref[ka:, :], preferred_element_type=F32))
    x1 = x_ref[...] + gate_ref[0] * mix
    x1_ref[...] = x1
    h2 = _rms_mod(x1, nw_ref[...], sh_ref[0], sc_ref[0])
    h2_ref[...] = h2.astype(BF16)
    lg_ref[...] = _dot3(h2, wr_ref[...]) + br_ref[...]


def _mix_out(ya, yb, x, gate, w_glu, w_out, nw, shift, scale, w_router, b_router, seq, tm):
    n_tok, d = x.shape
    tpb = seq // tm
    row = lambda i: (i, 0)
    bat = lambda i: (i // tpb, 0, 0)
    fix = lambda i: (0, 0)
    in_specs = [pl.BlockSpec((tm, ya.shape[1]), row), pl.BlockSpec((tm, yb.shape[1]), row),
                pl.BlockSpec((tm, d), row), pl.BlockSpec((1, 1, d), bat)]
    args = [ya, yb, x, gate]
    if w_glu is not None:
        in_specs.append(pl.BlockSpec(w_glu.shape, fix))
        args.append(w_glu)
    in_specs += [pl.BlockSpec(w_out.shape, fix), pl.BlockSpec((1, d), fix), pl.BlockSpec((1, 1, d), bat),
                 pl.BlockSpec((1, 1, d), bat), pl.BlockSpec(w_router.shape, fix), pl.BlockSpec(b_router.shape, fix)]
    args += [w_out, nw, shift, scale, w_router, b_router]
    return pl.pallas_call(
        functools.partial(_mix_out_kernel, glu=w_glu is not None),
        grid=(n_tok // tm,),
        in_specs=in_specs,
        out_specs=[pl.BlockSpec((tm, d), row), pl.BlockSpec((tm, d), row), pl.BlockSpec((tm, LANES), row)],
        out_shape=[jax.ShapeDtypeStruct((n_tok, d), F32), jax.ShapeDtypeStruct((n_tok, d), BF16),
                   jax.ShapeDtypeStruct((n_tok, LANES), F32)],
        compiler_params=_cparams("arbitrary"),
        name="mix_out",
    )(*args)


def _route(logits):
    n_tok = logits.shape[0]
    grp_logits = logits[:, :MOE_GROUPS]
    grp_prob = jax.nn.softmax(grp_logits, axis=-1)
    grp_idx = jnp.argmax(grp_prob, axis=-1)
    grp_p = jnp.max(grp_prob, axis=-1)
    exp_logits = logits[:, MOE_GROUPS:MOE_GROUPS + MOE_EXPERTS].reshape(n_tok, MOE_GROUPS, MOE_PER_GROUP)
    in_grp = jnp.take_along_axis(exp_logits, grp_idx[:, None, None], axis=1)[:, 0]
    i0 = jnp.argmax(in_grp, axis=-1)
    l0 = jnp.max(in_grp, axis=-1)
    rest = jnp.where(jnp.arange(MOE_PER_GROUP)[None, :] == i0[:, None], -jnp.inf, in_grp)
    i1 = jnp.argmax(rest, axis=-1)
    l1 = jnp.max(rest, axis=-1)
    e1 = jnp.exp(l1 - l0)
    g0 = grp_p / (1.0 + e1)
    g1 = grp_p * e1 / (1.0 + e1)
    experts = jnp.stack([grp_idx * MOE_PER_GROUP + i0, grp_idx * MOE_PER_GROUP + i1], axis=-1).astype(jnp.int32)
    return experts, jnp.stack([g0, g1], axis=-1)


def _dispatch(experts, n_tok):
    n_assign = n_tok * MOE_TOPK
    n_blocks = (n_assign + MOE_EXPERTS * (MOE_BLK - 1) + MOE_BLK - 1) // MOE_BLK
    flat_e = experts.reshape(-1)
    onehot = (flat_e[:, None] == jnp.arange(MOE_EXPERTS, dtype=jnp.int32)[None, :]).astype(jnp.int32)
    csum = jnp.cumsum(onehot, axis=0)
    rank = jnp.take_along_axis(csum, flat_e[:, None], axis=1)[:, 0] - 1
    counts = csum[-1]
    padded = ((counts + MOE_BLK - 1) // MOE_BLK) * MOE_BLK
    pad_end = jnp.cumsum(padded)
    pad_start = pad_end - padded
    dest = (pad_start[flat_e] + rank).astype(jnp.int32)
    flat_tok = jnp.repeat(jnp.arange(n_tok, dtype=jnp.int32), MOE_TOPK)
    row_tok = jnp.full((n_blocks * MOE_BLK,), n_tok, jnp.int32).at[dest].set(flat_tok)
    blk_expert = jnp.minimum(jnp.searchsorted(pad_end, jnp.arange(n_blocks) * MOE_BLK, side='right'),
                             MOE_EXPERTS - 1).astype(jnp.int32)
    n_used = (pad_end[-1] // MOE_BLK).astype(jnp.int32).reshape(1)
    return dest.reshape(n_tok, MOE_TOPK), row_tok, blk_expert, n_used, n_blocks


def _moe_kernel(be_ref, nu_ref, x_ref, wg_ref, wu_ref, wd_ref, o_ref, wg_s, wu_s, wd_s):
    i = pl.program_id(0)
    e = be_ref[i]
    prev = be_ref[jnp.maximum(i - 1, 0)]

    @pl.when((i == 0) | (e != prev))
    def _():
        wg_s[...] = wg_ref[0].astype(BF16)
        wu_s[...] = wu_ref[0].astype(BF16)
        wd_s[...] = wd_ref[0].astype(BF16)

    @pl.when(i < nu_ref[0])
    def _():
        x = x_ref[...]
        a = jnp.dot(x, wg_s[...], preferred_element_type=F32)
        b = jnp.dot(x, wu_s[...], preferred_element_type=F32)
        hid = (_silu(a) * b).astype(BF16)
        o_ref[...] = jnp.dot(hid, wd_s[...], preferred_element_type=F32).astype(o_ref.dtype)

    @pl.when(i >= nu_ref[0])
    def _():
        o_ref[...] = jnp.zeros_like(o_ref)


def _moe_experts(xs, blk_expert, n_used, w_gate, w_up, w_down, n_blocks):
    d = xs.shape[1]
    hid = w_gate.shape[2]
    grid_spec = pltpu.PrefetchScalarGridSpec(
        num_scalar_prefetch=2,
        grid=(n_blocks,),
        in_specs=[pl.BlockSpec((MOE_BLK, d), lambda i, be, nu: (i, 0)),
                  pl.BlockSpec((1, d, hid), lambda i, be, nu: (be[i], 0, 0)),
                  pl.BlockSpec((1, d, hid), lambda i, be, nu: (be[i], 0, 0)),
                  pl.BlockSpec((1, hid, d), lambda i, be, nu: (be[i], 0, 0))],
        out_specs=pl.BlockSpec((MOE_BLK, d), lambda i, be, nu: (i, 0)),
        scratch_shapes=[pltpu.VMEM((d, hid), BF16), pltpu.VMEM((d, hid), BF16), pltpu.VMEM((hid, d), BF16)],
    )
    return pl.pallas_call(
        _moe_kernel,
        grid_spec=grid_spec,
        out_shape=jax.ShapeDtypeStruct((n_blocks * MOE_BLK, d), BF16),
        compiler_params=_cparams("arbitrary"),
        name="moe_experts",
    )(blk_expert, n_used, xs, w_gate, w_up, w_down)


def _moe(h2, logits, w_gate, w_up, w_down):
    n_tok, d = h2.shape
    experts, gates = _route(logits)
    dest, row_tok, blk_expert, n_used, n_blocks = _dispatch(experts, n_tok)
    h_pad = jnp.concatenate([h2, jnp.zeros((1, d), h2.dtype)], axis=0)
    xs = jnp.take(h_pad, row_tok, axis=0)
    ys = _moe_experts(xs, blk_expert, n_used, w_gate, w_up, w_down, n_blocks)
    return jnp.take(ys, dest[:, 0], axis=0), jnp.take(ys, dest[:, 1], axis=0), gates


def _rw_kernel(p_ref, mu_ref, pv_ref, wwa_ref, g2_ref, bd_ref, tri_ref, o_ref, st_ref, last_ref, *, tt):
    i = pl.program_id(1)
    L = RW_L
    W = RW_WIDTH

    @pl.when(i == 0)
    def _():
        st_ref[...] = jnp.zeros_like(st_ref)
        last_ref[...] = jnp.zeros_like(last_ref)

    p = p_ref[...]
    row = lax.broadcasted_iota(jnp.int32, p.shape, 0)
    prev = jnp.where(row == 0, jnp.broadcast_to(last_ref[0:1, :], p.shape), pltpu.roll(p, 1, axis=0))
    last_ref[0:1, :] = p[tt - 1:tt, :]
    p = p + (prev - p) * mu_ref[...]
    r = p[:, 0:W]
    k = p[:, W:2 * W]
    v = p[:, 2 * W:3 * W]
    lw = p[:, 3 * W:3 * W + LANES]
    xg = p[:, 3 * W + LANES:3 * W + 2 * LANES]
    w0, a0, k_k, k_a, r_k, ln_w, ln_b = (pv_ref[j:j + 1, :] for j in range(7))
    lane = lax.broadcasted_iota(jnp.int32, lw.shape, 1)
    wa = _bdot(jnp.where(lane < 64, jnp.tanh(lw), lw), wwa_ref[...])
    w = -_softplus(-(w0 + wa[:, :W])) - 0.5
    logd = -jnp.exp(w)
    lr = _sigmoid(a0 + wa[:, W:])
    gate = _bdot(_sigmoid(xg), g2_ref[...])
    bd = bd_ref[...]
    kk = k * k_k
    kk = kk / jnp.maximum(jnp.sqrt(_dot_exact_rhs(kk * kk, bd)), 1e-12)
    k = k * (1.0 + (lr - 1.0) * k_a)
    av = -kk
    bv = kk * lr

    tri = tri_ref[...]
    lane_p = lax.broadcasted_iota(jnp.int32, (L, LANES), 1)
    m0 = lane_p < RW_HEAD
    ri = lax.broadcasted_iota(jnp.int32, (2 * L, 2 * L), 0)
    ci = lax.broadcasted_iota(jnp.int32, (2 * L, 2 * L), 1)
    same = (ri // L) == (ci // L)
    mask_s = same & (ci < ri)
    mask_i = same & (ci <= ri)
    eye = (ri == ci).astype(F32)

    def stack(x):
        return jnp.concatenate([jnp.where(m0, x, 0.0), jnp.where(m0, 0.0, x)], axis=0)

    for c in range(tt // L):
        rows = slice(c * L, (c + 1) * L)
        ld = logd[rows]
        cs = _dot_exact_lhs(tri, ld)
        total = cs[L - 1:L, :]
        e_in = jnp.exp(cs)
        e_out = jnp.exp(-cs)
        e_prev = jnp.exp(cs - ld)
        e_end = jnp.exp(total - cs)
        at = av[rows] * e_prev
        rt = r[rows] * e_in
        bt = bv[rows] * e_out
        kt = k[rows] * e_out
        bg = bv[rows] * e_end
        kg = k[rows] * e_end
        gam = jnp.exp(total)
        y_parts = []
        for hp in range(RW_HEADS // 2):
            ls = slice(hp * LANES, (hp + 1) * LANES)
            aa = stack(at[:, ls])
            rr = stack(rt[:, ls])
            vv = stack(v[rows, ls])
            bt2 = jnp.concatenate([bt[:, ls], bt[:, ls]], axis=0)
            kt2 = jnp.concatenate([kt[:, ls], kt[:, ls]], axis=0)
            n_ab = jnp.where(mask_s, _bdot_nt(aa, bt2), 0.0)
            n_ak = jnp.where(mask_s, _bdot_nt(aa, kt2), 0.0)
            n_rb = jnp.where(mask_i, _bdot_nt(rr, bt2), 0.0)
            n_rk = jnp.where(mask_i, _bdot_nt(rr, kt2), 0.0)
            tinv = eye + n_ab
            pw = n_ab
            for _ in range(int(math.log2(L)) - 1):
                pw = _bdot(pw, pw)
                tinv = tinv + _bdot(tinv, pw)
            tav = _bdot(tinv, _bdot(n_ak, vv))
            s = st_ref[hp]
            x1 = _bdot_nt(aa, s)
            x2 = _bdot_nt(rr, s)
            u = _bdot(tinv, x1) + tav
            yst = x2 + _bdot(n_rb, u) + _bdot(n_rk, vv)
            y_parts.append(yst[:L] + yst[L:])
            st_ref[hp] = s * gam[:, ls] + _bdot_tn(u, stack(bg[:, ls])) + _bdot_tn(vv, stack(kg[:, ls]))
        y = jnp.concatenate(y_parts, axis=1)
        mean = _dot_exact_rhs(y, bd) * (1.0 / RW_HEAD)
        yc = y - mean
        var = _dot_exact_rhs(yc * yc, bd) * (1.0 / RW_HEAD)
        yn = yc * lax.rsqrt(var + RW_LN_EPS) * ln_w + ln_b
        bonus = _dot_exact_rhs(r[rows] * k[rows] * r_k, bd) * v[rows]
        o_ref[rows, :] = ((yn + bonus) * gate[rows]).astype(o_ref.dtype)


def _rwkv7(p, mu, pv, wwa, g2, bsz, seq):
    tt = RW_TT
    nt = seq // tt
    L = RW_L
    hid = jnp.arange(RW_WIDTH) // RW_HEAD
    bd = (hid[:, None] == hid[None, :]).astype(BF16)
    tri = (jnp.arange(L)[:, None] >= jnp.arange(L)[None, :]).astype(BF16)
    fix = lambda b, i: (0, 0)
    return pl.pallas_call(
        functools.partial(_rw_kernel, tt=tt),
        grid=(bsz, nt),
        in_specs=[pl.BlockSpec((tt, RW_IN), lambda b, i: (b * nt + i, 0)),
                  pl.BlockSpec(mu.shape, fix), pl.BlockSpec(pv.shape, fix), pl.BlockSpec(wwa.shape, fix),
                  pl.BlockSpec(g2.shape, fix), pl.BlockSpec(bd.shape, fix), pl.BlockSpec(tri.shape, fix)],
        out_specs=pl.BlockSpec((tt, RW_WIDTH), lambda b, i: (b * nt + i, 0)),
        out_shape=jax.ShapeDtypeStruct((bsz * seq, RW_WIDTH), BF16),
        scratch_shapes=[pltpu.VMEM((RW_HEADS // 2, 2 * RW_HEAD, LANES), F32), pltpu.VMEM((8, RW_IN), F32)],
        compiler_params=_cparams("arbitrary", "arbitrary"),
        name="rwkv7",
    )(p, mu, pv, wwa, g2, bd, tri)


def _ssd_kernel(p_ref, cw_ref, cb_ref, hv_ref, dl_ref, nw_ref, tri_ref, o_ref, st_ref, tail_ref, *, L):
    i = pl.program_id(1)

    @pl.when(i == 0)
    def _():
        st_ref[...] = jnp.zeros_like(st_ref)
        tail_ref[...] = jnp.zeros_like(tail_ref)

    z = p_ref[:, 0:M2_INNER]
    xin = p_ref[:, M2_INNER:M2_INNER + M2_CONV_DIM]
    dt_raw = p_ref[:, M2_INNER + M2_CONV_DIM:M2_INNER + M2_CONV_DIM + LANES]
    tail = tail_ref[...]
    tail_ref[...] = xin[L - 8:L, :]
    row8 = lax.broadcasted_iota(jnp.int32, (8, M2_CONV_DIM), 0)
    conv = xin * cw_ref[M2_CONV - 1:M2_CONV, :] + cb_ref[...]
    for j in range(1, M2_CONV):
        rolled = pltpu.roll(xin, j, axis=0)
        head = jnp.where(row8 < j, pltpu.roll(tail, j, axis=0), rolled[0:8])
        shifted = jnp.concatenate([head, rolled[8:]], axis=0)
        conv = conv + shifted * cw_ref[M2_CONV - 1 - j:M2_CONV - j, :]
    xbc = _silu(conv)
    xs = xbc[:, 0:M2_INNER]
    dt = _softplus(dt_raw + hv_ref[0:1, :])
    adt = dt * hv_ref[1:2, :]
    cs = _dot_exact_lhs(tri_ref[...], adt)
    cs_t = jnp.transpose(cs)
    total = cs[L - 1:L, :]
    dec_in = jnp.exp(cs)
    dec_out = jnp.exp(total - cs)
    dec_all = jnp.exp(total)
    lane = lax.broadcasted_iota(jnp.int32, (L, LANES), 1)
    first = lane < 64
    rowp = lax.broadcasted_iota(jnp.int32, (LANES, M2_STATE), 0) < 64
    li = lax.broadcasted_iota(jnp.int32, (L, L), 0)
    si = lax.broadcasted_iota(jnp.int32, (L, L), 1)
    causal = li >= si
    y_parts = []
    for hp in range(M2_HEADS // 2):
        g = hp // 2
        h0, h1 = 2 * hp, 2 * hp + 1
        bm = xbc[:, M2_INNER + g * M2_STATE:M2_INNER + (g + 1) * M2_STATE]
        cm = xbc[:, M2_INNER + 2 * M2_STATE + g * M2_STATE:M2_INNER + 2 * M2_STATE + (g + 1) * M2_STATE]
        x_p = xs[:, hp * LANES:(hp + 1) * LANES]
        sel = lambda a: jnp.where(first, a[:, h0:h0 + 1], a[:, h1:h1 + 1])
        xd = x_p * sel(dt)
        cb = _bdot_nt(cm, bm)
        yd = []
        for h in (h0, h1):
            lmat = jnp.where(causal, jnp.exp(cs[:, h:h + 1] - cs_t[h:h + 1, :]), 0.0)
            yd.append(_bdot(cb * lmat, xd))
        s = st_ref[hp]
        y_off = _bdot_nt(cm, s) * sel(dec_in)
        y_parts.append(jnp.where(first, yd[0], yd[1]) + y_off)
        s_new = _bdot_tn(xd * sel(dec_out), bm)
        st_ref[hp] = s * jnp.where(rowp, dec_all[:, h0:h0 + 1], dec_all[:, h1:h1 + 1]) + s_new
    y = jnp.concatenate(y_parts, axis=1) + dl_ref[...] * xs
    y = y * _silu(z)
    half = M2_INNER // 2
    outs = []
    for g in range(2):
        yg = y[:, g * half:(g + 1) * half]
        outs.append(yg * lax.rsqrt(jnp.mean(yg * yg, axis=-1, keepdims=True) + RMS_EPS))
    o_ref[...] = (jnp.concatenate(outs, axis=1) * nw_ref[...]).astype(o_ref.dtype)


def _ssd(p, conv_w, conv_b, hv, d_lanes, norm_w, bsz, seq):
    L = M2_L
    nt = seq // L
    tri = (jnp.arange(L)[:, None] >= jnp.arange(L)[None, :]).astype(BF16)
    fix = lambda b, i: (0, 0)
    return pl.pallas_call(
        functools.partial(_ssd_kernel, L=L),
        grid=(bsz, nt),
        in_specs=[pl.BlockSpec((L, M2_IN_PAD), lambda b, i: (b * nt + i, 0)),
                  pl.BlockSpec(conv_w.shape, fix), pl.BlockSpec(conv_b.shape, fix), pl.BlockSpec(hv.shape, fix),
                  pl.BlockSpec(d_lanes.shape, fix), pl.BlockSpec(norm_w.shape, fix), pl.BlockSpec(tri.shape, fix)],
        out_specs=pl.BlockSpec((L, M2_INNER), lambda b, i: (b * nt + i, 0)),
        out_shape=jax.ShapeDtypeStruct((bsz * seq, M2_INNER), BF16),
        scratch_shapes=[pltpu.VMEM((M2_HEADS // 2, LANES, M2_STATE), F32), pltpu.VMEM((8, M2_CONV_DIM), F32)],
        compiler_params=_cparams("arbitrary", "arbitrary"),
        name="ssd",
    )(p, conv_w, conv_b, hv, d_lanes, norm_w, tri)


def _final_kernel(x_ref, ya_ref, yb_ref, g_ref, gate_ref, nw_ref, o_ref):
    g = g_ref[...]
    moe = g[:, 0:1] * ya_ref[...].astype(F32) + g[:, 1:2] * yb_ref[...].astype(F32)
    x = x_ref[...] + gate_ref[0] * moe
    ms = jnp.mean(x * x, axis=-1, keepdims=True)
    o_ref[...] = x * lax.rsqrt(ms + RMS_EPS) * nw_ref[...]


def _final(x, ya, yb, g, gate, nw, seq, tm):
    n_tok, d = x.shape
    tpb = seq // tm
    row = lambda i: (i, 0)
    return pl.pallas_call(
        _final_kernel,
        grid=(n_tok // tm,),
        in_specs=[pl.BlockSpec((tm, d), row), pl.BlockSpec((tm, d), row), pl.BlockSpec((tm, d), row),
                  pl.BlockSpec((tm, g.shape[1]), row), pl.BlockSpec((1, 1, d), lambda i: (i // tpb, 0, 0)),
                  pl.BlockSpec((1, d), lambda i: (0, 0))],
        out_specs=pl.BlockSpec((tm, d), row),
        out_shape=jax.ShapeDtypeStruct((n_tok, d), F32),
        compiler_params=_cparams("arbitrary"),
        name="final_norm",
    )(x, ya, yb, g, gate, nw)


def _router_weights(w_grp, b_grp, w_exp, b_exp):
    d = w_grp.shape[0]
    pad = LANES - MOE_GROUPS - MOE_EXPERTS
    w = jnp.concatenate([w_grp, w_exp, jnp.zeros((d, pad), F32)], axis=1)
    b = jnp.concatenate([b_grp, b_exp, jnp.zeros((pad,), F32)])[None, :]
    return w, b


def kernel(x, c, ada_w, ada_b, norm_mix_w, norm_ffn_w, even_w_in, even_w_out, s5_a_re, s5_a_im, s5_log_dt, s5_b_re, s5_b_im, s5_c_re, s5_c_im, s5_d, s5_w_glu, odd_w_in, odd_w_out, rw_mu, rw_w0, rw_w2, rw_a0, rw_a2, rw_g2, rw_k_k, rw_k_a, rw_r_k, rw_ln_w, rw_ln_b, m2_conv_w, m2_conv_b, m2_dt_bias, m2_a_log, m2_d, m2_norm_w, moe_w_grp, moe_b_grp, moe_w_exp, moe_b_exp, moe_w_gate, moe_w_up, moe_w_down, final_norm_w):
    bsz, seq, d = x.shape
    n_tok = bsz * seq
    tm = 512
    xt = x.reshape(n_tok, d)
    ada = _ada_params(c, ada_w, ada_b)
    mods = [[ada[i, :, j * d:(j + 1) * d].reshape(bsz, 1, d) for j in range(6)] for i in range(2)]

    sh1, sc1, g1, sh2, sc2, g2 = mods[0]
    u, q, k, v = _in_proj(xt, norm_mix_w[0][None, :], sh1, sc1, even_w_in[0].astype(BF16),
                          ((0, 512), (512, 1024), (1024, 1536), (1536, 2048)), (BF16,) * 4, seq, tm)
    tables = _s5_tables(s5_a_re[0], s5_a_im[0], s5_log_dt[0], s5_b_re[0], s5_b_im[0], s5_c_re[0], s5_c_im[0], s5_d[0])
    y_s5 = _s5_scan(u, tables, bsz, seq)
    y_sb = _sb_attention(q, k, v, bsz, seq)
    wr, br = _router_weights(moe_w_grp[0], moe_b_grp[0], moe_w_exp[0], moe_b_exp[0])
    x1, h2, logits = _mix_out(y_s5, y_sb, xt, g1, s5_w_glu[0].astype(BF16), even_w_out[0].astype(BF16),
                              norm_ffn_w[0][None, :], sh2, sc2, wr, br, seq, tm)
    ya, yb, gates = _moe(h2, logits, moe_w_gate[0], moe_w_up[0], moe_w_down[0])

    gate_prev = g2
    sh1, sc1, g1, sh2, sc2, g2 = mods[1]
    w_in = jnp.concatenate([odd_w_in[0], jnp.zeros((d, M2_IN_PAD - M2_IN), F32)], axis=1).astype(BF16)
    x2, p_rw, p_m2 = _in_proj(x1, norm_mix_w[1][None, :], sh1, sc1, w_in,
                              ((0, RW_IN), (RW_IN, RW_IN + M2_IN_PAD)), (F32, F32), seq, 256,
                              add=(ya, yb, gates, gate_prev))
    pv = jnp.stack([rw_w0[0], rw_a0[0], rw_k_k[0], rw_k_a[0], rw_r_k[0].reshape(-1), rw_ln_w[0], rw_ln_b[0],
                    jnp.zeros((RW_WIDTH,), F32)])
    zl = jnp.zeros((64, RW_WIDTH), F32)
    wwa = jnp.concatenate([jnp.concatenate([rw_w2[0], zl], axis=1),
                           jnp.concatenate([zl, rw_a2[0]], axis=1)], axis=0).astype(BF16)
    y_rw = _rwkv7(p_rw, rw_mu[0][None, :], pv, wwa, rw_g2[0].astype(BF16), bsz, seq)
    hv = jnp.zeros((8, LANES), F32)
    hv = hv.at[0, :M2_HEADS].set(m2_dt_bias[0]).at[1, :M2_HEADS].set(-jnp.exp(m2_a_log[0]))
    y_m2 = _ssd(p_m2, m2_conv_w[0], m2_conv_b[0][None, :], hv, jnp.repeat(m2_d[0], 64)[None, :],
                m2_norm_w[0][None, :], bsz, seq)
    wr, br = _router_weights(moe_w_grp[1], moe_b_grp[1], moe_w_exp[1], moe_b_exp[1])
    x3, h2, logits = _mix_out(y_rw, y_m2, x2, g1, None, odd_w_out[0].astype(BF16),
                              norm_ffn_w[1][None, :], sh2, sc2, wr, br, seq, tm)
    ya, yb, gates = _moe(h2, logits, moe_w_gate[1], moe_w_up[1], moe_w_down[1])
    out = _final(x3, ya, yb, gates, g2, final_norm_w[None, :], seq, tm)
    return out.reshape(bsz, seq, d)
```

```python
import functools
import math

import jax
import jax.numpy as jnp
from jax import lax
from jax.experimental import pallas as pl
from jax.experimental.pallas import tpu as pltpu

F32 = jnp.float32
BF16 = jnp.bfloat16

D_MODEL = 1024
RMS_EPS = 1e-6
LANES = 128
VMEM_LIMIT = 56 * 1024 * 1024

S5_WIDTH = 512
S5_P = 16
S5_G = 32
S5_N = 64
S5_L = 16
S5_GB = 8
SB_WIDTH = 512
SB_HEAD = 64
SB_T = 256
SB_FADE = 160.0

RW_WIDTH = 512
RW_HEAD = 64
RW_HEADS = 8
RW_IN = 1792
RW_LN_EPS = 64e-5
RW_L = 64
RW_TT = 256

M2_INNER = 512
M2_HEADS = 8
M2_STATE = 128
M2_CONV = 4
M2_CONV_DIM = 1024
M2_IN = 1544
M2_IN_PAD = 1664
M2_L = 256

MOE_GROUPS = 4
MOE_PER_GROUP = 8
MOE_EXPERTS = 32
MOE_TOPK = 2
MOE_HIDDEN = 512
MOE_BLK = 256


def _cparams(*sem):
    return pltpu.CompilerParams(dimension_semantics=sem, vmem_limit_bytes=VMEM_LIMIT)


def _bdot(a, b):
    return jnp.dot(a.astype(BF16), b.astype(BF16), preferred_element_type=F32)


def _bdot_nt(a, b):
    return lax.dot_general(a.astype(BF16), b.astype(BF16), (((1,), (1,)), ((), ())),
                           preferred_element_type=F32)


def _bdot_tn(a, b):
    return lax.dot_general(a.astype(BF16), b.astype(BF16), (((0,), (0,)), ((), ())),
                           preferred_element_type=F32)


def _split2(a):
    hi = a.astype(BF16)
    lo = (a - hi.astype(F32)).astype(BF16)
    return hi, lo


def _dot3(a, b):
    ah, al = _split2(a)
    bh, bl = _split2(b)
    d = functools.partial(jnp.dot, preferred_element_type=F32)
    return d(ah, bh) + d(ah, bl) + d(al, bh)


def _dot_exact_rhs(a, b01):
    ah, al = _split2(a)
    d = functools.partial(jnp.dot, preferred_element_type=F32)
    return d(ah, b01) + d(al, b01)


def _dot_exact_lhs(a01, b):
    bh, bl = _split2(b)
    d = functools.partial(jnp.dot, preferred_element_type=F32)
    return d(a01, bh) + d(a01, bl)


def _sigmoid(x):
    return 1.0 / (1.0 + jnp.exp(-x))


def _softplus(x):
    return jnp.maximum(x, 0.0) + jnp.log(1.0 + jnp.exp(-jnp.abs(x)))


def _silu(x):
    return x * _sigmoid(x)


def _gelu_tanh(x):
    c = math.sqrt(2.0 / math.pi)
    return 0.5 * x * (1.0 + jnp.tanh(c * (x + 0.044715 * (x * x * x))))


def _rms_mod(x, nw, shift, scale):
    ms = jnp.mean(x * x, axis=-1, keepdims=True)
    return (x * lax.rsqrt(ms + RMS_EPS)) * nw * (1.0 + scale) + shift


def _ada_kernel(c_ref, w_ref, b_ref, o_ref):
    cond = _silu(c_ref[...])
    o_ref[0] = _dot3(cond, w_ref[0]) + b_ref[0]


def _ada_params(c, ada_w, ada_b):
    depth, d, n = ada_w.shape
    bsz = c.shape[0]
    tn = 1024
    return pl.pallas_call(
        _ada_kernel,
        grid=(depth, n // tn),
        in_specs=[pl.BlockSpec((bsz, d), lambda i, j: (0, 0)),
                  pl.BlockSpec((1, d, tn), lambda i, j: (i, 0, j)),
                  pl.BlockSpec((1, 1, tn), lambda i, j: (i, 0, j))],
        out_specs=pl.BlockSpec((1, bsz, tn), lambda i, j: (i, 0, j)),
        out_shape=jax.ShapeDtypeStruct((depth, bsz, n), F32),
        compiler_params=_cparams("arbitrary", "arbitrary"),
        name="ada_params",
    )(c, ada_w, ada_b.reshape(depth, 1, n))


def _in_proj_kernel(*refs, has_add, splits):
    if has_add:
        x_ref, ya_ref, yb_ref, g_ref, gate_ref, nw_ref, sh_ref, sc_ref, w_ref = refs[:9]
        outs = refs[9:]
        x = x_ref[...]
        g = g_ref[...]
        moe = g[:, 0:1] * ya_ref[...].astype(F32) + g[:, 1:2] * yb_ref[...].astype(F32)
        x = x + gate_ref[0] * moe
        outs[0][...] = x
        outs = outs[1:]
    else:
        x_ref, nw_ref, sh_ref, sc_ref, w_ref = refs[:5]
        outs = refs[5:]
        x = x_ref[...]
    h = _rms_mod(x, nw_ref[...], sh_ref[0], sc_ref[0]).astype(BF16)
    for o_ref, (c0, c1) in zip(outs, splits):
        o_ref[...] = jnp.dot(h, w_ref[:, c0:c1], preferred_element_type=F32).astype(o_ref.dtype)


def _in_proj(x, nw, shift, scale, w, splits, dtypes, seq, tm, add=None):
    n_tok, d = x.shape
    tpb = seq // tm
    row = lambda i: (i, 0)
    bat = lambda i: (i // tpb, 0, 0)
    fix = lambda i: (0, 0)
    in_specs = [pl.BlockSpec((tm, d), row)]
    args = [x]
    out_shape = []
    out_specs = []
    if add is not None:
        ya, yb, g, gate = add
        in_specs += [pl.BlockSpec((tm, d), row), pl.BlockSpec((tm, d), row),
                     pl.BlockSpec((tm, g.shape[1]), row), pl.BlockSpec((1, 1, d), bat)]
        args += [ya, yb, g, gate]
        out_shape.append(jax.ShapeDtypeStruct((n_tok, d), F32))
        out_specs.append(pl.BlockSpec((tm, d), row))
    in_specs += [pl.BlockSpec((1, d), fix), pl.BlockSpec((1, 1, d), bat), pl.BlockSpec((1, 1, d), bat),
                 pl.BlockSpec(w.shape, fix)]
    args += [nw, shift, scale, w]
    for (c0, c1), dt in zip(splits, dtypes):
        out_shape.append(jax.ShapeDtypeStruct((n_tok, c1 - c0), dt))
        out_specs.append(pl.BlockSpec((tm, c1 - c0), row))
    return pl.pallas_call(
        functools.partial(_in_proj_kernel, has_add=add is not None, splits=splits),
        grid=(n_tok // tm,),
        in_specs=in_specs, out_specs=out_specs, out_shape=out_shape,
        compiler_params=_cparams("arbitrary"),
        name="in_proj",
    )(*args)


def _s5_tables(a_re, a_im, log_dt, b_re, b_im, c_re, c_im, d_skip):
    hp = lax.Precision.HIGHEST
    L = S5_L
    a_re = jnp.minimum(a_re.astype(F32), -1e-4)
    a_im = a_im.astype(F32)
    dt = jnp.exp(log_dt.astype(F32))[:, None]
    mag = jnp.exp(dt * a_re)
    abar_re, abar_im = mag * jnp.cos(dt * a_im), mag * jnp.sin(dt * a_im)
    den = a_re * a_re + a_im * a_im
    num_re, num_im = abar_re - 1.0, abar_im
    coef_re = (num_re * a_re + num_im * a_im) / den
    coef_im = (num_im * a_re - num_re * a_im) / den
    b_re = b_re.astype(F32)
    b_im = b_im.astype(F32)
    bb_re = coef_re[..., None] * b_re - coef_im[..., None] * b_im
    bb_im = coef_re[..., None] * b_im + coef_im[..., None] * b_re
    c_re = c_re.astype(F32)
    c_im = c_im.astype(F32)
    tau = jnp.arange(L + 1, dtype=F32)[None, :, None]
    pmag = jnp.exp(tau * (dt * a_re)[:, None, :])
    pw_re = pmag * jnp.cos(tau * (dt * a_im)[:, None, :])
    pw_im = pmag * jnp.sin(tau * (dt * a_im)[:, None, :])
    cl_re = c_re[:, None] * pw_re[:, :, None, :] - c_im[:, None] * pw_im[:, :, None, :]
    cl_im = c_re[:, None] * pw_im[:, :, None, :] + c_im[:, None] * pw_re[:, :, None, :]
    taps = (jnp.einsum('gtpn,gnq->gtpq', cl_re[:, :L], bb_re, precision=hp)
            - jnp.einsum('gtpn,gnq->gtpq', cl_im[:, :L], bb_im, precision=hp))
    nb = S5_G // S5_GB
    eye = jnp.eye(S5_GB, dtype=F32)
    dlag = jnp.einsum('bgtpq,gh->btgqhp', taps.reshape(nb, S5_GB, L, S5_P, S5_P), eye)
    dlag = dlag.reshape(nb, L, LANES, LANES)
    rev = L - 1 - jnp.arange(L)
    e_re = pw_re[:, rev][:, :, :, None] * bb_re[:, None] - pw_im[:, rev][:, :, :, None] * bb_im[:, None]
    e_im = pw_re[:, rev][:, :, :, None] * bb_im[:, None] + pw_im[:, rev][:, :, :, None] * bb_re[:, None]
    e_c = jnp.stack([e_re, e_im], axis=1).reshape(nb, S5_GB, 2, L, S5_N, S5_P)
    emap = jnp.einsum('bgcsnq,gh->bsgqchn', e_c, eye).reshape(nb, L * LANES, 2 * S5_GB * S5_N)
    q_c = jnp.stack([cl_re[:, 1:L + 1], -cl_im[:, 1:L + 1]], axis=1).reshape(nb, S5_GB, 2, L, S5_P, S5_N)
    qmap = jnp.einsum('bgctpn,gh->bcgnthp', q_c, eye).reshape(nb, 2 * S5_GB * S5_N, L * LANES)
    lr = pw_re[:, L].reshape(nb, 1, S5_GB * S5_N)
    li = pw_im[:, L].reshape(nb, 1, S5_GB * S5_N)
    lam_a = jnp.concatenate([lr, lr], axis=-1)
    lam_b = jnp.concatenate([-li, li], axis=-1)
    dvec = jnp.tile(d_skip.astype(F32).reshape(nb, 1, LANES), (1, 1, L))
    return dlag.astype(BF16), emap.astype(BF16), qmap.astype(BF16), lam_a, lam_b, dvec


def _s5_kernel(u_ref, dlag_ref, emap_ref, qmap_ref, la_ref, lb_ref, dv_ref, o_ref, toep_ref, h_ref, *, n_rows):
    L = S5_L
    ns = 2 * S5_GB * S5_N

    @pl.when(pl.program_id(1) == 0)
    def _():
        toep_ref[...] = jnp.zeros_like(toep_ref)
        for s in range(L):
            for t in range(s, L):
                toep_ref[s * LANES:(s + 1) * LANES, t * LANES:(t + 1) * LANES] = dlag_ref[0, t - s]

    up = jnp.concatenate([u_ref[pl.ds(s, n_rows, stride=L), :] for s in range(L)], axis=1)
    ub = up.astype(BF16)
    h_ref[...] = jnp.dot(ub, emap_ref[0], preferred_element_type=F32)
    la = la_ref[0]
    lb = lb_ref[0]

    def step(c8, h):
        rows = pl.ds(pl.multiple_of(c8 * 8, 8), 8)
        e = h_ref[rows, :]
        ent = []
        for j in range(8):
            ent.append(h)
            h = h * la + pltpu.roll(h, ns // 2, axis=1) * lb + e[j:j + 1, :]
        h_ref[rows, :] = jnp.concatenate(ent, axis=0)
        return h

    lax.fori_loop(0, n_rows // 8, step, jnp.zeros((1, ns), F32))
    hi, lo = _split2(h_ref[...])
    qm = qmap_ref[0]
    y = jnp.dot(ub, toep_ref[...], preferred_element_type=F32)
    y = y + jnp.dot(hi, qm, preferred_element_type=F32) + jnp.dot(lo, qm, preferred_element_type=F32)
    y = y + up * dv_ref[0]
    for t in range(L):
        o_ref[pl.ds(t, n_rows, stride=L), :] = y[:, t * LANES:(t + 1) * LANES]


def _s5_scan(u, tables, bsz, seq):
    dlag, emap, qmap, lam_a, lam_b, dvec = tables
    L = S5_L
    n_rows = seq // L
    nb = S5_WIDTH // LANES
    ns = 2 * S5_GB * S5_N
    blk = lambda j, b: (j, 0, 0)
    return pl.pallas_call(
        functools.partial(_s5_kernel, n_rows=n_rows),
        grid=(nb, bsz),
        in_specs=[pl.BlockSpec((seq, LANES), lambda j, b: (b, j)),
                  pl.BlockSpec((1, L, LANES, LANES), lambda j, b: (j, 0, 0, 0)),
                  pl.BlockSpec((1, L * LANES, ns), blk), pl.BlockSpec((1, ns, L * LANES), blk),
                  pl.BlockSpec((1, 1, ns), blk), pl.BlockSpec((1, 1, ns), blk), pl.BlockSpec((1, 1, L * LANES), blk)],
        out_specs=pl.BlockSpec((seq, LANES), lambda j, b: (b, j)),
        out_shape=jax.ShapeDtypeStruct((bsz * seq, S5_WIDTH), F32),
        scratch_shapes=[pltpu.VMEM((L * LANES, L * LANES), BF16), pltpu.VMEM((n_rows, ns), F32)],
        compiler_params=_cparams("arbitrary", "arbitrary"),
        name="s5_scan",
    )(u, dlag, emap, qmap, lam_a, lam_b, dvec)


def _sb_kernel(q_ref, k_ref, v_ref, tri_ref, o_ref, qh_ref, hl_ref, z_ref, w_ref, acc_ref, car_ref, *, t):
    qb = pl.program_id(2)
    q = (q_ref[...].astype(F32) * (0.125 * math.log2(math.e))).astype(BF16)
    lane = lax.broadcasted_iota(jnp.int32, (t, LANES), 1)
    zero = jnp.zeros_like(q)
    qh_ref[0] = jnp.where(lane < SB_HEAD, q, zero)
    qh_ref[1] = jnp.where(lane >= SB_HEAD, q, zero)
    w_ref[0] = jnp.zeros(w_ref.shape[1:], BF16)
    acc_ref[...] = jnp.zeros_like(acc_ref)
    car_ref[...] = jnp.zeros_like(car_ref)

    def scores(kb, slot, diagonal):
        rows = pl.ds(pl.multiple_of(kb * t, t), t)
        kblk = k_ref[rows, :]
        if diagonal:
            mask = lax.broadcasted_iota(jnp.int32, (t, t), 0) > lax.broadcasted_iota(jnp.int32, (t, t), 1)
        for h in range(2):
            z = lax.dot_general(qh_ref[h], kblk, (((1,), (1,)), ((), ())), preferred_element_type=F32)
            sp = jnp.maximum(z, 0.0) + jnp.log2(1.0 + jnp.exp2(-jnp.abs(z)))
            if diagonal:
                sp = jnp.where(mask, sp, 0.0)
                z = jnp.where(mask, z, -1e30)
            hi = sp.astype(BF16)
            hl_ref[slot, h, :, :t] = hi
            hl_ref[slot, h, :, t:] = (sp - hi.astype(F32)).astype(BF16)
            z_ref[slot, h] = z

    def weights(src, dst):
        tri = tri_ref[...]
        for h in range(2):
            cs = jnp.dot(hl_ref[src, h], tri, preferred_element_type=F32)
            car = car_ref[h]
            s = z_ref[src, h] - cs - jnp.concatenate([car, car], axis=1)
            w_ref[dst, h] = jnp.exp2(s).astype(BF16)
            car_ref[h] = car + jnp.broadcast_to(cs[:, 0:1], car.shape)

    def values(kb, slot):
        rows = pl.ds(pl.multiple_of(kb * t, t), t)
        vblk = v_ref[rows, :]
        for h in range(2):
            acc_ref[h] += jnp.dot(w_ref[slot, h], vblk, preferred_element_type=F32)

    scores(qb, 0, True)

    def step(it, cur, prev):
        scores(jnp.maximum(qb - it, 0), cur, False)
        weights(prev, cur)
        values(jnp.minimum(qb - it + 2, qb), prev)

    def faded():
        return jnp.minimum(jnp.min(car_ref[0]), jnp.min(car_ref[1])) >= SB_FADE

    def pair(state):
        j, _ = state
        step(2 * j + 1, 1, 0)
        step(2 * j + 2, 0, 1)
        return j + 1, faded()

    n_steps = qb + 1
    n_pairs, stop = lax.while_loop(lambda s: (s[0] < n_steps // 2) & jnp.logical_not(s[1]), pair,
                                   (jnp.int32(0), jnp.bool_(False)))
    odd = jnp.logical_not(stop) & (n_steps % 2 == 1)

    @pl.when(odd)
    def _():
        step(n_steps, 1, 0)

    done = 2 * n_pairs + odd.astype(jnp.int32)
    last = qb - done + 1

    @pl.when(done % 2 == 1)
    def _():
        values(last, 1)

    @pl.when(done % 2 == 0)
    def _():
        values(last, 0)

    o_ref[...] = jnp.where(lane < SB_HEAD, acc_ref[0], acc_ref[1]).astype(o_ref.dtype)


def _sb_attention(q, k, v, bsz, seq):
    t = SB_T
    nq = seq // t
    pairs = SB_WIDTH // LANES
    tri = (jnp.arange(t)[:, None] >= jnp.arange(t)[None, :]).astype(BF16)
    tri2 = jnp.concatenate([tri, tri], axis=0)
    return pl.pallas_call(
        functools.partial(_sb_kernel, t=t),
        grid=(bsz, pairs, nq),
        in_specs=[pl.BlockSpec((t, LANES), lambda b, p, i: (b * nq + i, p)),
                  pl.BlockSpec((seq, LANES), lambda b, p, i: (b, p)),
                  pl.BlockSpec((seq, LANES), lambda b, p, i: (b, p)),
                  pl.BlockSpec((2 * t, t), lambda b, p, i: (0, 0))],
        out_specs=pl.BlockSpec((t, LANES), lambda b, p, i: (b * nq + i, p)),
        out_shape=jax.ShapeDtypeStruct((bsz * seq, SB_WIDTH), BF16),
        scratch_shapes=[pltpu.VMEM((2, t, LANES), BF16), pltpu.VMEM((2, 2, t, 2 * t), BF16),
                        pltpu.VMEM((2, 2, t, t), F32), pltpu.VMEM((2, 2, t, t), BF16),
                        pltpu.VMEM((2, t, LANES), F32), pltpu.VMEM((2, t, LANES), F32)],
        compiler_params=_cparams("arbitrary", "arbitrary", "arbitrary"),
        name="sb_attention",
    )(q, k, v, tri2)


def _mix_out_kernel(*refs, glu):
    if glu:
        ya_ref, yb_ref, x_ref, gate_ref, wglu_ref, wo_ref, nw_ref, sh_ref, sc_ref, wr_ref, br_ref = refs[:11]
        outs = refs[11:]
        a = _gelu_tanh(ya_ref[...].astype(F32)).astype(BF16)
        g = jnp.dot(a, wglu_ref[...], preferred_element_type=F32)
        half = g.shape[1] // 2
        ya = (g[:, :half] * _sigmoid(g[:, half:])).astype(BF16)
    else:
        ya_ref, yb_ref, x_ref, gate_ref, wo_ref, nw_ref, sh_ref, sc_ref, wr_ref, br_ref = refs[:10]
        outs = refs[10:]
        ya = ya_ref[...]
    x1_ref, h2_ref, lg_ref = outs
    ka = ya.shape[1]
    mix = (jnp.dot(ya, wo_ref[:ka, :], preferred_element_type=F32)
           + jnp.dot(yb_ref[...], wo_ref[ka:, :], preferred_element_type=F32))
    x1 = x_ref[...] + gate_ref[0] * mix
    x1_ref[...] = x1
    h2 = _rms_mod(x1, nw_ref[...], sh_ref[0], sc_ref[0])
    h2_ref[...] = h2.astype(BF16)
    lg_ref[...] = _dot3(h2, wr_ref[...]) + br_ref[...]


def _mix_out(ya, yb, x, gate, w_glu, w_out, nw, shift, scale, w_router, b_router, seq, tm):
    n_tok, d = x.shape
    tpb = seq // tm
    row = lambda i: (i, 0)
    bat = lambda i: (i // tpb, 0, 0)
    fix = lambda i: (0, 0)
    in_specs = [pl.BlockSpec((tm, ya.shape[1]), row), pl.BlockSpec((tm, yb.shape[1]), row),
                pl.BlockSpec((tm, d), row), pl.BlockSpec((1, 1, d), bat)]
    args = [ya, yb, x, gate]
    if w_glu is not None:
        in_specs.append(pl.BlockSpec(w_glu.shape, fix))
        args.append(w_glu)
    in_specs += [pl.BlockSpec(w_out.shape, fix), pl.BlockSpec((1, d), fix), pl.BlockSpec((1, 1, d), bat),
                 pl.BlockSpec((1, 1, d), bat), pl.BlockSpec(w_router.shape, fix), pl.BlockSpec(b_router.shape, fix)]
    args += [w_out, nw, shift, scale, w_router, b_router]
    return pl.pallas_call(
        functools.partial(_mix_out_kernel, glu=w_glu is not None),
        grid=(n_tok // tm,),
        in_specs=in_specs,
        out_specs=[pl.BlockSpec((tm, d), row), pl.BlockSpec((tm, d), row), pl.BlockSpec((tm, LANES), row)],
        out_shape=[jax.ShapeDtypeStruct((n_tok, d), F32), jax.ShapeDtypeStruct((n_tok, d), BF16),
                   jax.ShapeDtypeStruct((n_tok, LANES), F32)],
        compiler_params=_cparams("arbitrary"),
        name="mix_out",
    )(*args)


def _route_kernel(lg_ref, tri_ref, o_ref, cnt_ref, run_ref):
    @pl.when(pl.program_id(0) == 0)
    def _():
        run_ref[...] = jnp.zeros_like(run_ref)

    lg = lg_ref[...]
    lane = lax.broadcasted_iota(jnp.int32, lg.shape, 1)
    neg = jnp.float32(-jnp.inf)
    far = jnp.int32(LANES)

    def top(x):
        m = jnp.max(x, axis=1, keepdims=True)
        return m, jnp.min(jnp.where(x == m, lane, far), axis=1, keepdims=True)

    is_grp = lane < MOE_GROUPS
    gmax, gidx = top(jnp.where(is_grp, lg, neg))
    grp_p = 1.0 / jnp.sum(jnp.where(is_grp, jnp.exp(lg - gmax), 0.0), axis=1, keepdims=True)
    first = MOE_GROUPS + MOE_PER_GROUP * gidx
    el = jnp.where((lane >= first) & (lane < first + MOE_PER_GROUP), lg, neg)
    l0, i0 = top(el)
    l1, i1 = top(jnp.where(lane == i0, neg, el))
    e1 = jnp.exp(l1 - l0)
    g0 = grp_p / (1.0 + e1)
    g1 = grp_p * e1 / (1.0 + e1)
    pick0 = lane == i0
    pick1 = lane == i1
    onehot = jnp.where(pick0 | pick1, 1.0, 0.0)
    before = jnp.dot(tri_ref[...], onehot.astype(BF16), preferred_element_type=F32) + run_ref[...]
    r0 = jnp.sum(jnp.where(pick0, before, 0.0), axis=1, keepdims=True)
    r1 = jnp.sum(jnp.where(pick1, before, 0.0), axis=1, keepdims=True)
    run = run_ref[...] + jnp.sum(onehot, axis=0, keepdims=True)
    run_ref[...] = run
    cnt_ref[...] = run
    cols = ((i0 - MOE_GROUPS).astype(F32), (i1 - MOE_GROUPS).astype(F32), r0, r1, g0, g1)
    out = jnp.zeros(lg.shape, F32)
    for j, col in enumerate(cols):
        out = jnp.where(lane == j, col, out)
    o_ref[...] = out


def _route(logits, tm):
    n_tok = logits.shape[0]
    tri = (jnp.arange(tm)[:, None] > jnp.arange(tm)[None, :]).astype(BF16)
    return pl.pallas_call(
        _route_kernel,
        grid=(n_tok // tm,),
        in_specs=[pl.BlockSpec((tm, LANES), lambda i: (i, 0)), pl.BlockSpec((tm, tm), lambda i: (0, 0))],
        out_specs=[pl.BlockSpec((tm, LANES), lambda i: (i, 0)), pl.BlockSpec((1, LANES), lambda i: (0, 0))],
        out_shape=[jax.ShapeDtypeStruct((n_tok, LANES), F32), jax.ShapeDtypeStruct((1, LANES), F32)],
        scratch_shapes=[pltpu.VMEM((1, LANES), F32)],
        compiler_params=_cparams("arbitrary"),
        name="route",
    )(logits, tri)


def _dispatch(routed, counts, n_tok):
    n_assign = n_tok * MOE_TOPK
    n_blocks = (n_assign + MOE_EXPERTS * (MOE_BLK - 1) + MOE_BLK - 1) // MOE_BLK
    experts = routed[:, 0:2].astype(jnp.int32)
    rank = routed[:, 2:4].astype(jnp.int32)
    counts = counts[0, MOE_GROUPS:MOE_GROUPS + MOE_EXPERTS].astype(jnp.int32)
    padded = ((counts + MOE_BLK - 1) // MOE_BLK) * MOE_BLK
    pad_end = jnp.cumsum(padded)
    pad_start = pad_end - padded
    dest = jnp.take(pad_start, experts, mode='clip') + rank
    flat_tok = jnp.repeat(jnp.arange(n_tok, dtype=jnp.int32), MOE_TOPK)
    n_rows = n_blocks * MOE_BLK
    row_tok = (jnp.arange(n_rows, dtype=jnp.int32) % n_tok).at[dest.reshape(-1)].set(flat_tok)
    blk_start = jnp.arange(n_blocks, dtype=jnp.int32) * MOE_BLK
    blk_expert = jnp.minimum(jnp.sum((pad_end[None, :] <= blk_start[:, None]).astype(jnp.int32), axis=1),
                             MOE_EXPERTS - 1).astype(jnp.int32)
    n_used = (pad_end[-1] // MOE_BLK).astype(jnp.int32).reshape(1)
    return dest, row_tok, blk_expert, n_used, n_blocks


def _moe_kernel(be_ref, nu_ref, x_ref, wg_ref, wu_ref, wd_ref, o_ref, wg_s, wu_s, wd_s):
    i = pl.program_id(0)
    e = be_ref[i]
    prev = be_ref[jnp.maximum(i - 1, 0)]

    @pl.when((i == 0) | (e != prev))
    def _():
        wg_s[...] = wg_ref[0, 0].astype(BF16)
        wu_s[...] = wu_ref[0, 0].astype(BF16)
        wd_s[...] = wd_ref[0, 0].astype(BF16)

    @pl.when(i < nu_ref[0])
    def _():
        x = x_ref[...]
        a = jnp.dot(x, wg_s[...], preferred_element_type=F32)
        b = jnp.dot(x, wu_s[...], preferred_element_type=F32)
        hid = (_silu(a) * b).astype(BF16)
        o_ref[...] = jnp.dot(hid, wd_s[...], preferred_element_type=F32).astype(o_ref.dtype)

    @pl.when(i >= nu_ref[0])
    def _():
        o_ref[...] = jnp.zeros_like(o_ref)


def _moe_experts(xs, blk_expert, n_used, w_gate, w_up, w_down, layer, n_blocks):
    d = xs.shape[1]
    hid = w_gate.shape[3]
    grid_spec = pltpu.PrefetchScalarGridSpec(
        num_scalar_prefetch=2,
        grid=(n_blocks,),
        in_specs=[pl.BlockSpec((MOE_BLK, d), lambda i, be, nu: (i, 0)),
                  pl.BlockSpec((1, 1, d, hid), lambda i, be, nu: (layer, be[i], 0, 0)),
                  pl.BlockSpec((1, 1, d, hid), lambda i, be, nu: (layer, be[i], 0, 0)),
                  pl.BlockSpec((1, 1, hid, d), lambda i, be, nu: (layer, be[i], 0, 0))],
        out_specs=pl.BlockSpec((MOE_BLK, d), lambda i, be, nu: (i, 0)),
        scratch_shapes=[pltpu.VMEM((d, hid), BF16), pltpu.VMEM((d, hid), BF16), pltpu.VMEM((hid, d), BF16)],
    )
    return pl.pallas_call(
        _moe_kernel,
        grid_spec=grid_spec,
        out_shape=jax.ShapeDtypeStruct((n_blocks * MOE_BLK, d), BF16),
        compiler_params=_cparams("arbitrary"),
        name="moe_experts",
    )(blk_expert, n_used, xs, w_gate, w_up, w_down)


def _moe(h2, logits, w_gate, w_up, w_down, layer):
    n_tok, d = h2.shape
    routed, counts = _route(logits, 512)
    gates = routed[:, 4:6]
    dest, row_tok, blk_expert, n_used, n_blocks = _dispatch(routed, counts, n_tok)
    rows = lambda a, idx: a.at[idx].get(mode='promise_in_bounds')
    xs = rows(h2, row_tok)
    ys = _moe_experts(xs, blk_expert, n_used, w_gate, w_up, w_down, layer, n_blocks)
    return rows(ys, dest[:, 0]), rows(ys, dest[:, 1]), gates


def _rw_kernel(p_ref, mu_ref, pv_ref, wwa_ref, g2_ref, bd_ref, tri_ref, o_ref, st_ref, last_ref, *, tt):
    i = pl.program_id(1)
    L = RW_L
    W = RW_WIDTH

    @pl.when(i == 0)
    def _():
        st_ref[...] = jnp.zeros_like(st_ref)
        last_ref[...] = jnp.zeros_like(last_ref)

    p = p_ref[...].astype(F32)
    row = lax.broadcasted_iota(jnp.int32, p.shape, 0)
    prev = jnp.where(row == 0, jnp.broadcast_to(last_ref[0:1, :], p.shape), pltpu.roll(p, 1, axis=0))
    last_ref[0:1, :] = p[tt - 1:tt, :]
    p = p + (prev - p) * mu_ref[...]
    r = p[:, 0:W]
    k = p[:, W:2 * W]
    v = p[:, 2 * W:3 * W]
    lw = p[:, 3 * W:3 * W + LANES]
    xg = p[:, 3 * W + LANES:3 * W + 2 * LANES]
    w0, a0, k_k, k_a, r_k, ln_w, ln_b = (pv_ref[j:j + 1, :] for j in range(7))
    lane = lax.broadcasted_iota(jnp.int32, lw.shape, 1)
    wa = _bdot(jnp.where(lane < 64, jnp.tanh(lw), lw), wwa_ref[...])
    w = -_softplus(-(w0 + wa[:, :W])) - 0.5
    logd = -jnp.exp(w)
    lr = _sigmoid(a0 + wa[:, W:])
    gate = _bdot(_sigmoid(xg), g2_ref[...])
    bd = bd_ref[...]
    kk = k * k_k
    kk = kk / jnp.maximum(jnp.sqrt(_dot_exact_rhs(kk * kk, bd)), 1e-12)
    k = k * (1.0 + (lr - 1.0) * k_a)
    av = -kk
    bv = kk * lr

    tri = tri_ref[...]
    lane_p = lax.broadcasted_iota(jnp.int32, (L, LANES), 1)
    m0 = lane_p < RW_HEAD
    ri = lax.broadcasted_iota(jnp.int32, (2 * L, 2 * L), 0)
    ci = lax.broadcasted_iota(jnp.int32, (2 * L, 2 * L), 1)
    same = (ri // L) == (ci // L)
    mask_s = same & (ci < ri)
    mask_i = same & (ci <= ri)
    eye = (ri == ci).astype(F32)

    def stack(x):
        return jnp.concatenate([jnp.where(m0, x, 0.0), jnp.where(m0, 0.0, x)], axis=0)

    n_chunks = tt // L
    n_pairs = RW_HEADS // 2
    aa, rr, vv, bt2, kt2, bgs, kgs, gam = [], [], [], [], [], [], [], []
    for c in range(n_chunks):
        rows = slice(c * L, (c + 1) * L)
        ld = logd[rows]
        cs = _dot_exact_lhs(tri, ld)
        total = cs[L - 1:L, :]
        e_out = jnp.exp(-cs)
        e_end = jnp.exp(total - cs)
        at = av[rows] * jnp.exp(cs - ld)
        rt = r[rows] * jnp.exp(cs)
        bt = bv[rows] * e_out
        kt = k[rows] * e_out
        bg = bv[rows] * e_end
        kg = k[rows] * e_end
        g_all = jnp.exp(total)
        for hp in range(n_pairs):
            ls = slice(hp * LANES, (hp + 1) * LANES)
            aa.append(stack(at[:, ls]).astype(BF16))
            rr.append(stack(rt[:, ls]))
            vv.append(stack(v[rows, ls]).astype(BF16))
            bt2.append(jnp.concatenate([bt[:, ls], bt[:, ls]], axis=0).astype(BF16))
            kt2.append(jnp.concatenate([kt[:, ls], kt[:, ls]], axis=0).astype(BF16))
            bgs.append(stack(bg[:, ls]).astype(BF16))
            kgs.append(stack(kg[:, ls]).astype(BF16))
            gam.append(g_all[:, ls])
    n_sys = len(aa)
    rng = range(n_sys)
    n_ab = [jnp.where(mask_s, _bdot_nt(aa[i], bt2[i]), 0.0) for i in rng]
    n_ak = [jnp.where(mask_s, _bdot_nt(aa[i], kt2[i]), 0.0).astype(BF16) for i in rng]
    n_rb = [jnp.where(mask_i, _bdot_nt(rr[i], bt2[i]), 0.0).astype(BF16) for i in rng]
    n_rk = [jnp.where(mask_i, _bdot_nt(rr[i], kt2[i]), 0.0).astype(BF16) for i in rng]
    tinv = [eye + n_ab[i] for i in rng]
    pw = n_ab
    for _ in range(int(math.log2(L)) - 1):
        pw = [_bdot(pw[i], pw[i]) for i in rng]
        tinv = [tinv[i] + _bdot(tinv[i], pw[i]) for i in rng]
    tinv = [tinv[i].astype(BF16) for i in rng]
    tav = [_bdot(tinv[i], _bdot(n_ak[i], vv[i])).astype(BF16) for i in rng]
    g_op = [_bdot_tn(tinv[i], bgs[i]) for i in rng]
    mg = [_bdot_tn(aa[i], g_op[i]).astype(BF16) for i in rng]
    w1 = [_bdot(n_rb[i], tinv[i]) for i in rng]
    ra = [(rr[i] + _bdot(w1[i], aa[i])).astype(BF16) for i in rng]
    y0 = [_bdot(n_rb[i], tav[i]) + _bdot(n_rk[i], vv[i]) for i in rng]
    s0 = [_bdot_tn(tav[i], bgs[i]) + _bdot_tn(vv[i], kgs[i]) for i in rng]
    for c in range(n_chunks):
        rows = slice(c * L, (c + 1) * L)
        y_parts = []
        for hp in range(n_pairs):
            i = c * n_pairs + hp
            s = st_ref[hp]
            sb = s.astype(BF16)
            yst = _bdot_nt(ra[i], sb) + y0[i]
            y_parts.append(yst[:L] + yst[L:])
            st_ref[hp] = s * gam[i] + jnp.dot(sb, mg[i], preferred_element_type=F32) + s0[i]
        y = jnp.concatenate(y_parts, axis=1)
        mean = _dot_exact_rhs(y, bd) * (1.0 / RW_HEAD)
        yc = y - mean
        var = _dot_exact_rhs(yc * yc, bd) * (1.0 / RW_HEAD)
        yn = yc * lax.rsqrt(var + RW_LN_EPS) * ln_w + ln_b
        bonus = _dot_exact_rhs(r[rows] * k[rows] * r_k, bd) * v[rows]
        o_ref[rows, :] = ((yn + bonus) * gate[rows]).astype(o_ref.dtype)


def _rwkv7(p, mu, pv, wwa, g2, bsz, seq):
    tt = RW_TT
    nt = seq // tt
    L = RW_L
    hid = jnp.arange(RW_WIDTH) // RW_HEAD
    bd = (hid[:, None] == hid[None, :]).astype(BF16)
    tri = (jnp.arange(L)[:, None] >= jnp.arange(L)[None, :]).astype(BF16)
    fix = lambda b, i: (0, 0)
    return pl.pallas_call(
        functools.partial(_rw_kernel, tt=tt),
        grid=(bsz, nt),
        in_specs=[pl.BlockSpec((tt, RW_IN), lambda b, i: (b * nt + i, 0)),
                  pl.BlockSpec(mu.shape, fix), pl.BlockSpec(pv.shape, fix), pl.BlockSpec(wwa.shape, fix),
                  pl.BlockSpec(g2.shape, fix), pl.BlockSpec(bd.shape, fix), pl.BlockSpec(tri.shape, fix)],
        out_specs=pl.BlockSpec((tt, RW_WIDTH), lambda b, i: (b * nt + i, 0)),
        out_shape=jax.ShapeDtypeStruct((bsz * seq, RW_WIDTH), BF16),
        scratch_shapes=[pltpu.VMEM((RW_HEADS // 2, 2 * RW_HEAD, LANES), F32), pltpu.VMEM((8, RW_IN), F32)],
        compiler_params=_cparams("arbitrary", "arbitrary"),
        name="rwkv7",
    )(p, mu, pv, wwa, g2, bd, tri)


def _ssd_kernel(p_ref, cw_ref, cb_ref, hv_ref, dl_ref, nw_ref, tri_ref, o_ref, st_ref, tail_ref, *, L):
    i = pl.program_id(1)

    @pl.when(i == 0)
    def _():
        st_ref[...] = jnp.zeros_like(st_ref)
        tail_ref[...] = jnp.zeros_like(tail_ref)

    z = p_ref[:, 0:M2_INNER].astype(F32)
    xin = p_ref[:, M2_INNER:M2_INNER + M2_CONV_DIM].astype(F32)
    dt_raw = p_ref[:, M2_INNER + M2_CONV_DIM:M2_INNER + M2_CONV_DIM + LANES].astype(F32)
    tail = tail_ref[...]
    tail_ref[...] = xin[L - 8:L, :]
    row8 = lax.broadcasted_iota(jnp.int32, (8, M2_CONV_DIM), 0)
    conv = xin * cw_ref[M2_CONV - 1:M2_CONV, :] + cb_ref[...]
    for j in range(1, M2_CONV):
        rolled = pltpu.roll(xin, j, axis=0)
        head = jnp.where(row8 < j, pltpu.roll(tail, j, axis=0), rolled[0:8])
        shifted = jnp.concatenate([head, rolled[8:]], axis=0)
        conv = conv + shifted * cw_ref[M2_CONV - 1 - j:M2_CONV - j, :]
    xbc = _silu(conv)
    xs = xbc[:, 0:M2_INNER]
    dt = _softplus(dt_raw + hv_ref[0:1, :])
    adt = dt * hv_ref[1:2, :]
    cs = _dot_exact_lhs(tri_ref[...], adt)
    cs_t = jnp.transpose(cs)
    total = cs[L - 1:L, :]
    dec_in = jnp.exp(cs)
    dec_out = jnp.exp(total - cs)
    dec_all = jnp.exp(total)
    lane = lax.broadcasted_iota(jnp.int32, (L, LANES), 1)
    first = lane < 64
    rowp = lax.broadcasted_iota(jnp.int32, (LANES, M2_STATE), 0) < 64
    li = lax.broadcasted_iota(jnp.int32, (L, L), 0)
    si = lax.broadcasted_iota(jnp.int32, (L, L), 1)
    causal = li >= si
    y_parts = []
    for hp in range(M2_HEADS // 2):
        g = hp // 2
        h0, h1 = 2 * hp, 2 * hp + 1
        bm = xbc[:, M2_INNER + g * M2_STATE:M2_INNER + (g + 1) * M2_STATE]
        cm = xbc[:, M2_INNER + 2 * M2_STATE + g * M2_STATE:M2_INNER + 2 * M2_STATE + (g + 1) * M2_STATE]
        x_p = xs[:, hp * LANES:(hp + 1) * LANES]
        sel = lambda a: jnp.where(first, a[:, h0:h0 + 1], a[:, h1:h1 + 1])
        xd = x_p * sel(dt)
        cb = _bdot_nt(cm, bm)
        yd = []
        for h in (h0, h1):
            lmat = jnp.where(causal, jnp.exp(cs[:, h:h + 1] - cs_t[h:h + 1, :]), 0.0)
            yd.append(_bdot(cb * lmat, xd))
        s = st_ref[hp]
        y_off = _bdot_nt(cm, s) * sel(dec_in)
        y_parts.append(jnp.where(first, yd[0], yd[1]) + y_off)
        s_new = _bdot_tn(xd * sel(dec_out), bm)
        st_ref[hp] = s * jnp.where(rowp, dec_all[:, h0:h0 + 1], dec_all[:, h1:h1 + 1]) + s_new
    y = jnp.concatenate(y_parts, axis=1) + dl_ref[...] * xs
    y = y * _silu(z)
    half = M2_INNER // 2
    outs = []
    for g in range(2):
        yg = y[:, g * half:(g + 1) * half]
        outs.append(yg * lax.rsqrt(jnp.mean(yg * yg, axis=-1, keepdims=True) + RMS_EPS))
    o_ref[...] = (jnp.concatenate(outs, axis=1) * nw_ref[...]).astype(o_ref.dtype)


def _ssd(p, conv_w, conv_b, hv, d_lanes, norm_w, bsz, seq):
    L = M2_L
    nt = seq // L
    tri = (jnp.arange(L)[:, None] >= jnp.arange(L)[None, :]).astype(BF16)
    fix = lambda b, i: (0, 0)
    return pl.pallas_call(
        functools.partial(_ssd_kernel, L=L),
        grid=(bsz, nt),
        in_specs=[pl.BlockSpec((L, M2_IN_PAD), lambda b, i: (b * nt + i, 0)),
                  pl.BlockSpec(conv_w.shape, fix), pl.BlockSpec(conv_b.shape, fix), pl.BlockSpec(hv.shape, fix),
                  pl.BlockSpec(d_lanes.shape, fix), pl.BlockSpec(norm_w.shape, fix), pl.BlockSpec(tri.shape, fix)],
        out_specs=pl.BlockSpec((L, M2_INNER), lambda b, i: (b * nt + i, 0)),
        out_shape=jax.ShapeDtypeStruct((bsz * seq, M2_INNER), BF16),
        scratch_shapes=[pltpu.VMEM((M2_HEADS // 2, LANES, M2_STATE), F32), pltpu.VMEM((8, M2_CONV_DIM), F32)],
        compiler_params=_cparams("arbitrary", "arbitrary"),
        name="ssd",
    )(p, conv_w, conv_b, hv, d_lanes, norm_w, tri)


def _final_kernel(x_ref, ya_ref, yb_ref, g_ref, gate_ref, nw_ref, o_ref):
    g = g_ref[...]
    moe = g[:, 0:1] * ya_ref[...].astype(F32) + g[:, 1:2] * yb_ref[...].astype(F32)
    x = x_ref[...] + gate_ref[0] * moe
    ms = jnp.mean(x * x, axis=-1, keepdims=True)
    o_ref[...] = x * lax.rsqrt(ms + RMS_EPS) * nw_ref[...]


def _final(x, ya, yb, g, gate, nw, seq, tm):
    n_tok, d = x.shape
    tpb = seq // tm
    row = lambda i: (i, 0)
    return pl.pallas_call(
        _final_kernel,
        grid=(n_tok // tm,),
        in_specs=[pl.BlockSpec((tm, d), row), pl.BlockSpec((tm, d), row), pl.BlockSpec((tm, d), row),
                  pl.BlockSpec((tm, g.shape[1]), row), pl.BlockSpec((1, 1, d), lambda i: (i // tpb, 0, 0)),
                  pl.BlockSpec((1, d), lambda i: (0, 0))],
        out_specs=pl.BlockSpec((tm, d), row),
        out_shape=jax.ShapeDtypeStruct((n_tok, d), F32),
        compiler_params=_cparams("arbitrary"),
        name="final_norm",
    )(x, ya, yb, g, gate, nw)


def _router_weights(w_grp, b_grp, w_exp, b_exp):
    d = w_grp.shape[0]
    pad = LANES - MOE_GROUPS - MOE_EXPERTS
    w = jnp.concatenate([w_grp, w_exp, jnp.zeros((d, pad), F32)], axis=1)
    b = jnp.concatenate([b_grp, b_exp, jnp.zeros((pad,), F32)])[None, :]
    return w, b


def kernel(x, c, ada_w, ada_b, norm_mix_w, norm_ffn_w, even_w_in, even_w_out, s5_a_re, s5_a_im, s5_log_dt, s5_b_re, s5_b_im, s5_c_re, s5_c_im, s5_d, s5_w_glu, odd_w_in, odd_w_out, rw_mu, rw_w0, rw_w2, rw_a0, rw_a2, rw_g2, rw_k_k, rw_k_a, rw_r_k, rw_ln_w, rw_ln_b, m2_conv_w, m2_conv_b, m2_dt_bias, m2_a_log, m2_d, m2_norm_w, moe_w_grp, moe_b_grp, moe_w_exp, moe_b_exp, moe_w_gate, moe_w_up, moe_w_down, final_norm_w):
    bsz, seq, d = x.shape
    n_tok = bsz * seq
    tm = 512
    xt = x.reshape(n_tok, d)
    ada = _ada_params(c, ada_w, ada_b)
    mods = [[ada[i, :, j * d:(j + 1) * d].reshape(bsz, 1, d) for j in range(6)] for i in range(2)]

    sh1, sc1, g1, sh2, sc2, g2 = mods[0]
    u, q, k, v = _in_proj(xt, norm_mix_w[0][None, :], sh1, sc1, even_w_in[0].astype(BF16),
                          ((0, 512), (512, 1024), (1024, 1536), (1536, 2048)), (F32, BF16, BF16, BF16), seq, tm)
    tables = _s5_tables(s5_a_re[0], s5_a_im[0], s5_log_dt[0], s5_b_re[0], s5_b_im[0], s5_c_re[0], s5_c_im[0], s5_d[0])
    y_s5 = _s5_scan(u, tables, bsz, seq)
    y_sb = _sb_attention(q, k, v, bsz, seq)
    wr, br = _router_weights(moe_w_grp[0], moe_b_grp[0], moe_w_exp[0], moe_b_exp[0])
    x1, h2, logits = _mix_out(y_s5, y_sb, xt, g1, s5_w_glu[0].astype(BF16), even_w_out[0].astype(BF16),
                              norm_ffn_w[0][None, :], sh2, sc2, wr, br, seq, tm)
    ya, yb, gates = _moe(h2, logits, moe_w_gate, moe_w_up, moe_w_down, 0)

    gate_prev = g2
    sh1, sc1, g1, sh2, sc2, g2 = mods[1]
    w_in = jnp.concatenate([odd_w_in[0], jnp.zeros((d, M2_IN_PAD - M2_IN), F32)], axis=1).astype(BF16)
    x2, p_rw, p_m2 = _in_proj(x1, norm_mix_w[1][None, :], sh1, sc1, w_in,
                              ((0, RW_IN), (RW_IN, RW_IN + M2_IN_PAD)), (BF16, BF16), seq, 256,
                              add=(ya, yb, gates, gate_prev))
    pv = jnp.stack([rw_w0[0], rw_a0[0], rw_k_k[0], rw_k_a[0], rw_r_k[0].reshape(-1), rw_ln_w[0], rw_ln_b[0],
                    jnp.zeros((RW_WIDTH,), F32)])
    zl = jnp.zeros((64, RW_WIDTH), F32)
    wwa = jnp.concatenate([jnp.concatenate([rw_w2[0], zl], axis=1),
                           jnp.concatenate([zl, rw_a2[0]], axis=1)], axis=0).astype(BF16)
    y_rw = _rwkv7(p_rw, rw_mu[0][None, :], pv, wwa, rw_g2[0].astype(BF16), bsz, seq)
    hv = jnp.zeros((8, LANES), F32)
    hv = hv.at[0, :M2_HEADS].set(m2_dt_bias[0]).at[1, :M2_HEADS].set(-jnp.exp(m2_a_log[0]))
    y_m2 = _ssd(p_m2, m2_conv_w[0], m2_conv_b[0][None, :], hv, jnp.repeat(m2_d[0], 64)[None, :],
                m2_norm_w[0][None, :], bsz, seq)
    wr, br = _router_weights(moe_w_grp[1], moe_b_grp[1], moe_w_exp[1], moe_b_exp[1])
    x3, h2, logits = _mix_out(y_rw, y_m2, x2, g1, None, odd_w_out[0].astype(BF16),
                              norm_ffn_w[1][None, :], sh2, sc2, wr, br, seq, tm)
    ya, yb, gates = _moe(h2, logits, moe_w_gate, moe_w_up, moe_w_down, 1)
    out = _final(x3, ya, yb, gates, g2, final_norm_w[None, :], seq, tm)
    return out.reshape(bsz, seq, d)
```

```python
import functools
import math

import jax
import jax.numpy as jnp
from jax import lax
from jax.experimental import pallas as pl
from jax.experimental.pallas import tpu as pltpu

F32 = jnp.float32
BF16 = jnp.bfloat16

D_MODEL = 1024
RMS_EPS = 1e-6
LANES = 128
VMEM_LIMIT = 56 * 1024 * 1024

S5_WIDTH = 512
S5_P = 16
S5_G = 32
S5_N = 64
S5_L = 16
S5_GB = 8
SB_WIDTH = 512
SB_HEAD = 64
SB_T = 256
SB_FADE = 160.0

RW_WIDTH = 512
RW_HEAD = 64
RW_HEADS = 8
RW_IN = 1792
RW_LN_EPS = 64e-5
RW_L = 64
RW_TT = 256

M2_INNER = 512
M2_HEADS = 8
M2_STATE = 128
M2_CONV = 4
M2_CONV_DIM = 1024
M2_IN = 1544
M2_IN_PAD = 1664
M2_L = 256

MOE_GROUPS = 4
MOE_PER_GROUP = 8
MOE_EXPERTS = 32
MOE_TOPK = 2
MOE_HIDDEN = 512
MOE_BLK = 512


def _cparams(*sem):
    return pltpu.CompilerParams(dimension_semantics=sem, vmem_limit_bytes=VMEM_LIMIT)


def _bdot(a, b):
    return jnp.dot(a.astype(BF16), b.astype(BF16), preferred_element_type=F32)


def _bdot_nt(a, b):
    return lax.dot_general(a.astype(BF16), b.astype(BF16), (((1,), (1,)), ((), ())),
                           preferred_element_type=F32)


def _bdot_tn(a, b):
    return lax.dot_general(a.astype(BF16), b.astype(BF16), (((0,), (0,)), ((), ())),
                           preferred_element_type=F32)


def _split2(a):
    hi = a.astype(BF16)
    lo = (a - hi.astype(F32)).astype(BF16)
    return hi, lo


def _dot3(a, b):
    ah, al = _split2(a)
    bh, bl = _split2(b)
    d = functools.partial(jnp.dot, preferred_element_type=F32)
    return d(ah, bh) + d(ah, bl) + d(al, bh)


def _dot_exact_rhs(a, b01):
    ah, al = _split2(a)
    d = functools.partial(jnp.dot, preferred_element_type=F32)
    return d(ah, b01) + d(al, b01)


def _dot_exact_lhs(a01, b):
    bh, bl = _split2(b)
    d = functools.partial(jnp.dot, preferred_element_type=F32)
    return d(a01, bh) + d(a01, bl)


def _sigmoid(x):
    return 1.0 / (1.0 + jnp.exp(-x))


def _softplus(x):
    return jnp.maximum(x, 0.0) + jnp.log(1.0 + jnp.exp(-jnp.abs(x)))


def _silu(x):
    return x * _sigmoid(x)


def _gelu_tanh(x):
    c = math.sqrt(2.0 / math.pi)
    return 0.5 * x * (1.0 + jnp.tanh(c * (x + 0.044715 * (x * x * x))))


def _rms_mod(x, nw, shift, scale):
    ms = jnp.mean(x * x, axis=-1, keepdims=True)
    return (x * lax.rsqrt(ms + RMS_EPS)) * nw * (1.0 + scale) + shift


def _ada_kernel(c_ref, w_ref, b_ref, o_ref):
    cond = _silu(c_ref[...])
    o_ref[0] = _dot3(cond, w_ref[0]) + b_ref[0]


def _ada_params(c, ada_w, ada_b):
    depth, d, n = ada_w.shape
    bsz = c.shape[0]
    tn = 1024
    return pl.pallas_call(
        _ada_kernel,
        grid=(depth, n // tn),
        in_specs=[pl.BlockSpec((bsz, d), lambda i, j: (0, 0)),
                  pl.BlockSpec((1, d, tn), lambda i, j: (i, 0, j)),
                  pl.BlockSpec((1, 1, tn), lambda i, j: (i, 0, j))],
        out_specs=pl.BlockSpec((1, bsz, tn), lambda i, j: (i, 0, j)),
        out_shape=jax.ShapeDtypeStruct((depth, bsz, n), F32),
        compiler_params=_cparams("arbitrary", "arbitrary"),
        name="ada_params",
    )(c, ada_w, ada_b.reshape(depth, 1, n))


def _in_proj_kernel(*refs, has_add, splits):
    if has_add:
        x_ref, ya_ref, yb_ref, g_ref, gate_ref, nw_ref, sh_ref, sc_ref, w_ref = refs[:9]
        outs = refs[9:]
        x = x_ref[...]
        g = g_ref[...]
        moe = g[:, 0:1] * ya_ref[...].astype(F32) + g[:, 1:2] * yb_ref[...].astype(F32)
        x = x + gate_ref[0] * moe
        outs[0][...] = x
        outs = outs[1:]
    else:
        x_ref, nw_ref, sh_ref, sc_ref, w_ref = refs[:5]
        outs = refs[5:]
        x = x_ref[...]
    h = _rms_mod(x, nw_ref[...], sh_ref[0], sc_ref[0]).astype(BF16)
    for o_ref, (c0, c1) in zip(outs, splits):
        o_ref[...] = jnp.dot(h, w_ref[:, c0:c1], preferred_element_type=F32).astype(o_ref.dtype)


def _in_proj(x, nw, shift, scale, w, splits, dtypes, seq, tm, add=None):
    n_tok, d = x.shape
    tpb = seq // tm
    row = lambda i: (i, 0)
    bat = lambda i: (i // tpb, 0, 0)
    fix = lambda i: (0, 0)
    in_specs = [pl.BlockSpec((tm, d), row)]
    args = [x]
    out_shape = []
    out_specs = []
    if add is not None:
        ya, yb, g, gate = add
        in_specs += [pl.BlockSpec((tm, d), row), pl.BlockSpec((tm, d), row),
                     pl.BlockSpec((tm, g.shape[1]), row), pl.BlockSpec((1, 1, d), bat)]
        args += [ya, yb, g, gate]
        out_shape.append(jax.ShapeDtypeStruct((n_tok, d), F32))
        out_specs.append(pl.BlockSpec((tm, d), row))
    in_specs += [pl.BlockSpec((1, d), fix), pl.BlockSpec((1, 1, d), bat), pl.BlockSpec((1, 1, d), bat),
                 pl.BlockSpec(w.shape, fix)]
    args += [nw, shift, scale, w]
    for (c0, c1), dt in zip(splits, dtypes):
        out_shape.append(jax.ShapeDtypeStruct((n_tok, c1 - c0), dt))
        out_specs.append(pl.BlockSpec((tm, c1 - c0), row))
    return pl.pallas_call(
        functools.partial(_in_proj_kernel, has_add=add is not None, splits=splits),
        grid=(n_tok // tm,),
        in_specs=in_specs, out_specs=out_specs, out_shape=out_shape,
        compiler_params=_cparams("arbitrary"),
        name="in_proj",
    )(*args)


def _s5_tables(a_re, a_im, log_dt, b_re, b_im, c_re, c_im, d_skip):
    hp = lax.Precision.HIGHEST
    L = S5_L
    a_re = jnp.minimum(a_re.astype(F32), -1e-4)
    a_im = a_im.astype(F32)
    dt = jnp.exp(log_dt.astype(F32))[:, None]
    mag = jnp.exp(dt * a_re)
    abar_re, abar_im = mag * jnp.cos(dt * a_im), mag * jnp.sin(dt * a_im)
    den = a_re * a_re + a_im * a_im
    num_re, num_im = abar_re - 1.0, abar_im
    coef_re = (num_re * a_re + num_im * a_im) / den
    coef_im = (num_im * a_re - num_re * a_im) / den
    b_re = b_re.astype(F32)
    b_im = b_im.astype(F32)
    bb_re = coef_re[..., None] * b_re - coef_im[..., None] * b_im
    bb_im = coef_re[..., None] * b_im + coef_im[..., None] * b_re
    c_re = c_re.astype(F32)
    c_im = c_im.astype(F32)
    tau = jnp.arange(L + 1, dtype=F32)[None, :, None]
    pmag = jnp.exp(tau * (dt * a_re)[:, None, :])
    pw_re = pmag * jnp.cos(tau * (dt * a_im)[:, None, :])
    pw_im = pmag * jnp.sin(tau * (dt * a_im)[:, None, :])
    cl_re = c_re[:, None] * pw_re[:, :, None, :] - c_im[:, None] * pw_im[:, :, None, :]
    cl_im = c_re[:, None] * pw_im[:, :, None, :] + c_im[:, None] * pw_re[:, :, None, :]
    taps = (jnp.einsum('gtpn,gnq->gtpq', cl_re[:, :L], bb_re, precision=hp)
            - jnp.einsum('gtpn,gnq->gtpq', cl_im[:, :L], bb_im, precision=hp))
    nb = S5_G // S5_GB
    eye = jnp.eye(S5_GB, dtype=F32)
    dlag = jnp.einsum('bgtpq,gh->btgqhp', taps.reshape(nb, S5_GB, L, S5_P, S5_P), eye)
    dlag = dlag.reshape(nb, L, LANES, LANES)
    rev = L - 1 - jnp.arange(L)
    e_re = pw_re[:, rev][:, :, :, None] * bb_re[:, None] - pw_im[:, rev][:, :, :, None] * bb_im[:, None]
    e_im = pw_re[:, rev][:, :, :, None] * bb_im[:, None] + pw_im[:, rev][:, :, :, None] * bb_re[:, None]
    e_c = jnp.stack([e_re, e_im], axis=1).reshape(nb, S5_GB, 2, L, S5_N, S5_P)
    emap = jnp.einsum('bgcsnq,gh->bsgqchn', e_c, eye).reshape(nb, L * LANES, 2 * S5_GB * S5_N)
    q_c = jnp.stack([cl_re[:, 1:L + 1], -cl_im[:, 1:L + 1]], axis=1).reshape(nb, S5_GB, 2, L, S5_P, S5_N)
    qmap = jnp.einsum('bgctpn,gh->bcgnthp', q_c, eye).reshape(nb, 2 * S5_GB * S5_N, L * LANES)
    lr = pw_re[:, L].reshape(nb, 1, S5_GB * S5_N)
    li = pw_im[:, L].reshape(nb, 1, S5_GB * S5_N)
    lam_a = jnp.concatenate([lr, lr], axis=-1)
    lam_b = jnp.concatenate([-li, li], axis=-1)
    dvec = jnp.tile(d_skip.astype(F32).reshape(nb, 1, LANES), (1, 1, L))
    return dlag.astype(BF16), emap.astype(BF16), qmap.astype(BF16), lam_a, lam_b, dvec


def _s5_kernel(u_ref, dlag_ref, emap_ref, qmap_ref, la_ref, lb_ref, dv_ref, o_ref, toep_ref, h_ref, *, n_rows):
    L = S5_L
    ns = 2 * S5_GB * S5_N

    @pl.when(pl.program_id(1) == 0)
    def _():
        toep_ref[...] = jnp.zeros_like(toep_ref)
        for s in range(L):
            for t in range(s, L):
                toep_ref[s * LANES:(s + 1) * LANES, t * LANES:(t + 1) * LANES] = dlag_ref[0, t - s]

    up = jnp.concatenate([u_ref[pl.ds(s, n_rows, stride=L), :] for s in range(L)], axis=1)
    ub = up.astype(BF16)
    h_ref[...] = jnp.dot(ub, emap_ref[0], preferred_element_type=F32)
    la = la_ref[0]
    lb = lb_ref[0]

    def step(c8, h):
        rows = pl.ds(pl.multiple_of(c8 * 8, 8), 8)
        e = h_ref[rows, :]
        ent = []
        for j in range(8):
            ent.append(h)
            h = h * la + pltpu.roll(h, ns // 2, axis=1) * lb + e[j:j + 1, :]
        h_ref[rows, :] = jnp.concatenate(ent, axis=0)
        return h

    lax.fori_loop(0, n_rows // 8, step, jnp.zeros((1, ns), F32))
    hi, lo = _split2(h_ref[...])
    qm = qmap_ref[0]
    y = jnp.dot(ub, toep_ref[...], preferred_element_type=F32)
    y = y + jnp.dot(hi, qm, preferred_element_type=F32) + jnp.dot(lo, qm, preferred_element_type=F32)
    y = y + up * dv_ref[0]
    for t in range(L):
        o_ref[pl.ds(t, n_rows, stride=L), :] = y[:, t * LANES:(t + 1) * LANES]


def _s5_scan(u, tables, bsz, seq):
    dlag, emap, qmap, lam_a, lam_b, dvec = tables
    L = S5_L
    n_rows = seq // L
    nb = S5_WIDTH // LANES
    ns = 2 * S5_GB * S5_N
    blk = lambda j, b: (j, 0, 0)
    return pl.pallas_call(
        functools.partial(_s5_kernel, n_rows=n_rows),
        grid=(nb, bsz),
        in_specs=[pl.BlockSpec((seq, LANES), lambda j, b: (b, j)),
                  pl.BlockSpec((1, L, LANES, LANES), lambda j, b: (j, 0, 0, 0)),
                  pl.BlockSpec((1, L * LANES, ns), blk), pl.BlockSpec((1, ns, L * LANES), blk),
                  pl.BlockSpec((1, 1, ns), blk), pl.BlockSpec((1, 1, ns), blk), pl.BlockSpec((1, 1, L * LANES), blk)],
        out_specs=pl.BlockSpec((seq, LANES), lambda j, b: (b, j)),
        out_shape=jax.ShapeDtypeStruct((bsz * seq, S5_WIDTH), F32),
        scratch_shapes=[pltpu.VMEM((L * LANES, L * LANES), BF16), pltpu.VMEM((n_rows, ns), F32)],
        compiler_params=_cparams("arbitrary", "arbitrary"),
        name="s5_scan",
    )(u, dlag, emap, qmap, lam_a, lam_b, dvec)


def _sb_kernel(q_ref, k_ref, v_ref, tri_ref, o_ref, qh_ref, hl_ref, z_ref, w_ref, acc_ref, car_ref, *, t):
    qb = pl.program_id(1)
    heads = SB_WIDTH // SB_HEAD
    lane = lax.broadcasted_iota(jnp.int32, (t, LANES), 1)
    low = lane < SB_HEAD
    for p in range(heads // 2):
        q = (q_ref[:, p * LANES:(p + 1) * LANES].astype(F32) * (0.125 * math.log2(math.e))).astype(BF16)
        zero = jnp.zeros_like(q)
        qh_ref[2 * p] = jnp.where(low, q, zero)
        qh_ref[2 * p + 1] = jnp.where(low, zero, q)
    acc_ref[...] = jnp.zeros_like(acc_ref)
    car_ref[...] = jnp.zeros_like(car_ref)

    def scores(kb, h, slot, diagonal):
        rows = pl.ds(pl.multiple_of(kb * t, t), t)
        kblk = k_ref[rows, (h // 2) * LANES:(h // 2 + 1) * LANES]
        z = lax.dot_general(qh_ref[h], kblk, (((1,), (1,)), ((), ())), preferred_element_type=F32)
        sp = jnp.maximum(z, 0.0) + jnp.log2(1.0 + jnp.exp2(-jnp.abs(z)))
        if diagonal:
            mask = lax.broadcasted_iota(jnp.int32, (t, t), 0) > lax.broadcasted_iota(jnp.int32, (t, t), 1)
            sp = jnp.where(mask, sp, 0.0)
            z = jnp.where(mask, z, -1e30)
        hi = sp.astype(BF16)
        hl_ref[slot, :, :t] = hi
        hl_ref[slot, :, t:] = (sp - hi.astype(F32)).astype(BF16)
        z_ref[slot] = z

    def weights(h, src, dst):
        cs = jnp.dot(hl_ref[src], tri_ref[...], preferred_element_type=F32)
        car = car_ref[h]
        s = z_ref[src] - cs - jnp.concatenate([car, car], axis=1)
        w_ref[dst] = jnp.exp2(s).astype(BF16)
        car_ref[h] = car + jnp.broadcast_to(cs[:, 0:1], car.shape)

    def values(kb, h, slot):
        rows = pl.ds(pl.multiple_of(kb * t, t), t)
        vblk = v_ref[rows, (h // 2) * LANES:(h // 2 + 1) * LANES]
        acc_ref[h] += jnp.dot(w_ref[slot], vblk, preferred_element_type=F32)

    def key_tile(kb, diagonal):
        for j in range(heads + 2):
            if j < heads:
                scores(kb, j, j % 2, diagonal)
            if 1 <= j <= heads:
                weights(j - 1, (j - 1) % 2, j % 2)
            if j >= 2:
                values(kb, j - 2, (j - 1) % 2)

    def faded():
        return jnp.min(car_ref[...]) >= SB_FADE

    key_tile(qb, True)

    def more(state):
        kb, _ = state
        key_tile(kb, False)
        return kb - 1, faded()

    lax.while_loop(lambda s: (s[0] >= 0) & jnp.logical_not(s[1]), more, (qb - 1, faded()))
    for p in range(heads // 2):
        o_ref[:, p * LANES:(p + 1) * LANES] = jnp.where(low, acc_ref[2 * p], acc_ref[2 * p + 1]).astype(o_ref.dtype)


def _sb_attention(q, k, v, bsz, seq):
    t = SB_T
    nq = seq // t
    heads = SB_WIDTH // SB_HEAD
    tri = (jnp.arange(t)[:, None] >= jnp.arange(t)[None, :]).astype(BF16)
    tri2 = jnp.concatenate([tri, tri], axis=0)
    return pl.pallas_call(
        functools.partial(_sb_kernel, t=t),
        grid=(bsz, nq),
        in_specs=[pl.BlockSpec((t, SB_WIDTH), lambda b, i: (b * nq + i, 0)),
                  pl.BlockSpec((seq, SB_WIDTH), lambda b, i: (b, 0)),
                  pl.BlockSpec((seq, SB_WIDTH), lambda b, i: (b, 0)),
                  pl.BlockSpec((2 * t, t), lambda b, i: (0, 0))],
        out_specs=pl.BlockSpec((t, SB_WIDTH), lambda b, i: (b * nq + i, 0)),
        out_shape=jax.ShapeDtypeStruct((bsz * seq, SB_WIDTH), BF16),
        scratch_shapes=[pltpu.VMEM((heads, t, LANES), BF16), pltpu.VMEM((2, t, 2 * t), BF16),
                        pltpu.VMEM((2, t, t), F32), pltpu.VMEM((2, t, t), BF16),
                        pltpu.VMEM((heads, t, LANES), F32), pltpu.VMEM((heads, t, LANES), F32)],
        compiler_params=_cparams("arbitrary", "arbitrary"),
        name="sb_attention",
    )(q, k, v, tri2)


def _mix_out_kernel(*refs, glu):
    if glu:
        ya_ref, yb_ref, x_ref, gate_ref, wglu_ref, wo_ref, nw_ref, sh_ref, sc_ref, wr_ref, br_ref = refs[:11]
        outs = refs[11:]
        a = _gelu_tanh(ya_ref[...].astype(F32)).astype(BF16)
        g = jnp.dot(a, wglu_ref[...], preferred_element_type=F32)
        half = g.shape[1] // 2
        ya = (g[:, :half] * _sigmoid(g[:, half:])).astype(BF16)
    else:
        ya_ref, yb_ref, x_ref, gate_ref, wo_ref, nw_ref, sh_ref, sc_ref, wr_ref, br_ref = refs[:10]
        outs = refs[10:]
        ya = ya_ref[...]
    x1_ref, h2_ref, lg_ref = outs
    ka = ya.shape[1]
    mix = (jnp.dot(ya, wo_ref[:ka, :], preferred_element_type=F32)
           + jnp.dot(yb_ref[...], wo_ref[ka:, :], preferred_element_type=F32))
    x1 = x_ref[...] + gate_ref[0] * mix
    x1_ref[...] = x1
    h2 = _rms_mod(x1, nw_ref[...], sh_ref[0], sc_ref[0])
    h2_ref[...] = h2.astype(BF16)
    lg_ref[...] = _dot3(h2, wr_ref[...]) + br_ref[...]


def _mix_out(ya, yb, x, gate, w_glu, w_out, nw, shift, scale, w_router, b_router, seq, tm):
    n_tok, d = x.shape
    tpb = seq // tm
    row = lambda i: (i, 0)
    bat = lambda i: (i // tpb, 0, 0)
    fix = lambda i: (0, 0)
    in_specs = [pl.BlockSpec((tm, ya.shape[1]), row), pl.BlockSpec((tm, yb.shape[1]), row),
                pl.BlockSpec((tm, d), row), pl.BlockSpec((1, 1, d), bat)]
    args = [ya, yb, x, gate]
    if w_glu is not None:
        in_specs.append(pl.BlockSpec(w_glu.shape, fix))
        args.append(w_glu)
    in_specs += [pl.BlockSpec(w_out.shape, fix), pl.BlockSpec((1, d), fix), pl.BlockSpec((1, 1, d), bat),
                 pl.BlockSpec((1, 1, d), bat), pl.BlockSpec(w_router.shape, fix), pl.BlockSpec(b_router.shape, fix)]
    args += [w_out, nw, shift, scale, w_router, b_router]
    return pl.pallas_call(
        functools.partial(_mix_out_kernel, glu=w_glu is not None),
        grid=(n_tok // tm,),
        in_specs=in_specs,
        out_specs=[pl.BlockSpec((tm, d), row), pl.BlockSpec((tm, d), row), pl.BlockSpec((tm, LANES), row)],
        out_shape=[jax.ShapeDtypeStruct((n_tok, d), F32), jax.ShapeDtypeStruct((n_tok, d), BF16),
                   jax.ShapeDtypeStruct((n_tok, LANES), F32)],
        compiler_params=_cparams("arbitrary"),
        name="mix_out",
    )(*args)


def _route_kernel(lg_ref, tri_ref, o_ref, cnt_ref, run_ref):
    @pl.when(pl.program_id(0) == 0)
    def _():
        run_ref[...] = jnp.zeros_like(run_ref)

    lg = lg_ref[...]
    lane = lax.broadcasted_iota(jnp.int32, lg.shape, 1)
    neg = jnp.float32(-jnp.inf)
    far = jnp.int32(LANES)

    def top(x):
        m = jnp.max(x, axis=1, keepdims=True)
        return m, jnp.min(jnp.where(x == m, lane, far), axis=1, keepdims=True)

    is_grp = lane < MOE_GROUPS
    gmax, gidx = top(jnp.where(is_grp, lg, neg))
    grp_p = 1.0 / jnp.sum(jnp.where(is_grp, jnp.exp(lg - gmax), 0.0), axis=1, keepdims=True)
    first = MOE_GROUPS + MOE_PER_GROUP * gidx
    el = jnp.where((lane >= first) & (lane < first + MOE_PER_GROUP), lg, neg)
    l0, i0 = top(el)
    l1, i1 = top(jnp.where(lane == i0, neg, el))
    e1 = jnp.exp(l1 - l0)
    g0 = grp_p / (1.0 + e1)
    g1 = grp_p * e1 / (1.0 + e1)
    pick0 = lane == i0
    pick1 = lane == i1
    onehot = jnp.where(pick0 | pick1, 1.0, 0.0)
    before = jnp.dot(tri_ref[...], onehot.astype(BF16), preferred_element_type=F32) + run_ref[...]
    r0 = jnp.sum(jnp.where(pick0, before, 0.0), axis=1, keepdims=True)
    r1 = jnp.sum(jnp.where(pick1, before, 0.0), axis=1, keepdims=True)
    run = run_ref[...] + jnp.sum(onehot, axis=0, keepdims=True)
    run_ref[...] = run
    cnt_ref[...] = run
    cols = ((i0 - MOE_GROUPS).astype(F32), (i1 - MOE_GROUPS).astype(F32), r0, r1, g0, g1)
    out = jnp.zeros(lg.shape, F32)
    for j, col in enumerate(cols):
        out = jnp.where(lane == j, col, out)
    o_ref[...] = out


def _route(logits, tm):
    n_tok = logits.shape[0]
    tri = (jnp.arange(tm)[:, None] > jnp.arange(tm)[None, :]).astype(BF16)
    return pl.pallas_call(
        _route_kernel,
        grid=(n_tok // tm,),
        in_specs=[pl.BlockSpec((tm, LANES), lambda i: (i, 0)), pl.BlockSpec((tm, tm), lambda i: (0, 0))],
        out_specs=[pl.BlockSpec((tm, LANES), lambda i: (i, 0)), pl.BlockSpec((1, LANES), lambda i: (0, 0))],
        out_shape=[jax.ShapeDtypeStruct((n_tok, LANES), F32), jax.ShapeDtypeStruct((1, LANES), F32)],
        scratch_shapes=[pltpu.VMEM((1, LANES), F32)],
        compiler_params=_cparams("arbitrary"),
        name="route",
    )(logits, tri)


def _dispatch(routed, counts, n_tok):
    n_assign = n_tok * MOE_TOPK
    n_blocks = (n_assign + MOE_EXPERTS * (MOE_BLK - 1) + MOE_BLK - 1) // MOE_BLK
    experts = routed[:, 0:2].astype(jnp.int32)
    rank = routed[:, 2:4].astype(jnp.int32)
    counts = counts[0, MOE_GROUPS:MOE_GROUPS + MOE_EXPERTS].astype(jnp.int32)
    padded = ((counts + MOE_BLK - 1) // MOE_BLK) * MOE_BLK
    pad_end = jnp.cumsum(padded)
    pad_start = pad_end - padded
    dest = jnp.take(pad_start, experts, mode='clip') + rank
    flat_tok = jnp.repeat(jnp.arange(n_tok, dtype=jnp.int32), MOE_TOPK)
    n_rows = n_blocks * MOE_BLK
    row_tok = (jnp.arange(n_rows, dtype=jnp.int32) % n_tok).at[dest.reshape(-1)].set(
        flat_tok, unique_indices=True, mode='promise_in_bounds')
    blk_start = jnp.arange(n_blocks, dtype=jnp.int32) * MOE_BLK
    blk_expert = jnp.minimum(jnp.sum((pad_end[None, :] <= blk_start[:, None]).astype(jnp.int32), axis=1),
                             MOE_EXPERTS - 1).astype(jnp.int32)
    n_used = (pad_end[-1] // MOE_BLK).astype(jnp.int32).reshape(1)
    return dest, row_tok, blk_expert, n_used, n_blocks


def _moe_kernel(be_ref, nu_ref, x_ref, wg_ref, wu_ref, wd_ref, o_ref, wg_s, wu_s, wd_s):
    i = pl.program_id(0)
    e = be_ref[i]
    prev = be_ref[jnp.maximum(i - 1, 0)]

    @pl.when((i == 0) | (e != prev))
    def _():
        wg_s[...] = wg_ref[0, 0].astype(BF16)
        wu_s[...] = wu_ref[0, 0].astype(BF16)
        wd_s[...] = wd_ref[0, 0].astype(BF16)

    @pl.when(i < nu_ref[0])
    def _():
        x = x_ref[...]
        a = jnp.dot(x, wg_s[...], preferred_element_type=F32)
        b = jnp.dot(x, wu_s[...], preferred_element_type=F32)
        hid = (_silu(a) * b).astype(BF16)
        o_ref[...] = jnp.dot(hid, wd_s[...], preferred_element_type=F32).astype(o_ref.dtype)

    @pl.when(i >= nu_ref[0])
    def _():
        o_ref[...] = jnp.zeros_like(o_ref)


def _moe_experts(xs, blk_expert, n_used, w_gate, w_up, w_down, layer, n_blocks):
    d = xs.shape[1]
    hid = w_gate.shape[3]
    grid_spec = pltpu.PrefetchScalarGridSpec(
        num_scalar_prefetch=2,
        grid=(n_blocks,),
        in_specs=[pl.BlockSpec((MOE_BLK, d), lambda i, be, nu: (i, 0)),
                  pl.BlockSpec((1, 1, d, hid), lambda i, be, nu: (layer, be[i], 0, 0)),
                  pl.BlockSpec((1, 1, d, hid), lambda i, be, nu: (layer, be[i], 0, 0)),
                  pl.BlockSpec((1, 1, hid, d), lambda i, be, nu: (layer, be[i], 0, 0))],
        out_specs=pl.BlockSpec((MOE_BLK, d), lambda i, be, nu: (i, 0)),
        scratch_shapes=[pltpu.VMEM((d, hid), BF16), pltpu.VMEM((d, hid), BF16), pltpu.VMEM((hid, d), BF16)],
    )
    return pl.pallas_call(
        _moe_kernel,
        grid_spec=grid_spec,
        out_shape=jax.ShapeDtypeStruct((n_blocks * MOE_BLK, d), BF16),
        compiler_params=_cparams("arbitrary"),
        name="moe_experts",
    )(blk_expert, n_used, xs, w_gate, w_up, w_down)


def _moe(h2, logits, w_gate, w_up, w_down, layer):
    n_tok, d = h2.shape
    routed, counts = _route(logits, 512)
    gates = routed[:, 4:6]
    dest, row_tok, blk_expert, n_used, n_blocks = _dispatch(routed, counts, n_tok)
    rows = lambda a, idx: a.at[idx].get(mode='promise_in_bounds')
    xs = rows(h2, row_tok)
    ys = _moe_experts(xs, blk_expert, n_used, w_gate, w_up, w_down, layer, n_blocks)
    return rows(ys, dest[:, 0]), rows(ys, dest[:, 1]), gates


def _rw_kernel(p_ref, mu_ref, pv_ref, wwa_ref, g2_ref, bd_ref, tri_ref, o_ref, st_ref, last_ref, *, tt):
    i = pl.program_id(1)
    L = RW_L
    W = RW_WIDTH

    @pl.when(i == 0)
    def _():
        st_ref[...] = jnp.zeros_like(st_ref)
        last_ref[...] = jnp.zeros_like(last_ref)

    p = p_ref[...].astype(F32)
    row = lax.broadcasted_iota(jnp.int32, p.shape, 0)
    prev = jnp.where(row == 0, jnp.broadcast_to(last_ref[0:1, :], p.shape), pltpu.roll(p, 1, axis=0))
    last_ref[0:1, :] = p[tt - 1:tt, :]
    p = p + (prev - p) * mu_ref[...]
    r = p[:, 0:W]
    k = p[:, W:2 * W]
    v = p[:, 2 * W:3 * W]
    lw = p[:, 3 * W:3 * W + LANES]
    xg = p[:, 3 * W + LANES:3 * W + 2 * LANES]
    w0, a0, k_k, k_a, r_k, ln_w, ln_b = (pv_ref[j:j + 1, :] for j in range(7))
    lane = lax.broadcasted_iota(jnp.int32, lw.shape, 1)
    wa = _bdot(jnp.where(lane < 64, jnp.tanh(lw), lw), wwa_ref[...])
    w = -_softplus(-(w0 + wa[:, :W])) - 0.5
    logd = -jnp.exp(w)
    lr = _sigmoid(a0 + wa[:, W:])
    gate = _bdot(_sigmoid(xg), g2_ref[...])
    bd = bd_ref[...]
    kk = k * k_k
    kk = kk / jnp.maximum(jnp.sqrt(_bdot(kk * kk, bd)), 1e-12)
    k = k * (1.0 + (lr - 1.0) * k_a)
    av = -kk
    bv = kk * lr

    tri = tri_ref[...]
    lane_p = lax.broadcasted_iota(jnp.int32, (L, LANES), 1)
    m0 = lane_p < RW_HEAD
    ri = lax.broadcasted_iota(jnp.int32, (2 * L, 2 * L), 0)
    ci = lax.broadcasted_iota(jnp.int32, (2 * L, 2 * L), 1)
    same = (ri // L) == (ci // L)
    mask_s = same & (ci < ri)
    mask_i = same & (ci <= ri)
    mask_s2 = jnp.concatenate([mask_s, mask_s], axis=1)
    mask_i2 = jnp.concatenate([mask_i, mask_i], axis=1)
    eye = (ri == ci).astype(F32)

    def stack(x):
        return jnp.concatenate([jnp.where(m0, x, 0.0), jnp.where(m0, 0.0, x)], axis=0)

    n_chunks = tt // L
    n_pairs = RW_HEADS // 2
    aa, rr, vv, bk2, bkg, gam = [], [], [], [], [], []
    for c in range(n_chunks):
        rows = slice(c * L, (c + 1) * L)
        ld = logd[rows]
        cs = _dot_exact_lhs(tri, ld)
        total = cs[L - 1:L, :]
        e_out = jnp.exp(-cs)
        e_end = jnp.exp(total - cs)
        at = av[rows] * jnp.exp(cs - ld)
        rt = r[rows] * jnp.exp(cs)
        bt = bv[rows] * e_out
        kt = k[rows] * e_out
        bg = bv[rows] * e_end
        kg = k[rows] * e_end
        g_all = jnp.exp(total)
        for hp in range(n_pairs):
            ls = slice(hp * LANES, (hp + 1) * LANES)
            aa.append(stack(at[:, ls]).astype(BF16))
            rr.append(stack(rt[:, ls]))
            vv.append(stack(v[rows, ls]).astype(BF16))
            bk2.append(jnp.concatenate([bt[:, ls], bt[:, ls], kt[:, ls], kt[:, ls]], axis=0).astype(BF16))
            bkg.append(jnp.concatenate([stack(bg[:, ls]), stack(kg[:, ls])], axis=0).astype(BF16))
            gam.append(g_all[:, ls])
    n_sys = len(aa)
    rng = range(n_sys)
    n_a = [jnp.where(mask_s2, _bdot_nt(aa[i], bk2[i]), 0.0) for i in rng]
    n_r = [jnp.where(mask_i2, _bdot_nt(rr[i], bk2[i]), 0.0).astype(BF16) for i in rng]
    n_ab = [n_a[i][:, :2 * L] for i in rng]
    n_ak = [n_a[i][:, 2 * L:].astype(BF16) for i in rng]
    n_rb = [n_r[i][:, :2 * L] for i in rng]
    tinv = [eye + n_ab[i] for i in rng]
    pw = n_ab
    for _ in range(int(math.log2(L)) - 1):
        pw = [_bdot(pw[i], pw[i]) for i in rng]
        tinv = [tinv[i] + _bdot(tinv[i], pw[i]) for i in rng]
    tinv = [tinv[i].astype(BF16) for i in rng]
    tav = [_bdot(tinv[i], _bdot(n_ak[i], vv[i])).astype(BF16) for i in rng]
    uv = [jnp.concatenate([tav[i], vv[i]], axis=0) for i in rng]
    g_op = [_bdot_tn(tinv[i], bkg[i][:2 * L]) for i in rng]
    mg = [_bdot_tn(aa[i], g_op[i]).astype(BF16) for i in rng]
    w1 = [_bdot(n_rb[i], tinv[i]) for i in rng]
    ra = [(rr[i] + _bdot(w1[i], aa[i])).astype(BF16) for i in rng]
    y0 = [_bdot(n_r[i], uv[i]) for i in rng]
    s0 = [_bdot_tn(uv[i], bkg[i]) for i in rng]
    for c in range(n_chunks):
        rows = slice(c * L, (c + 1) * L)
        y_parts = []
        for hp in range(n_pairs):
            i = c * n_pairs + hp
            s = st_ref[hp]
            sb = s.astype(BF16)
            yst = _bdot_nt(ra[i], sb) + y0[i]
            y_parts.append(yst[:L] + yst[L:])
            st_ref[hp] = s * gam[i] + jnp.dot(sb, mg[i], preferred_element_type=F32) + s0[i]
        y = jnp.concatenate(y_parts, axis=1)
        mean = _bdot(y, bd) * (1.0 / RW_HEAD)
        yc = y - mean
        var = _bdot(yc * yc, bd) * (1.0 / RW_HEAD)
        yn = yc * lax.rsqrt(var + RW_LN_EPS) * ln_w + ln_b
        bonus = _bdot(r[rows] * k[rows] * r_k, bd) * v[rows]
        o_ref[rows, :] = ((yn + bonus) * gate[rows]).astype(o_ref.dtype)


def _rwkv7(p, mu, pv, wwa, g2, bsz, seq):
    tt = RW_TT
    nt = seq // tt
    L = RW_L
    hid = jnp.arange(RW_WIDTH) // RW_HEAD
    bd = (hid[:, None] == hid[None, :]).astype(BF16)
    tri = (jnp.arange(L)[:, None] >= jnp.arange(L)[None, :]).astype(BF16)
    fix = lambda b, i: (0, 0)
    return pl.pallas_call(
        functools.partial(_rw_kernel, tt=tt),
        grid=(bsz, nt),
        in_specs=[pl.BlockSpec((tt, RW_IN), lambda b, i: (b * nt + i, 0)),
                  pl.BlockSpec(mu.shape, fix), pl.BlockSpec(pv.shape, fix), pl.BlockSpec(wwa.shape, fix),
                  pl.BlockSpec(g2.shape, fix), pl.BlockSpec(bd.shape, fix), pl.BlockSpec(tri.shape, fix)],
        out_specs=pl.BlockSpec((tt, RW_WIDTH), lambda b, i: (b * nt + i, 0)),
        out_shape=jax.ShapeDtypeStruct((bsz * seq, RW_WIDTH), BF16),
        scratch_shapes=[pltpu.VMEM((RW_HEADS // 2, 2 * RW_HEAD, LANES), F32), pltpu.VMEM((8, RW_IN), F32)],
        compiler_params=_cparams("arbitrary", "arbitrary"),
        name="rwkv7",
    )(p, mu, pv, wwa, g2, bd, tri)


def _ssd_kernel(p_ref, cw_ref, cb_ref, hv_ref, dl_ref, nw_ref, tri_ref, o_ref, st_ref, tail_ref, *, L):
    i = pl.program_id(1)

    @pl.when(i == 0)
    def _():
        st_ref[...] = jnp.zeros_like(st_ref)
        tail_ref[...] = jnp.zeros_like(tail_ref)

    z = p_ref[:, 0:M2_INNER].astype(F32)
    xin = p_ref[:, M2_INNER:M2_INNER + M2_CONV_DIM].astype(F32)
    dt_raw = p_ref[:, M2_INNER + M2_CONV_DIM:M2_INNER + M2_CONV_DIM + LANES].astype(F32)
    tail = tail_ref[...]
    tail_ref[...] = xin[L - 8:L, :]
    row8 = lax.broadcasted_iota(jnp.int32, (8, M2_CONV_DIM), 0)
    conv = xin * cw_ref[M2_CONV - 1:M2_CONV, :] + cb_ref[...]
    for j in range(1, M2_CONV):
        rolled = pltpu.roll(xin, j, axis=0)
        head = jnp.where(row8 < j, pltpu.roll(tail, j, axis=0), rolled[0:8])
        shifted = jnp.concatenate([head, rolled[8:]], axis=0)
        conv = conv + shifted * cw_ref[M2_CONV - 1 - j:M2_CONV - j, :]
    xbc = _silu(conv)
    xs = xbc[:, 0:M2_INNER]
    dt = _softplus(dt_raw + hv_ref[0:1, :])
    adt = dt * hv_ref[1:2, :]
    cs = _dot_exact_lhs(tri_ref[...], adt)
    cs_t = jnp.transpose(cs)
    total = cs[L - 1:L, :]
    dec_in = jnp.exp(cs)
    dec_out = jnp.exp(total - cs)
    dec_all = jnp.exp(total)
    lane = lax.broadcasted_iota(jnp.int32, (L, LANES), 1)
    first = lane < 64
    rowp = lax.broadcasted_iota(jnp.int32, (LANES, M2_STATE), 0) < 64
    li = lax.broadcasted_iota(jnp.int32, (L, L), 0)
    si = lax.broadcasted_iota(jnp.int32, (L, L), 1)
    causal = li >= si
    y_parts = []
    for hp in range(M2_HEADS // 2):
        g = hp // 2
        h0, h1 = 2 * hp, 2 * hp + 1
        bm = xbc[:, M2_INNER + g * M2_STATE:M2_INNER + (g + 1) * M2_STATE]
        cm = xbc[:, M2_INNER + 2 * M2_STATE + g * M2_STATE:M2_INNER + 2 * M2_STATE + (g + 1) * M2_STATE]
        x_p = xs[:, hp * LANES:(hp + 1) * LANES]
        sel = lambda a: jnp.where(first, a[:, h0:h0 + 1], a[:, h1:h1 + 1])
        xd = x_p * sel(dt)
        cb = _bdot_nt(cm, bm)
        yd = []
        for h in (h0, h1):
            lmat = jnp.where(causal, jnp.exp(cs[:, h:h + 1] - cs_t[h:h + 1, :]), 0.0)
            yd.append(_bdot(cb * lmat, xd))
        s = st_ref[hp]
        y_off = _bdot_nt(cm, s) * sel(dec_in)
        y_parts.append(jnp.where(first, yd[0], yd[1]) + y_off)
        s_new = _bdot_tn(xd * sel(dec_out), bm)
        st_ref[hp] = s * jnp.where(rowp, dec_all[:, h0:h0 + 1], dec_all[:, h1:h1 + 1]) + s_new
    y = jnp.concatenate(y_parts, axis=1) + dl_ref[...] * xs
    y = y * _silu(z)
    half = M2_INNER // 2
    outs = []
    for g in range(2):
        yg = y[:, g * half:(g + 1) * half]
        outs.append(yg * lax.rsqrt(jnp.mean(yg * yg, axis=-1, keepdims=True) + RMS_EPS))
    o_ref[...] = (jnp.concatenate(outs, axis=1) * nw_ref[...]).astype(o_ref.dtype)


def _ssd(p, conv_w, conv_b, hv, d_lanes, norm_w, bsz, seq):
    L = M2_L
    nt = seq // L
    tri = (jnp.arange(L)[:, None] >= jnp.arange(L)[None, :]).astype(BF16)
    fix = lambda b, i: (0, 0)
    return pl.pallas_call(
        functools.partial(_ssd_kernel, L=L),
        grid=(bsz, nt),
        in_specs=[pl.BlockSpec((L, M2_IN_PAD), lambda b, i: (b * nt + i, 0)),
                  pl.BlockSpec(conv_w.shape, fix), pl.BlockSpec(conv_b.shape, fix), pl.BlockSpec(hv.shape, fix),
                  pl.BlockSpec(d_lanes.shape, fix), pl.BlockSpec(norm_w.shape, fix), pl.BlockSpec(tri.shape, fix)],
        out_specs=pl.BlockSpec((L, M2_INNER), lambda b, i: (b * nt + i, 0)),
        out_shape=jax.ShapeDtypeStruct((bsz * seq, M2_INNER), BF16),
        scratch_shapes=[pltpu.VMEM((M2_HEADS // 2, LANES, M2_STATE), F32), pltpu.VMEM((8, M2_CONV_DIM), F32)],
        compiler_params=_cparams("arbitrary", "arbitrary"),
        name="ssd",
    )(p, conv_w, conv_b, hv, d_lanes, norm_w, tri)


def _final_kernel(x_ref, ya_ref, yb_ref, g_ref, gate_ref, nw_ref, o_ref):
    g = g_ref[...]
    moe = g[:, 0:1] * ya_ref[...].astype(F32) + g[:, 1:2] * yb_ref[...].astype(F32)
    x = x_ref[...] + gate_ref[0] * moe
    ms = jnp.mean(x * x, axis=-1, keepdims=True)
    o_ref[...] = x * lax.rsqrt(ms + RMS_EPS) * nw_ref[...]


def _final(x, ya, yb, g, gate, nw, seq, tm):
    n_tok, d = x.shape
    tpb = seq // tm
    row = lambda i: (i, 0)
    return pl.pallas_call(
        _final_kernel,
        grid=(n_tok // tm,),
        in_specs=[pl.BlockSpec((tm, d), row), pl.BlockSpec((tm, d), row), pl.BlockSpec((tm, d), row),
                  pl.BlockSpec((tm, g.shape[1]), row), pl.BlockSpec((1, 1, d), lambda i: (i // tpb, 0, 0)),
                  pl.BlockSpec((1, d), lambda i: (0, 0))],
        out_specs=pl.BlockSpec((tm, d), row),
        out_shape=jax.ShapeDtypeStruct((n_tok, d), F32),
        compiler_params=_cparams("arbitrary"),
        name="final_norm",
    )(x, ya, yb, g, gate, nw)


def _router_weights(w_grp, b_grp, w_exp, b_exp):
    d = w_grp.shape[0]
    pad = LANES - MOE_GROUPS - MOE_EXPERTS
    w = jnp.concatenate([w_grp, w_exp, jnp.zeros((d, pad), F32)], axis=1)
    b = jnp.concatenate([b_grp, b_exp, jnp.zeros((pad,), F32)])[None, :]
    return w, b


def kernel(x, c, ada_w, ada_b, norm_mix_w, norm_ffn_w, even_w_in, even_w_out, s5_a_re, s5_a_im, s5_log_dt, s5_b_re, s5_b_im, s5_c_re, s5_c_im, s5_d, s5_w_glu, odd_w_in, odd_w_out, rw_mu, rw_w0, rw_w2, rw_a0, rw_a2, rw_g2, rw_k_k, rw_k_a, rw_r_k, rw_ln_w, rw_ln_b, m2_conv_w, m2_conv_b, m2_dt_bias, m2_a_log, m2_d, m2_norm_w, moe_w_grp, moe_b_grp, moe_w_exp, moe_b_exp, moe_w_gate, moe_w_up, moe_w_down, final_norm_w):
    bsz, seq, d = x.shape
    n_tok = bsz * seq
    tm = 512
    xt = x.reshape(n_tok, d)
    ada = _ada_params(c, ada_w, ada_b)
    mods = [[ada[i, :, j * d:(j + 1) * d].reshape(bsz, 1, d) for j in range(6)] for i in range(2)]

    sh1, sc1, g1, sh2, sc2, g2 = mods[0]
    u, q, k, v = _in_proj(xt, norm_mix_w[0][None, :], sh1, sc1, even_w_in[0].astype(BF16),
                          ((0, 512), (512, 1024), (1024, 1536), (1536, 2048)), (F32, BF16, BF16, BF16), seq, tm)
    tables = _s5_tables(s5_a_re[0], s5_a_im[0], s5_log_dt[0], s5_b_re[0], s5_b_im[0], s5_c_re[0], s5_c_im[0], s5_d[0])
    y_s5 = _s5_scan(u, tables, bsz, seq)
    y_sb = _sb_attention(q, k, v, bsz, seq)
    wr, br = _router_weights(moe_w_grp[0], moe_b_grp[0], moe_w_exp[0], moe_b_exp[0])
    x1, h2, logits = _mix_out(y_s5, y_sb, xt, g1, s5_w_glu[0].astype(BF16), even_w_out[0].astype(BF16),
                              norm_ffn_w[0][None, :], sh2, sc2, wr, br, seq, tm)
    ya, yb, gates = _moe(h2, logits, moe_w_gate, moe_w_up, moe_w_down, 0)

    gate_prev = g2
    sh1, sc1, g1, sh2, sc2, g2 = mods[1]
    w_in = jnp.concatenate([odd_w_in[0], jnp.zeros((d, M2_IN_PAD - M2_IN), F32)], axis=1).astype(BF16)
    x2, p_rw, p_m2 = _in_proj(x1, norm_mix_w[1][None, :], sh1, sc1, w_in,
                              ((0, RW_IN), (RW_IN, RW_IN + M2_IN_PAD)), (BF16, BF16), seq, tm,
                              add=(ya, yb, gates, gate_prev))
    pv = jnp.stack([rw_w0[0], rw_a0[0], rw_k_k[0], rw_k_a[0], rw_r_k[0].reshape(-1), rw_ln_w[0], rw_ln_b[0],
                    jnp.zeros((RW_WIDTH,), F32)])
    zl = jnp.zeros((64, RW_WIDTH), F32)
    wwa = jnp.concatenate([jnp.concatenate([rw_w2[0], zl], axis=1),
                           jnp.concatenate([zl, rw_a2[0]], axis=1)], axis=0).astype(BF16)
    y_rw = _rwkv7(p_rw, rw_mu[0][None, :], pv, wwa, rw_g2[0].astype(BF16), bsz, seq)
    hv = jnp.zeros((8, LANES), F32)
    hv = hv.at[0, :M2_HEADS].set(m2_dt_bias[0]).at[1, :M2_HEADS].set(-jnp.exp(m2_a_log[0]))
    y_m2 = _ssd(p_m2, m2_conv_w[0], m2_conv_b[0][None, :], hv, jnp.repeat(m2_d[0], 64)[None, :],
                m2_norm_w[0][None, :], bsz, seq)
    wr, br = _router_weights(moe_w_grp[1], moe_b_grp[1], moe_w_exp[1], moe_b_exp[1])
    x3, h2, logits = _mix_out(y_rw, y_m2, x2, g1, None, odd_w_out[0].astype(BF16),
                              norm_ffn_w[1][None, :], sh2, sc2, wr, br, seq, tm)
    ya, yb, gates = _moe(h2, logits, moe_w_gate, moe_w_up, moe_w_down, 1)
    out = _final(x3, ya, yb, gates, g2, final_norm_w[None, :], seq, tm)
    return out.reshape(bsz, seq, d)
```

```python
import functools
import math

import jax
import jax.numpy as jnp
from jax import lax
from jax.experimental import pallas as pl
from jax.experimental.pallas import tpu as pltpu

F32 = jnp.float32
BF16 = jnp.bfloat16

D_MODEL = 1024
RMS_EPS = 1e-6
LANES = 128
VMEM_LIMIT = 56 * 1024 * 1024

S5_WIDTH = 512
S5_P = 16
S5_G = 32
S5_N = 64
S5_L = 16
S5_GB = 8
SB_WIDTH = 512
SB_HEAD = 64
SB_T = 256
SB_FADE = 160.0

RW_WIDTH = 512
RW_HEAD = 64
RW_HEADS = 8
RW_IN = 1792
RW_LN_EPS = 64e-5
RW_L = 64
RW_TT = 256

M2_INNER = 512
M2_HEADS = 8
M2_STATE = 128
M2_CONV = 4
M2_CONV_DIM = 1024
M2_IN = 1544
M2_IN_PAD = 1664
M2_L = 256

MOE_GROUPS = 4
MOE_PER_GROUP = 8
MOE_EXPERTS = 32
MOE_TOPK = 2
MOE_HIDDEN = 512
MOE_BLK = 512


def _cparams(*sem):
    return pltpu.CompilerParams(dimension_semantics=sem, vmem_limit_bytes=VMEM_LIMIT)


def _bdot(a, b):
    return jnp.dot(a.astype(BF16), b.astype(BF16), preferred_element_type=F32)


def _bdot_nt(a, b):
    return lax.dot_general(a.astype(BF16), b.astype(BF16), (((1,), (1,)), ((), ())),
                           preferred_element_type=F32)


def _bdot_tn(a, b):
    return lax.dot_general(a.astype(BF16), b.astype(BF16), (((0,), (0,)), ((), ())),
                           preferred_element_type=F32)


def _split2(a):
    hi = a.astype(BF16)
    lo = (a - hi.astype(F32)).astype(BF16)
    return hi, lo


def _dot3(a, b):
    ah, al = _split2(a)
    bh, bl = _split2(b)
    d = functools.partial(jnp.dot, preferred_element_type=F32)
    return d(ah, bh) + d(ah, bl) + d(al, bh)


def _dot_exact_rhs(a, b01):
    ah, al = _split2(a)
    d = functools.partial(jnp.dot, preferred_element_type=F32)
    return d(ah, b01) + d(al, b01)


def _dot_exact_lhs(a01, b):
    bh, bl = _split2(b)
    d = functools.partial(jnp.dot, preferred_element_type=F32)
    return d(a01, bh) + d(a01, bl)


def _sigmoid(x):
    return 1.0 / (1.0 + jnp.exp(-x))


def _softplus(x):
    return jnp.maximum(x, 0.0) + jnp.log(1.0 + jnp.exp(-jnp.abs(x)))


def _silu(x):
    return x * _sigmoid(x)


def _gelu_tanh(x):
    c = math.sqrt(2.0 / math.pi)
    return 0.5 * x * (1.0 + jnp.tanh(c * (x + 0.044715 * (x * x * x))))


def _rms_mod(x, nw, shift, scale):
    ms = jnp.mean(x * x, axis=-1, keepdims=True)
    return (x * lax.rsqrt(ms + RMS_EPS)) * nw * (1.0 + scale) + shift


def _ada_kernel(c_ref, w_ref, b_ref, o_ref):
    cond = _silu(c_ref[...])
    o_ref[0] = _dot3(cond, w_ref[0]) + b_ref[0]


def _ada_params(c, ada_w, ada_b):
    depth, d, n = ada_w.shape
    bsz = c.shape[0]
    tn = 1024
    return pl.pallas_call(
        _ada_kernel,
        grid=(depth, n // tn),
        in_specs=[pl.BlockSpec((bsz, d), lambda i, j: (0, 0)),
                  pl.BlockSpec((1, d, tn), lambda i, j: (i, 0, j)),
                  pl.BlockSpec((1, 1, tn), lambda i, j: (i, 0, j))],
        out_specs=pl.BlockSpec((1, bsz, tn), lambda i, j: (i, 0, j)),
        out_shape=jax.ShapeDtypeStruct((depth, bsz, n), F32),
        compiler_params=_cparams("arbitrary", "arbitrary"),
        name="ada_params",
    )(c, ada_w, ada_b.reshape(depth, 1, n))


def _in_proj_kernel(*refs, has_add, splits):
    if has_add:
        x_ref, ya_ref, yb_ref, g_ref, gate_ref, nw_ref, sh_ref, sc_ref, w_ref = refs[:9]
        outs = refs[9:]
        x = x_ref[...]
        g = g_ref[...]
        moe = g[:, 4:5] * ya_ref[...].astype(F32) + g[:, 5:6] * yb_ref[...].astype(F32)
        x = x + gate_ref[0] * moe
        outs[0][...] = x
        outs = outs[1:]
    else:
        x_ref, nw_ref, sh_ref, sc_ref, w_ref = refs[:5]
        outs = refs[5:]
        x = x_ref[...]
    h = _rms_mod(x, nw_ref[...], sh_ref[0], sc_ref[0]).astype(BF16)
    for o_ref, (c0, c1) in zip(outs, splits):
        o_ref[...] = jnp.dot(h, w_ref[:, c0:c1], preferred_element_type=F32).astype(o_ref.dtype)


def _in_proj(x, nw, shift, scale, w, splits, dtypes, seq, tm, add=None):
    n_tok, d = x.shape
    tpb = seq // tm
    row = lambda i: (i, 0)
    bat = lambda i: (i // tpb, 0, 0)
    fix = lambda i: (0, 0)
    in_specs = [pl.BlockSpec((tm, d), row)]
    args = [x]
    out_shape = []
    out_specs = []
    if add is not None:
        ya, yb, g, gate = add
        in_specs += [pl.BlockSpec((tm, d), row), pl.BlockSpec((tm, d), row),
                     pl.BlockSpec((tm, g.shape[1]), row), pl.BlockSpec((1, 1, d), bat)]
        args += [ya, yb, g, gate]
        out_shape.append(jax.ShapeDtypeStruct((n_tok, d), F32))
        out_specs.append(pl.BlockSpec((tm, d), row))
    in_specs += [pl.BlockSpec((1, d), fix), pl.BlockSpec((1, 1, d), bat), pl.BlockSpec((1, 1, d), bat),
                 pl.BlockSpec(w.shape, fix)]
    args += [nw, shift, scale, w]
    for (c0, c1), dt in zip(splits, dtypes):
        out_shape.append(jax.ShapeDtypeStruct((n_tok, c1 - c0), dt))
        out_specs.append(pl.BlockSpec((tm, c1 - c0), row))
    return pl.pallas_call(
        functools.partial(_in_proj_kernel, has_add=add is not None, splits=splits),
        grid=(n_tok // tm,),
        in_specs=in_specs, out_specs=out_specs, out_shape=out_shape,
        compiler_params=_cparams("arbitrary"),
        name="in_proj",
    )(*args)


def _s5_tables(a_re, a_im, log_dt, b_re, b_im, c_re, c_im, d_skip):
    hp = lax.Precision.HIGHEST
    L = S5_L
    a_re = jnp.minimum(a_re.astype(F32), -1e-4)
    a_im = a_im.astype(F32)
    dt = jnp.exp(log_dt.astype(F32))[:, None]
    mag = jnp.exp(dt * a_re)
    abar_re, abar_im = mag * jnp.cos(dt * a_im), mag * jnp.sin(dt * a_im)
    den = a_re * a_re + a_im * a_im
    num_re, num_im = abar_re - 1.0, abar_im
    coef_re = (num_re * a_re + num_im * a_im) / den
    coef_im = (num_im * a_re - num_re * a_im) / den
    b_re = b_re.astype(F32)
    b_im = b_im.astype(F32)
    bb_re = coef_re[..., None] * b_re - coef_im[..., None] * b_im
    bb_im = coef_re[..., None] * b_im + coef_im[..., None] * b_re
    c_re = c_re.astype(F32)
    c_im = c_im.astype(F32)
    tau = jnp.arange(L + 1, dtype=F32)[None, :, None]
    pmag = jnp.exp(tau * (dt * a_re)[:, None, :])
    pw_re = pmag * jnp.cos(tau * (dt * a_im)[:, None, :])
    pw_im = pmag * jnp.sin(tau * (dt * a_im)[:, None, :])
    cl_re = c_re[:, None] * pw_re[:, :, None, :] - c_im[:, None] * pw_im[:, :, None, :]
    cl_im = c_re[:, None] * pw_im[:, :, None, :] + c_im[:, None] * pw_re[:, :, None, :]
    taps = (jnp.einsum('gtpn,gnq->gtpq', cl_re[:, :L], bb_re, precision=hp)
            - jnp.einsum('gtpn,gnq->gtpq', cl_im[:, :L], bb_im, precision=hp))
    nb = S5_G // S5_GB

    def compact(x):
        return x.reshape(nb, S5_GB, L, S5_P, 2 * S5_N).transpose(0, 2, 1, 3, 4).reshape(nb, L, LANES, 2 * S5_N)

    eye = jnp.eye(S5_GB, dtype=F32)
    dlag = jnp.einsum('bgtpq,gh->btgqhp', taps.reshape(nb, S5_GB, L, S5_P, S5_P), eye)
    dlag = dlag.reshape(nb, L, LANES, LANES)
    rev = L - 1 - jnp.arange(L)
    e_re = pw_re[:, rev][:, :, :, None] * bb_re[:, None] - pw_im[:, rev][:, :, :, None] * bb_im[:, None]
    e_im = pw_re[:, rev][:, :, :, None] * bb_im[:, None] + pw_im[:, rev][:, :, :, None] * bb_re[:, None]
    emap = compact(jnp.concatenate([e_re.transpose(0, 1, 3, 2), e_im.transpose(0, 1, 3, 2)], axis=-1))
    qmap = compact(jnp.concatenate([cl_re[:, 1:L + 1], -cl_im[:, 1:L + 1]], axis=-1))
    lr = pw_re[:, L]
    li = pw_im[:, L]
    lam_a = jnp.concatenate([lr, lr], axis=-1).reshape(nb, 1, S5_GB * 2 * S5_N)
    lam_b = jnp.concatenate([-li, li], axis=-1).reshape(nb, 1, S5_GB * 2 * S5_N)
    dvec = jnp.tile(d_skip.astype(F32).reshape(nb, 1, LANES), (1, 1, L))
    return dlag.astype(BF16), emap.astype(BF16), qmap.astype(BF16), lam_a, lam_b, dvec


def _s5_kernel(u_ref, dlag_ref, emap_ref, qmap_ref, la_ref, lb_ref, dv_ref, o_ref, toep_ref, esp_ref, qsp_ref, h_ref,
               *, n_rows):
    L = S5_L
    ns = 2 * S5_GB * S5_N

    @pl.when(pl.program_id(1) == 0)
    def _():
        toep_ref[...] = jnp.zeros_like(toep_ref)
        esp_ref[...] = jnp.zeros_like(esp_ref)
        qsp_ref[...] = jnp.zeros_like(qsp_ref)
        for s in range(L):
            for t in range(s, L):
                toep_ref[s * LANES:(s + 1) * LANES, t * LANES:(t + 1) * LANES] = dlag_ref[0, t - s]
            for g in range(S5_GB):
                rows = slice(s * LANES + g * S5_P, s * LANES + (g + 1) * S5_P)
                cols = slice(g * 2 * S5_N, (g + 1) * 2 * S5_N)
                esp_ref[rows, cols] = emap_ref[0, s, g * S5_P:(g + 1) * S5_P, :]
                qsp_ref[rows, cols] = qmap_ref[0, s, g * S5_P:(g + 1) * S5_P, :]

    up = jnp.concatenate([u_ref[pl.ds(s, n_rows, stride=L), :] for s in range(L)], axis=1)
    ub = up.astype(BF16)
    h_ref[...] = jnp.dot(ub, esp_ref[...], preferred_element_type=F32)
    la = la_ref[0]
    lb = lb_ref[0]
    is_re = (lax.broadcasted_iota(jnp.int32, (1, ns), 1) % (2 * S5_N)) < S5_N

    def step(c8, h):
        rows = pl.ds(pl.multiple_of(c8 * 8, 8), 8)
        e = h_ref[rows, :]
        ent = []
        for j in range(8):
            ent.append(h)
            swapped = jnp.where(is_re, pltpu.roll(h, ns - S5_N, axis=1), pltpu.roll(h, S5_N, axis=1))
            h = h * la + swapped * lb + e[j:j + 1, :]
        h_ref[rows, :] = jnp.concatenate(ent, axis=0)
        return h

    lax.fori_loop(0, n_rows // 8, step, jnp.zeros((1, ns), F32))
    hi, lo = _split2(h_ref[...])
    qs = qsp_ref[...]
    nt = lambda a, b: lax.dot_general(a, b, (((1,), (1,)), ((), ())), preferred_element_type=F32)
    y_off = nt(hi, qs) + nt(lo, qs)
    wide = 2 * LANES
    for j in range(L * LANES // wide):
        k_end = (j + 1) * wide
        cols = slice(j * wide, k_end)
        y = jnp.dot(ub[:, :k_end], toep_ref[:k_end, cols], preferred_element_type=F32)
        y = y + y_off[:, cols] + up[:, cols] * dv_ref[0, :, cols]
        for t in range(2 * j, 2 * j + 2):
            o_ref[pl.ds(t, n_rows, stride=L), :] = y[:, (t - 2 * j) * LANES:(t - 2 * j + 1) * LANES]


def _s5_scan(u, tables, bsz, seq):
    dlag, emap, qmap, lam_a, lam_b, dvec = tables
    L = S5_L
    n_rows = seq // L
    nb = S5_WIDTH // LANES
    ns = 2 * S5_GB * S5_N
    blk3 = lambda j, b: (j, 0, 0)
    blk4 = lambda j, b: (j, 0, 0, 0)
    return pl.pallas_call(
        functools.partial(_s5_kernel, n_rows=n_rows),
        grid=(nb, bsz),
        in_specs=[pl.BlockSpec((seq, LANES), lambda j, b: (b, j)),
                  pl.BlockSpec((1, L, LANES, LANES), blk4), pl.BlockSpec((1, L, LANES, 2 * S5_N), blk4),
                  pl.BlockSpec((1, L, LANES, 2 * S5_N), blk4),
                  pl.BlockSpec((1, 1, ns), blk3), pl.BlockSpec((1, 1, ns), blk3), pl.BlockSpec((1, 1, L * LANES), blk3)],
        out_specs=pl.BlockSpec((seq, LANES), lambda j, b: (b, j)),
        out_shape=jax.ShapeDtypeStruct((bsz * seq, S5_WIDTH), F32),
        scratch_shapes=[pltpu.VMEM((L * LANES, L * LANES), BF16), pltpu.VMEM((L * LANES, ns), BF16),
                        pltpu.VMEM((L * LANES, ns), BF16), pltpu.VMEM((n_rows, ns), F32)],
        compiler_params=_cparams("arbitrary", "arbitrary"),
        name="s5_scan",
    )(u, dlag, emap, qmap, lam_a, lam_b, dvec)


def _sb_kernel(q_ref, k_ref, v_ref, tri_ref, o_ref, qh_ref, hl_ref, z_ref, w_ref, acc_ref, car_ref, *, t):
    qb = pl.program_id(1)
    heads = SB_WIDTH // SB_HEAD
    lane = lax.broadcasted_iota(jnp.int32, (t, LANES), 1)
    low = lane < SB_HEAD
    for p in range(heads // 2):
        q = (q_ref[:, p * LANES:(p + 1) * LANES].astype(F32) * (0.125 * math.log2(math.e))).astype(BF16)
        zero = jnp.zeros_like(q)
        qh_ref[2 * p] = jnp.where(low, q, zero)
        qh_ref[2 * p + 1] = jnp.where(low, zero, q)
    acc_ref[...] = jnp.zeros_like(acc_ref)
    car_ref[...] = jnp.zeros_like(car_ref)

    def scores(kb, h, slot, diagonal):
        rows = pl.ds(pl.multiple_of(kb * t, t), t)
        kblk = k_ref[rows, (h // 2) * LANES:(h // 2 + 1) * LANES]
        z = lax.dot_general(qh_ref[h], kblk, (((1,), (1,)), ((), ())), preferred_element_type=F32)
        sp = jnp.maximum(z, 0.0) + jnp.log2(1.0 + jnp.exp2(-jnp.abs(z)))
        if diagonal:
            mask = lax.broadcasted_iota(jnp.int32, (t, t), 0) > lax.broadcasted_iota(jnp.int32, (t, t), 1)
            sp = jnp.where(mask, sp, 0.0)
            z = jnp.where(mask, z, -1e30)
        hi = sp.astype(BF16)
        hl_ref[slot, :, :t] = hi
        hl_ref[slot, :, t:] = (sp - hi.astype(F32)).astype(BF16)
        z_ref[slot] = z

    def weights(h, src, dst):
        cs = jnp.dot(hl_ref[src], tri_ref[...], preferred_element_type=F32)
        car = car_ref[h]
        s = z_ref[src] - cs - jnp.concatenate([car, car], axis=1)
        w_ref[dst] = jnp.exp2(s).astype(BF16)
        car_ref[h] = car + jnp.broadcast_to(cs[:, 0:1], car.shape)

    def values(kb, h, slot):
        rows = pl.ds(pl.multiple_of(kb * t, t), t)
        vblk = v_ref[rows, (h // 2) * LANES:(h // 2 + 1) * LANES]
        acc_ref[h] += jnp.dot(w_ref[slot], vblk, preferred_element_type=F32)

    def key_tile(kb, diagonal):
        for j in range(heads + 2):
            if j < heads:
                scores(kb, j, j % 2, diagonal)
            if 1 <= j <= heads:
                weights(j - 1, (j - 1) % 2, j % 2)
            if j >= 2:
                values(kb, j - 2, (j - 1) % 2)

    def faded():
        return jnp.min(car_ref[...]) >= SB_FADE

    key_tile(qb, True)

    def more(state):
        kb, _ = state
        key_tile(kb, False)
        return kb - 1, faded()

    lax.while_loop(lambda s: (s[0] >= 0) & jnp.logical_not(s[1]), more, (qb - 1, faded()))
    for p in range(heads // 2):
        o_ref[:, p * LANES:(p + 1) * LANES] = jnp.where(low, acc_ref[2 * p], acc_ref[2 * p + 1]).astype(o_ref.dtype)


def _sb_attention(q, k, v, bsz, seq):
    t = SB_T
    nq = seq // t
    heads = SB_WIDTH // SB_HEAD
    tri = (jnp.arange(t)[:, None] >= jnp.arange(t)[None, :]).astype(BF16)
    tri2 = jnp.concatenate([tri, tri], axis=0)
    return pl.pallas_call(
        functools.partial(_sb_kernel, t=t),
        grid=(bsz, nq),
        in_specs=[pl.BlockSpec((t, SB_WIDTH), lambda b, i: (b * nq + i, 0)),
                  pl.BlockSpec((seq, SB_WIDTH), lambda b, i: (b, 0)),
                  pl.BlockSpec((seq, SB_WIDTH), lambda b, i: (b, 0)),
                  pl.BlockSpec((2 * t, t), lambda b, i: (0, 0))],
        out_specs=pl.BlockSpec((t, SB_WIDTH), lambda b, i: (b * nq + i, 0)),
        out_shape=jax.ShapeDtypeStruct((bsz * seq, SB_WIDTH), BF16),
        scratch_shapes=[pltpu.VMEM((heads, t, LANES), BF16), pltpu.VMEM((2, t, 2 * t), BF16),
                        pltpu.VMEM((2, t, t), F32), pltpu.VMEM((2, t, t), BF16),
                        pltpu.VMEM((heads, t, LANES), F32), pltpu.VMEM((heads, t, LANES), F32)],
        compiler_params=_cparams("arbitrary", "arbitrary"),
        name="sb_attention",
    )(q, k, v, tri2)


def _mix_out_kernel(*refs, glu):
    if glu:
        ya_ref, yb_ref, x_ref, gate_ref, wglu_ref, wo_ref, nw_ref, sh_ref, sc_ref, wr_ref, br_ref = refs[:11]
        outs = refs[11:]
        a = _gelu_tanh(ya_ref[...].astype(F32)).astype(BF16)
        g = jnp.dot(a, wglu_ref[...], preferred_element_type=F32)
        half = g.shape[1] // 2
        ya = (g[:, :half] * _sigmoid(g[:, half:])).astype(BF16)
    else:
        ya_ref, yb_ref, x_ref, gate_ref, wo_ref, nw_ref, sh_ref, sc_ref, wr_ref, br_ref = refs[:10]
        outs = refs[10:]
        ya = ya_ref[...]
    x1_ref, h2_ref, lg_ref = outs
    ka = ya.shape[1]
    mix = (jnp.dot(ya, wo_ref[:ka, :], preferred_element_type=F32)
           + jnp.dot(yb_ref[...], wo_ref[ka:, :], preferred_element_type=F32))
    x1 = x_ref[...] + gate_ref[0] * mix
    x1_ref[...] = x1
    h2 = _rms_mod(x1, nw_ref[...], sh_ref[0], sc_ref[0])
    h2_ref[...] = h2.astype(BF16)
    lg_ref[...] = _dot3(h2, wr_ref[...]) + br_ref[...]


def _mix_out(ya, yb, x, gate, w_glu, w_out, nw, shift, scale, w_router, b_router, seq, tm):
    n_tok, d = x.shape
    tpb = seq // tm
    row = lambda i: (i, 0)
    bat = lambda i: (i // tpb, 0, 0)
    fix = lambda i: (0, 0)
    in_specs = [pl.BlockSpec((tm, ya.shape[1]), row), pl.BlockSpec((tm, yb.shape[1]), row),
                pl.BlockSpec((tm, d), row), pl.BlockSpec((1, 1, d), bat)]
    args = [ya, yb, x, gate]
    if w_glu is not None:
        in_specs.append(pl.BlockSpec(w_glu.shape, fix))
        args.append(w_glu)
    in_specs += [pl.BlockSpec(w_out.shape, fix), pl.BlockSpec((1, d), fix), pl.BlockSpec((1, 1, d), bat),
                 pl.BlockSpec((1, 1, d), bat), pl.BlockSpec(w_router.shape, fix), pl.BlockSpec(b_router.shape, fix)]
    args += [w_out, nw, shift, scale, w_router, b_router]
    return pl.pallas_call(
        functools.partial(_mix_out_kernel, glu=w_glu is not None),
        grid=(n_tok // tm,),
        in_specs=in_specs,
        out_specs=[pl.BlockSpec((tm, d), row), pl.BlockSpec((tm, d), row), pl.BlockSpec((tm, LANES), row)],
        out_shape=[jax.ShapeDtypeStruct((n_tok, d), F32), jax.ShapeDtypeStruct((n_tok, d), BF16),
                   jax.ShapeDtypeStruct((n_tok, LANES), F32)],
        compiler_params=_cparams("arbitrary"),
        name="mix_out",
    )(*args)


def _route_kernel(lg_ref, tri_ref, o_ref, ot_ref, cnt_ref, run_ref):
    @pl.when(pl.program_id(0) == 0)
    def _():
        run_ref[...] = jnp.zeros_like(run_ref)

    lg = lg_ref[...]
    lane = lax.broadcasted_iota(jnp.int32, lg.shape, 1)
    neg = jnp.float32(-jnp.inf)
    far = jnp.int32(LANES)

    def top(x):
        m = jnp.max(x, axis=1, keepdims=True)
        return m, jnp.min(jnp.where(x == m, lane, far), axis=1, keepdims=True)

    is_grp = lane < MOE_GROUPS
    gmax, gidx = top(jnp.where(is_grp, lg, neg))
    grp_p = 1.0 / jnp.sum(jnp.where(is_grp, jnp.exp(lg - gmax), 0.0), axis=1, keepdims=True)
    first = MOE_GROUPS + MOE_PER_GROUP * gidx
    el = jnp.where((lane >= first) & (lane < first + MOE_PER_GROUP), lg, neg)
    l0, i0 = top(el)
    l1, i1 = top(jnp.where(lane == i0, neg, el))
    e1 = jnp.exp(l1 - l0)
    g0 = grp_p / (1.0 + e1)
    g1 = grp_p * e1 / (1.0 + e1)
    pick0 = lane == i0
    pick1 = lane == i1
    onehot = jnp.where(pick0 | pick1, 1.0, 0.0)
    before = jnp.dot(tri_ref[...], onehot.astype(BF16), preferred_element_type=F32) + run_ref[...]
    r0 = jnp.sum(jnp.where(pick0, before, 0.0), axis=1, keepdims=True)
    r1 = jnp.sum(jnp.where(pick1, before, 0.0), axis=1, keepdims=True)
    run = run_ref[...] + jnp.sum(onehot, axis=0, keepdims=True)
    run_ref[...] = run
    cnt_ref[...] = run
    cols = ((i0 - MOE_GROUPS).astype(F32), (i1 - MOE_GROUPS).astype(F32), r0, r1, g0, g1)
    out = jnp.zeros(lg.shape, F32)
    for j, col in enumerate(cols):
        out = jnp.where(lane == j, col, out)
    o_ref[...] = out
    ot_ref[...] = jnp.transpose(out)[0:8, :]


def _route(logits, tm):
    n_tok = logits.shape[0]
    tri = (jnp.arange(tm)[:, None] > jnp.arange(tm)[None, :]).astype(BF16)
    return pl.pallas_call(
        _route_kernel,
        grid=(n_tok // tm,),
        in_specs=[pl.BlockSpec((tm, LANES), lambda i: (i, 0)), pl.BlockSpec((tm, tm), lambda i: (0, 0))],
        out_specs=[pl.BlockSpec((tm, LANES), lambda i: (i, 0)), pl.BlockSpec((8, tm), lambda i: (0, i)),
                   pl.BlockSpec((1, LANES), lambda i: (0, 0))],
        out_shape=[jax.ShapeDtypeStruct((n_tok, LANES), F32), jax.ShapeDtypeStruct((8, n_tok), F32),
                   jax.ShapeDtypeStruct((1, LANES), F32)],
        scratch_shapes=[pltpu.VMEM((1, LANES), F32)],
        compiler_params=_cparams("arbitrary"),
        name="route",
    )(logits, tri)


def _dispatch(fields, counts, n_tok):
    n_assign = n_tok * MOE_TOPK
    n_blocks = (n_assign + MOE_EXPERTS * (MOE_BLK - 1) + MOE_BLK - 1) // MOE_BLK
    n_rows = n_blocks * MOE_BLK
    experts = fields[0:2].astype(jnp.int32)
    rank = fields[2:4].astype(jnp.int32)
    counts = counts[0, MOE_GROUPS:MOE_GROUPS + MOE_EXPERTS].astype(jnp.int32)
    padded = ((counts + MOE_BLK - 1) // MOE_BLK) * MOE_BLK
    pad_end = jnp.cumsum(padded)
    pad_start = pad_end - padded
    start = jnp.cumsum(counts) - counts
    dest = jnp.take(pad_start, experts, mode='clip') + rank
    blk_start = jnp.arange(n_blocks, dtype=jnp.int32) * MOE_BLK
    blk_expert = jnp.minimum(jnp.sum((pad_end[None, :] <= blk_start[:, None]).astype(jnp.int32), axis=1),
                             MOE_EXPERTS - 1).astype(jnp.int32)
    n_used = (pad_end[-1] // MOE_BLK).astype(jnp.int32).reshape(1)
    tok = jnp.tile(jnp.arange(n_tok, dtype=jnp.int32), MOE_TOPK)
    _, by_row = lax.sort_key_val(dest.reshape(-1), tok)
    per_blk = lambda a: jnp.repeat(jnp.take(a, blk_expert), MOE_BLK)
    row = jnp.arange(n_rows, dtype=jnp.int32)
    j = row - per_blk(pad_start)
    src_idx = jnp.clip(per_blk(start) + j, 0, n_assign - 1)
    row_tok = jnp.where(j < per_blk(counts), by_row.at[src_idx].get(mode='promise_in_bounds'), row % n_tok)
    return dest, row_tok, blk_expert, n_used, n_blocks


def _moe_kernel(be_ref, nu_ref, x_ref, wg_ref, wu_ref, wd_ref, o_ref, wg_s, wu_s, wd_s):
    i = pl.program_id(0)
    e = be_ref[i]
    prev = be_ref[jnp.maximum(i - 1, 0)]

    @pl.when((i == 0) | (e != prev))
    def _():
        wg_s[...] = wg_ref[0, 0].astype(BF16)
        wu_s[...] = wu_ref[0, 0].astype(BF16)
        wd_s[...] = wd_ref[0, 0].astype(BF16)

    @pl.when(i < nu_ref[0])
    def _():
        x = x_ref[...]
        a = jnp.dot(x, wg_s[...], preferred_element_type=F32)
        b = jnp.dot(x, wu_s[...], preferred_element_type=F32)
        hid = (_silu(a) * b).astype(BF16)
        o_ref[...] = jnp.dot(hid, wd_s[...], preferred_element_type=F32).astype(o_ref.dtype)

    @pl.when(i >= nu_ref[0])
    def _():
        o_ref[...] = jnp.zeros_like(o_ref)


def _moe_experts(xs, blk_expert, n_used, w_gate, w_up, w_down, layer, n_blocks):
    d = xs.shape[1]
    hid = w_gate.shape[3]
    grid_spec = pltpu.PrefetchScalarGridSpec(
        num_scalar_prefetch=2,
        grid=(n_blocks,),
        in_specs=[pl.BlockSpec((MOE_BLK, d), lambda i, be, nu: (i, 0)),
                  pl.BlockSpec((1, 1, d, hid), lambda i, be, nu: (layer, be[i], 0, 0)),
                  pl.BlockSpec((1, 1, d, hid), lambda i, be, nu: (layer, be[i], 0, 0)),
                  pl.BlockSpec((1, 1, hid, d), lambda i, be, nu: (layer, be[i], 0, 0))],
        out_specs=pl.BlockSpec((MOE_BLK, d), lambda i, be, nu: (i, 0)),
        scratch_shapes=[pltpu.VMEM((d, hid), BF16), pltpu.VMEM((d, hid), BF16), pltpu.VMEM((hid, d), BF16)],
    )
    return pl.pallas_call(
        _moe_kernel,
        grid_spec=grid_spec,
        out_shape=jax.ShapeDtypeStruct((n_blocks * MOE_BLK, d), BF16),
        compiler_params=_cparams("arbitrary"),
        name="moe_experts",
    )(blk_expert, n_used, xs, w_gate, w_up, w_down)


def _moe(h2, logits, w_gate, w_up, w_down, layer):
    n_tok, d = h2.shape
    routed, fields, counts = _route(logits, 512)
    dest, row_tok, blk_expert, n_used, n_blocks = _dispatch(fields, counts, n_tok)
    rows = lambda a, idx: a.at[idx].get(mode='promise_in_bounds')
    xs = rows(h2, row_tok)
    ys = _moe_experts(xs, blk_expert, n_used, w_gate, w_up, w_down, layer, n_blocks)
    return rows(ys, dest[0]), rows(ys, dest[1]), routed


def _rw_kernel(p_ref, mu_ref, pv_ref, wwa_ref, g2_ref, bd_ref, tri_ref, o_ref, st_ref, last_ref, *, tt):
    i = pl.program_id(1)
    L = RW_L
    W = RW_WIDTH

    @pl.when(i == 0)
    def _():
        st_ref[...] = jnp.zeros_like(st_ref)
        last_ref[...] = jnp.zeros_like(last_ref)

    p = p_ref[...].astype(F32)
    row = lax.broadcasted_iota(jnp.int32, p.shape, 0)
    prev = jnp.where(row == 0, jnp.broadcast_to(last_ref[0:1, :], p.shape), pltpu.roll(p, 1, axis=0))
    last_ref[0:1, :] = p[tt - 1:tt, :]
    p = p + (prev - p) * mu_ref[...]
    r = p[:, 0:W]
    k = p[:, W:2 * W]
    v = p[:, 2 * W:3 * W]
    lw = p[:, 3 * W:3 * W + LANES]
    xg = p[:, 3 * W + LANES:3 * W + 2 * LANES]
    w0, a0, k_k, k_a, r_k, ln_w, ln_b = (pv_ref[j:j + 1, :] for j in range(7))
    lane = lax.broadcasted_iota(jnp.int32, lw.shape, 1)
    wa = _bdot(jnp.where(lane < 64, jnp.tanh(lw), lw), wwa_ref[...])
    w = -_softplus(-(w0 + wa[:, :W])) - 0.5
    logd = -jnp.exp(w)
    lr = _sigmoid(a0 + wa[:, W:])
    gate = _bdot(_sigmoid(xg), g2_ref[...])
    bd = bd_ref[...]
    kk = k * k_k
    kk = kk / jnp.maximum(jnp.sqrt(_bdot(kk * kk, bd)), 1e-12)
    k = k * (1.0 + (lr - 1.0) * k_a)
    av = -kk
    bv = kk * lr

    tri = tri_ref[...]
    lane_p = lax.broadcasted_iota(jnp.int32, (L, LANES), 1)
    m0 = lane_p < RW_HEAD
    ri = lax.broadcasted_iota(jnp.int32, (2 * L, 2 * L), 0)
    ci = lax.broadcasted_iota(jnp.int32, (2 * L, 2 * L), 1)
    same = (ri // L) == (ci // L)
    mask_s = same & (ci < ri)
    mask_i = same & (ci <= ri)
    mask_s2 = jnp.concatenate([mask_s, mask_s], axis=1)
    mask_i2 = jnp.concatenate([mask_i, mask_i], axis=1)
    eye = (ri == ci).astype(F32)

    def stack(x):
        return jnp.concatenate([jnp.where(m0, x, 0.0), jnp.where(m0, 0.0, x)], axis=0)

    n_chunks = tt // L
    n_pairs = RW_HEADS // 2
    aa, rr, vv, bk2, bkg, gam = [], [], [], [], [], []
    for c in range(n_chunks):
        rows = slice(c * L, (c + 1) * L)
        ld = logd[rows]
        cs = _dot_exact_lhs(tri, ld)
        total = cs[L - 1:L, :]
        e_out = jnp.exp(-cs)
        e_end = jnp.exp(total - cs)
        at = av[rows] * jnp.exp(cs - ld)
        rt = r[rows] * jnp.exp(cs)
        bt = bv[rows] * e_out
        kt = k[rows] * e_out
        bg = bv[rows] * e_end
        kg = k[rows] * e_end
        g_all = jnp.exp(total)
        for hp in range(n_pairs):
            ls = slice(hp * LANES, (hp + 1) * LANES)
            aa.append(stack(at[:, ls]).astype(BF16))
            rr.append(stack(rt[:, ls]))
            vv.append(stack(v[rows, ls]).astype(BF16))
            bk2.append(jnp.concatenate([bt[:, ls], bt[:, ls], kt[:, ls], kt[:, ls]], axis=0).astype(BF16))
            bkg.append(jnp.concatenate([stack(bg[:, ls]), stack(kg[:, ls])], axis=0).astype(BF16))
            gam.append(g_all[:, ls])
    n_sys = len(aa)
    rng = range(n_sys)
    n_a = [jnp.where(mask_s2, _bdot_nt(aa[i], bk2[i]), 0.0) for i in rng]
    n_r = [jnp.where(mask_i2, _bdot_nt(rr[i], bk2[i]), 0.0).astype(BF16) for i in rng]
    n_ab = [n_a[i][:, :2 * L] for i in rng]
    n_ak = [n_a[i][:, 2 * L:].astype(BF16) for i in rng]
    n_rb = [n_r[i][:, :2 * L] for i in rng]
    tinv = [eye + n_ab[i] for i in rng]
    pw = n_ab
    for _ in range(int(math.log2(L)) - 1):
        pw = [_bdot(pw[i], pw[i]) for i in rng]
        tinv = [tinv[i] + _bdot(tinv[i], pw[i]) for i in rng]
    tinv = [tinv[i].astype(BF16) for i in rng]
    tav = [_bdot(tinv[i], _bdot(n_ak[i], vv[i])).astype(BF16) for i in rng]
    uv = [jnp.concatenate([tav[i], vv[i]], axis=0) for i in rng]
    g_op = [_bdot_tn(tinv[i], bkg[i][:2 * L]) for i in rng]
    mg = [_bdot_tn(aa[i], g_op[i]).astype(BF16) for i in rng]
    w1 = [_bdot(n_rb[i], tinv[i]) for i in rng]
    ra = [(rr[i] + _bdot(w1[i], aa[i])).astype(BF16) for i in rng]
    y0 = [_bdot(n_r[i], uv[i]) for i in rng]
    s0 = [_bdot_tn(uv[i], bkg[i]) for i in rng]
    for c in range(n_chunks):
        rows = slice(c * L, (c + 1) * L)
        y_parts = []
        for hp in range(n_pairs):
            i = c * n_pairs + hp
            s = st_ref[hp]
            sb = s.astype(BF16)
            yst = _bdot_nt(ra[i], sb) + y0[i]
            y_parts.append(yst[:L] + yst[L:])
            st_ref[hp] = s * gam[i] + jnp.dot(sb, mg[i], preferred_element_type=F32) + s0[i]
        y = jnp.concatenate(y_parts, axis=1)
        mean = _bdot(y, bd) * (1.0 / RW_HEAD)
        yc = y - mean
        var = _bdot(yc * yc, bd) * (1.0 / RW_HEAD)
        yn = yc * lax.rsqrt(var + RW_LN_EPS) * ln_w + ln_b
        bonus = _bdot(r[rows] * k[rows] * r_k, bd) * v[rows]
        o_ref[rows, :] = ((yn + bonus) * gate[rows]).astype(o_ref.dtype)


def _rwkv7(p, mu, pv, wwa, g2, bsz, seq):
    tt = RW_TT
    nt = seq // tt
    L = RW_L
    hid = jnp.arange(RW_WIDTH) // RW_HEAD
    bd = (hid[:, None] == hid[None, :]).astype(BF16)
    tri = (jnp.arange(L)[:, None] >= jnp.arange(L)[None, :]).astype(BF16)
    fix = lambda b, i: (0, 0)
    return pl.pallas_call(
        functools.partial(_rw_kernel, tt=tt),
        grid=(bsz, nt),
        in_specs=[pl.BlockSpec((tt, RW_IN), lambda b, i: (b * nt + i, 0)),
                  pl.BlockSpec(mu.shape, fix), pl.BlockSpec(pv.shape, fix), pl.BlockSpec(wwa.shape, fix),
                  pl.BlockSpec(g2.shape, fix), pl.BlockSpec(bd.shape, fix), pl.BlockSpec(tri.shape, fix)],
        out_specs=pl.BlockSpec((tt, RW_WIDTH), lambda b, i: (b * nt + i, 0)),
        out_shape=jax.ShapeDtypeStruct((bsz * seq, RW_WIDTH), BF16),
        scratch_shapes=[pltpu.VMEM((RW_HEADS // 2, 2 * RW_HEAD, LANES), F32), pltpu.VMEM((8, RW_IN), F32)],
        compiler_params=_cparams("arbitrary", "arbitrary"),
        name="rwkv7",
    )(p, mu, pv, wwa, g2, bd, tri)


def _ssd_kernel(p_ref, cw_ref, cb_ref, hv_ref, dl_ref, nw_ref, tri_ref, o_ref, st_ref, tail_ref, *, L):
    i = pl.program_id(1)

    @pl.when(i == 0)
    def _():
        st_ref[...] = jnp.zeros_like(st_ref)
        tail_ref[...] = jnp.zeros_like(tail_ref)

    z = p_ref[:, 0:M2_INNER].astype(F32)
    xin = p_ref[:, M2_INNER:M2_INNER + M2_CONV_DIM].astype(F32)
    dt_raw = p_ref[:, M2_INNER + M2_CONV_DIM:M2_INNER + M2_CONV_DIM + LANES].astype(F32)
    tail = tail_ref[...]
    tail_ref[...] = xin[L - 8:L, :]
    row8 = lax.broadcasted_iota(jnp.int32, (8, M2_CONV_DIM), 0)
    conv = xin * cw_ref[M2_CONV - 1:M2_CONV, :] + cb_ref[...]
    for j in range(1, M2_CONV):
        rolled = pltpu.roll(xin, j, axis=0)
        head = jnp.where(row8 < j, pltpu.roll(tail, j, axis=0), rolled[0:8])
        shifted = jnp.concatenate([head, rolled[8:]], axis=0)
        conv = conv + shifted * cw_ref[M2_CONV - 1 - j:M2_CONV - j, :]
    xbc = _silu(conv)
    xs = xbc[:, 0:M2_INNER]
    dt = _softplus(dt_raw + hv_ref[0:1, :])
    adt = dt * hv_ref[1:2, :]
    cs = _dot_exact_lhs(tri_ref[...], adt)
    cs_t = jnp.transpose(cs)
    total = cs[L - 1:L, :]
    dec_in = jnp.exp(cs)
    dec_out = jnp.exp(total - cs)
    dec_all = jnp.exp(total)
    lane = lax.broadcasted_iota(jnp.int32, (L, LANES), 1)
    first = lane < 64
    rowp = lax.broadcasted_iota(jnp.int32, (LANES, M2_STATE), 0) < 64
    li = lax.broadcasted_iota(jnp.int32, (L, L), 0)
    si = lax.broadcasted_iota(jnp.int32, (L, L), 1)
    causal = li >= si
    y_parts = []
    for hp in range(M2_HEADS // 2):
        g = hp // 2
        h0, h1 = 2 * hp, 2 * hp + 1
        bm = xbc[:, M2_INNER + g * M2_STATE:M2_INNER + (g + 1) * M2_STATE]
        cm = xbc[:, M2_INNER + 2 * M2_STATE + g * M2_STATE:M2_INNER + 2 * M2_STATE + (g + 1) * M2_STATE]
        x_p = xs[:, hp * LANES:(hp + 1) * LANES]
        sel = lambda a: jnp.where(first, a[:, h0:h0 + 1], a[:, h1:h1 + 1])
        xd = x_p * sel(dt)
        cb = _bdot_nt(cm, bm)
        yd = []
        for h in (h0, h1):
            lmat = jnp.where(causal, jnp.exp(cs[:, h:h + 1] - cs_t[h:h + 1, :]), 0.0)
            yd.append(_bdot(cb * lmat, xd))
        s = st_ref[hp]
        y_off = _bdot_nt(cm, s) * sel(dec_in)
        y_parts.append(jnp.where(first, yd[0], yd[1]) + y_off)
        s_new = _bdot_tn(xd * sel(dec_out), bm)
        st_ref[hp] = s * jnp.where(rowp, dec_all[:, h0:h0 + 1], dec_all[:, h1:h1 + 1]) + s_new
    y = jnp.concatenate(y_parts, axis=1) + dl_ref[...] * xs
    y = y * _silu(z)
    half = M2_INNER // 2
    outs = []
    for g in range(2):
        yg = y[:, g * half:(g + 1) * half]
        outs.append(yg * lax.rsqrt(jnp.mean(yg * yg, axis=-1, keepdims=True) + RMS_EPS))
    o_ref[...] = (jnp.concatenate(outs, axis=1) * nw_ref[...]).astype(o_ref.dtype)


def _ssd(p, conv_w, conv_b, hv, d_lanes, norm_w, bsz, seq):
    L = M2_L
    nt = seq // L
    tri = (jnp.arange(L)[:, None] >= jnp.arange(L)[None, :]).astype(BF16)
    fix = lambda b, i: (0, 0)
    return pl.pallas_call(
        functools.partial(_ssd_kernel, L=L),
        grid=(bsz, nt),
        in_specs=[pl.BlockSpec((L, M2_IN_PAD), lambda b, i: (b * nt + i, 0)),
                  pl.BlockSpec(conv_w.shape, fix), pl.BlockSpec(conv_b.shape, fix), pl.BlockSpec(hv.shape, fix),
                  pl.BlockSpec(d_lanes.shape, fix), pl.BlockSpec(norm_w.shape, fix), pl.BlockSpec(tri.shape, fix)],
        out_specs=pl.BlockSpec((L, M2_INNER), lambda b, i: (b * nt + i, 0)),
        out_shape=jax.ShapeDtypeStruct((bsz * seq, M2_INNER), BF16),
        scratch_shapes=[pltpu.VMEM((M2_HEADS // 2, LANES, M2_STATE), F32), pltpu.VMEM((8, M2_CONV_DIM), F32)],
        compiler_params=_cparams("arbitrary", "arbitrary"),
        name="ssd",
    )(p, conv_w, conv_b, hv, d_lanes, norm_w, tri)


def _final_kernel(x_ref, ya_ref, yb_ref, g_ref, gate_ref, nw_ref, o_ref):
    g = g_ref[...]
    moe = g[:, 4:5] * ya_ref[...].astype(F32) + g[:, 5:6] * yb_ref[...].astype(F32)
    x = x_ref[...] + gate_ref[0] * moe
    ms = jnp.mean(x * x, axis=-1, keepdims=True)
    o_ref[...] = x * lax.rsqrt(ms + RMS_EPS) * nw_ref[...]


def _final(x, ya, yb, g, gate, nw, seq, tm):
    n_tok, d = x.shape
    tpb = seq // tm
    row = lambda i: (i, 0)
    return pl.pallas_call(
        _final_kernel,
        grid=(n_tok // tm,),
        in_specs=[pl.BlockSpec((tm, d), row), pl.BlockSpec((tm, d), row), pl.BlockSpec((tm, d), row),
                  pl.BlockSpec((tm, g.shape[1]), row), pl.BlockSpec((1, 1, d), lambda i: (i // tpb, 0, 0)),
                  pl.BlockSpec((1, d), lambda i: (0, 0))],
        out_specs=pl.BlockSpec((tm, d), row),
        out_shape=jax.ShapeDtypeStruct((n_tok, d), F32),
        compiler_params=_cparams("arbitrary"),
        name="final_norm",
    )(x, ya, yb, g, gate, nw)


def _router_weights(w_grp, b_grp, w_exp, b_exp):
    d = w_grp.shape[0]
    pad = LANES - MOE_GROUPS - MOE_EXPERTS
    w = jnp.concatenate([w_grp, w_exp, jnp.zeros((d, pad), F32)], axis=1)
    b = jnp.concatenate([b_grp, b_exp, jnp.zeros((pad,), F32)])[None, :]
    return w, b


def kernel(x, c, ada_w, ada_b, norm_mix_w, norm_ffn_w, even_w_in, even_w_out, s5_a_re, s5_a_im, s5_log_dt, s5_b_re, s5_b_im, s5_c_re, s5_c_im, s5_d, s5_w_glu, odd_w_in, odd_w_out, rw_mu, rw_w0, rw_w2, rw_a0, rw_a2, rw_g2, rw_k_k, rw_k_a, rw_r_k, rw_ln_w, rw_ln_b, m2_conv_w, m2_conv_b, m2_dt_bias, m2_a_log, m2_d, m2_norm_w, moe_w_grp, moe_b_grp, moe_w_exp, moe_b_exp, moe_w_gate, moe_w_up, moe_w_down, final_norm_w):
    bsz, seq, d = x.shape
    n_tok = bsz * seq
    tm = 512
    xt = x.reshape(n_tok, d)
    ada = _ada_params(c, ada_w, ada_b)
    mods = [[ada[i, :, j * d:(j + 1) * d].reshape(bsz, 1, d) for j in range(6)] for i in range(2)]

    sh1, sc1, g1, sh2, sc2, g2 = mods[0]
    u, q, k, v = _in_proj(xt, norm_mix_w[0][None, :], sh1, sc1, even_w_in[0].astype(BF16),
                          ((0, 512), (512, 1024), (1024, 1536), (1536, 2048)), (F32, BF16, BF16, BF16), seq, tm)
    tables = _s5_tables(s5_a_re[0], s5_a_im[0], s5_log_dt[0], s5_b_re[0], s5_b_im[0], s5_c_re[0], s5_c_im[0], s5_d[0])
    y_s5 = _s5_scan(u, tables, bsz, seq)
    y_sb = _sb_attention(q, k, v, bsz, seq)
    wr, br = _router_weights(moe_w_grp[0], moe_b_grp[0], moe_w_exp[0], moe_b_exp[0])
    x1, h2, logits = _mix_out(y_s5, y_sb, xt, g1, s5_w_glu[0].astype(BF16), even_w_out[0].astype(BF16),
                              norm_ffn_w[0][None, :], sh2, sc2, wr, br, seq, tm)
    ya, yb, gates = _moe(h2, logits, moe_w_gate, moe_w_up, moe_w_down, 0)

    gate_prev = g2
    sh1, sc1, g1, sh2, sc2, g2 = mods[1]
    w_in = jnp.concatenate([odd_w_in[0], jnp.zeros((d, M2_IN_PAD - M2_IN), F32)], axis=1).astype(BF16)
    x2, p_rw, p_m2 = _in_proj(x1, norm_mix_w[1][None, :], sh1, sc1, w_in,
                              ((0, RW_IN), (RW_IN, RW_IN + M2_IN_PAD)), (BF16, BF16), seq, tm,
                              add=(ya, yb, gates, gate_prev))
    pv = jnp.stack([rw_w0[0], rw_a0[0], rw_k_k[0], rw_k_a[0], rw_r_k[0].reshape(-1), rw_ln_w[0], rw_ln_b[0],
                    jnp.zeros((RW_WIDTH,), F32)])
    zl = jnp.zeros((64, RW_WIDTH), F32)
    wwa = jnp.concatenate([jnp.concatenate([rw_w2[0], zl], axis=1),
                           jnp.concatenate([zl, rw_a2[0]], axis=1)], axis=0).astype(BF16)
    y_rw = _rwkv7(p_rw, rw_mu[0][None, :], pv, wwa, rw_g2[0].astype(BF16), bsz, seq)
    hv = jnp.zeros((8, LANES), F32)
    hv = hv.at[0, :M2_HEADS].set(m2_dt_bias[0]).at[1, :M2_HEADS].set(-jnp.exp(m2_a_log[0]))
    y_m2 = _ssd(p_m2, m2_conv_w[0], m2_conv_b[0][None, :], hv, jnp.repeat(m2_d[0], 64)[None, :],
                m2_norm_w[0][None, :], bsz, seq)
    wr, br = _router_weights(moe_w_grp[1], moe_b_grp[1], moe_w_exp[1], moe_b_exp[1])
    x3, h2, logits = _mix_out(y_rw, y_m2, x2, g1, None, odd_w_out[0].astype(BF16),
                              norm_ffn_w[1][None, :], sh2, sc2, wr, br, seq, tm)
    ya, yb, gates = _moe(h2, logits, moe_w_gate, moe_w_up, moe_w_down, 1)
    out = _final(x3, ya, yb, gates, g2, final_norm_w[None, :], seq, tm)
    return out.reshape(bsz, seq, d)
```

```python
import functools
import math

import jax
import jax.numpy as jnp
from jax import lax
from jax.experimental import pallas as pl
from jax.experimental.pallas import tpu as pltpu

F32 = jnp.float32
BF16 = jnp.bfloat16

D_MODEL = 1024
RMS_EPS = 1e-6
LANES = 128
VMEM_LIMIT = 56 * 1024 * 1024

S5_WIDTH = 512
S5_P = 16
S5_G = 32
S5_N = 64
S5_L = 16
S5_GB = 8
SB_WIDTH = 512
SB_HEAD = 64
SB_T = 256
SB_FADE = 160.0

RW_WIDTH = 512
RW_HEAD = 64
RW_HEADS = 8
RW_IN = 1792
RW_LN_EPS = 64e-5
RW_L = 64
RW_TT = 256

M2_INNER = 512
M2_HEADS = 8
M2_STATE = 128
M2_CONV = 4
M2_CONV_DIM = 1024
M2_IN = 1544
M2_IN_PAD = 1664
M2_L = 256

MOE_GROUPS = 4
MOE_PER_GROUP = 8
MOE_EXPERTS = 32
MOE_TOPK = 2
MOE_HIDDEN = 512
MOE_BLK = 512


def _cparams(*sem):
    return pltpu.CompilerParams(dimension_semantics=sem, vmem_limit_bytes=VMEM_LIMIT)


def _bdot(a, b):
    return jnp.dot(a.astype(BF16), b.astype(BF16), preferred_element_type=F32)


def _bdot_nt(a, b):
    return lax.dot_general(a.astype(BF16), b.astype(BF16), (((1,), (1,)), ((), ())),
                           preferred_element_type=F32)


def _bdot_tn(a, b):
    return lax.dot_general(a.astype(BF16), b.astype(BF16), (((0,), (0,)), ((), ())),
                           preferred_element_type=F32)


def _split2(a):
    hi = a.astype(BF16)
    lo = (a - hi.astype(F32)).astype(BF16)
    return hi, lo


def _dot3(a, b):
    ah, al = _split2(a)
    bh, bl = _split2(b)
    d = functools.partial(jnp.dot, preferred_element_type=F32)
    return d(ah, bh) + d(ah, bl) + d(al, bh)


def _dot_exact_rhs(a, b01):
    ah, al = _split2(a)
    d = functools.partial(jnp.dot, preferred_element_type=F32)
    return d(ah, b01) + d(al, b01)


def _dot_exact_lhs(a01, b):
    bh, bl = _split2(b)
    d = functools.partial(jnp.dot, preferred_element_type=F32)
    return d(a01, bh) + d(a01, bl)


def _sigmoid(x):
    return 1.0 / (1.0 + jnp.exp(-x))


def _softplus(x):
    return jnp.maximum(x, 0.0) + jnp.log(1.0 + jnp.exp(-jnp.abs(x)))


def _silu(x):
    return x * _sigmoid(x)


def _gelu_tanh(x):
    c = math.sqrt(2.0 / math.pi)
    return 0.5 * x * (1.0 + jnp.tanh(c * (x + 0.044715 * (x * x * x))))


def _rms_mod(x, nw, shift, scale):
    ms = jnp.mean(x * x, axis=-1, keepdims=True)
    return (x * lax.rsqrt(ms + RMS_EPS)) * nw * (1.0 + scale) + shift


def _ada_kernel(c_ref, w_ref, b_ref, o_ref):
    cond = _silu(c_ref[...])
    o_ref[0] = _dot3(cond, w_ref[0]) + b_ref[0]


def _ada_params(c, ada_w, ada_b):
    depth, d, n = ada_w.shape
    bsz = c.shape[0]
    tn = 1024
    return pl.pallas_call(
        _ada_kernel,
        grid=(depth, n // tn),
        in_specs=[pl.BlockSpec((bsz, d), lambda i, j: (0, 0)),
                  pl.BlockSpec((1, d, tn), lambda i, j: (i, 0, j)),
                  pl.BlockSpec((1, 1, tn), lambda i, j: (i, 0, j))],
        out_specs=pl.BlockSpec((1, bsz, tn), lambda i, j: (i, 0, j)),
        out_shape=jax.ShapeDtypeStruct((depth, bsz, n), F32),
        compiler_params=_cparams("arbitrary", "arbitrary"),
        name="ada_params",
    )(c, ada_w, ada_b.reshape(depth, 1, n))


def _in_proj_kernel(*refs, has_add, splits):
    if has_add:
        x_ref, ya_ref, yb_ref, g_ref, gate_ref, nw_ref, sh_ref, sc_ref, w_ref = refs[:9]
        outs = refs[9:]
        x = x_ref[...]
        g = g_ref[...]
        moe = g[:, 4:5] * ya_ref[...].astype(F32) + g[:, 5:6] * yb_ref[...].astype(F32)
        x = x + gate_ref[0] * moe
        outs[0][...] = x
        outs = outs[1:]
    else:
        x_ref, nw_ref, sh_ref, sc_ref, w_ref = refs[:5]
        outs = refs[5:]
        x = x_ref[...]
    h = _rms_mod(x, nw_ref[...], sh_ref[0], sc_ref[0]).astype(BF16)
    for o_ref, (c0, c1) in zip(outs, splits):
        o_ref[...] = jnp.dot(h, w_ref[:, c0:c1], preferred_element_type=F32).astype(o_ref.dtype)


def _in_proj(x, nw, shift, scale, w, splits, dtypes, seq, tm, add=None):
    n_tok, d = x.shape
    tpb = seq // tm
    row = lambda i: (i, 0)
    bat = lambda i: (i // tpb, 0, 0)
    fix = lambda i: (0, 0)
    in_specs = [pl.BlockSpec((tm, d), row)]
    args = [x]
    out_shape = []
    out_specs = []
    if add is not None:
        ya, yb, g, gate = add
        in_specs += [pl.BlockSpec((tm, d), row), pl.BlockSpec((tm, d), row),
                     pl.BlockSpec((tm, g.shape[1]), row), pl.BlockSpec((1, 1, d), bat)]
        args += [ya, yb, g, gate]
        out_shape.append(jax.ShapeDtypeStruct((n_tok, d), F32))
        out_specs.append(pl.BlockSpec((tm, d), row))
    in_specs += [pl.BlockSpec((1, d), fix), pl.BlockSpec((1, 1, d), bat), pl.BlockSpec((1, 1, d), bat),
                 pl.BlockSpec(w.shape, fix)]
    args += [nw, shift, scale, w]
    for (c0, c1), dt in zip(splits, dtypes):
        out_shape.append(jax.ShapeDtypeStruct((n_tok, c1 - c0), dt))
        out_specs.append(pl.BlockSpec((tm, c1 - c0), row))
    return pl.pallas_call(
        functools.partial(_in_proj_kernel, has_add=add is not None, splits=splits),
        grid=(n_tok // tm,),
        in_specs=in_specs, out_specs=out_specs, out_shape=out_shape,
        compiler_params=_cparams("arbitrary"),
        name="in_proj",
    )(*args)


def _s5_tables(a_re, a_im, log_dt, b_re, b_im, c_re, c_im, d_skip):
    hp = lax.Precision.HIGHEST
    L = S5_L
    a_re = jnp.minimum(a_re.astype(F32), -1e-4)
    a_im = a_im.astype(F32)
    dt = jnp.exp(log_dt.astype(F32))[:, None]
    mag = jnp.exp(dt * a_re)
    abar_re, abar_im = mag * jnp.cos(dt * a_im), mag * jnp.sin(dt * a_im)
    den = a_re * a_re + a_im * a_im
    num_re, num_im = abar_re - 1.0, abar_im
    coef_re = (num_re * a_re + num_im * a_im) / den
    coef_im = (num_im * a_re - num_re * a_im) / den
    b_re = b_re.astype(F32)
    b_im = b_im.astype(F32)
    bb_re = coef_re[..., None] * b_re - coef_im[..., None] * b_im
    bb_im = coef_re[..., None] * b_im + coef_im[..., None] * b_re
    c_re = c_re.astype(F32)
    c_im = c_im.astype(F32)
    tau = jnp.arange(L + 1, dtype=F32)[None, :, None]
    pmag = jnp.exp(tau * (dt * a_re)[:, None, :])
    pw_re = pmag * jnp.cos(tau * (dt * a_im)[:, None, :])
    pw_im = pmag * jnp.sin(tau * (dt * a_im)[:, None, :])
    cl_re = c_re[:, None] * pw_re[:, :, None, :] - c_im[:, None] * pw_im[:, :, None, :]
    cl_im = c_re[:, None] * pw_im[:, :, None, :] + c_im[:, None] * pw_re[:, :, None, :]
    taps = (jnp.einsum('gtpn,gnq->gtpq', cl_re[:, :L], bb_re, precision=hp)
            - jnp.einsum('gtpn,gnq->gtpq', cl_im[:, :L], bb_im, precision=hp))
    nb = S5_G // S5_GB

    def compact(x):
        return x.reshape(nb, S5_GB, L, S5_P, 2 * S5_N).transpose(0, 2, 1, 3, 4).reshape(nb, L, LANES, 2 * S5_N)

    eye = jnp.eye(S5_GB, dtype=F32)
    dlag = jnp.einsum('bgtpq,gh->btgqhp', taps.reshape(nb, S5_GB, L, S5_P, S5_P), eye)
    dlag = dlag.reshape(nb, L, LANES, LANES)
    rev = L - 1 - jnp.arange(L)
    e_re = pw_re[:, rev][:, :, :, None] * bb_re[:, None] - pw_im[:, rev][:, :, :, None] * bb_im[:, None]
    e_im = pw_re[:, rev][:, :, :, None] * bb_im[:, None] + pw_im[:, rev][:, :, :, None] * bb_re[:, None]
    e_re = e_re.transpose(0, 1, 3, 2)
    e_im = e_im.transpose(0, 1, 3, 2)
    emap = jnp.stack([compact(jnp.concatenate([e_re, e_im], axis=-1)),
                      compact(jnp.concatenate([e_im, e_re], axis=-1))], axis=1)
    q_c = jnp.stack([cl_re[:, 1:L + 1], -cl_im[:, 1:L + 1]], axis=1).reshape(nb, S5_GB, 2, L, S5_P, S5_N)
    qmap = jnp.einsum('bgctpn,gh->bcgnthp', q_c, eye).reshape(nb, 2 * S5_GB * S5_N, L * LANES)
    lr = pw_re[:, L].reshape(nb, 1, S5_GB * S5_N)
    li = pw_im[:, L].reshape(nb, 1, S5_GB * S5_N)
    lam_a = jnp.concatenate([lr, lr], axis=-1)
    lam_b = jnp.concatenate([-li, li], axis=-1)
    dvec = jnp.tile(d_skip.astype(F32).reshape(nb, 1, LANES), (1, 1, L))
    return dlag.astype(BF16), emap.astype(BF16), qmap.astype(BF16), lam_a, lam_b, dvec


def _s5_kernel(u_ref, dlag_ref, emap_ref, qmap_ref, la_ref, lb_ref, dv_ref, o_ref, toep_ref, esp_ref, h_ref, *, n_rows):
    L = S5_L
    ns = 2 * S5_GB * S5_N

    @pl.when(pl.program_id(1) == 0)
    def _():
        toep_ref[...] = jnp.zeros_like(toep_ref)
        esp_ref[...] = jnp.zeros_like(esp_ref)
        for s in range(L):
            for t in range(s, L):
                toep_ref[s * LANES:(s + 1) * LANES, t * LANES:(t + 1) * LANES] = dlag_ref[0, t - s]
            for g in range(S5_GB):
                rows = slice(s * LANES + g * S5_P, s * LANES + (g + 1) * S5_P)
                src = slice(g * S5_P, (g + 1) * S5_P)
                half = slice((g % 2) * S5_N, (g % 2 + 1) * S5_N)
                re_at = g * S5_N
                im_at = ns // 2 + g * S5_N
                esp_ref[rows, re_at:re_at + S5_N] = emap_ref[0, g % 2, s, src, half]
                esp_ref[rows, im_at:im_at + S5_N] = emap_ref[0, 1 - g % 2, s, src, half]

    up = jnp.concatenate([u_ref[pl.ds(s, n_rows, stride=L), :] for s in range(L)], axis=1)
    ub = up.astype(BF16)
    h_ref[...] = jnp.dot(ub, esp_ref[...], preferred_element_type=F32)
    la = la_ref[0]
    lb = lb_ref[0]

    def step(c8, h):
        rows = pl.ds(pl.multiple_of(c8 * 8, 8), 8)
        e = h_ref[rows, :]
        ent = []
        for j in range(8):
            ent.append(h)
            h = h * la + pltpu.roll(h, ns // 2, axis=1) * lb + e[j:j + 1, :]
        h_ref[rows, :] = jnp.concatenate(ent, axis=0)
        return h

    lax.fori_loop(0, n_rows // 8, step, jnp.zeros((1, ns), F32))
    hi, lo = _split2(h_ref[...])
    qm = qmap_ref[0]
    y_off = jnp.dot(hi, qm, preferred_element_type=F32) + jnp.dot(lo, qm, preferred_element_type=F32)
    wide = 2 * LANES
    for j in range(L * LANES // wide):
        k_end = (j + 1) * wide
        cols = slice(j * wide, k_end)
        y = jnp.dot(ub[:, :k_end], toep_ref[:k_end, cols], preferred_element_type=F32)
        y = y + y_off[:, cols] + up[:, cols] * dv_ref[0, :, cols]
        for t in range(2 * j, 2 * j + 2):
            o_ref[pl.ds(t, n_rows, stride=L), :] = y[:, (t - 2 * j) * LANES:(t - 2 * j + 1) * LANES]


def _s5_scan(u, tables, bsz, seq):
    dlag, emap, qmap, lam_a, lam_b, dvec = tables
    L = S5_L
    n_rows = seq // L
    nb = S5_WIDTH // LANES
    ns = 2 * S5_GB * S5_N
    blk3 = lambda j, b: (j, 0, 0)
    blk4 = lambda j, b: (j, 0, 0, 0)
    return pl.pallas_call(
        functools.partial(_s5_kernel, n_rows=n_rows),
        grid=(nb, bsz),
        in_specs=[pl.BlockSpec((seq, LANES), lambda j, b: (b, j)),
                  pl.BlockSpec((1, L, LANES, LANES), blk4),
                  pl.BlockSpec((1, 2, L, LANES, 2 * S5_N), lambda j, b: (j, 0, 0, 0, 0)),
                  pl.BlockSpec((1, ns, L * LANES), blk3),
                  pl.BlockSpec((1, 1, ns), blk3), pl.BlockSpec((1, 1, ns), blk3), pl.BlockSpec((1, 1, L * LANES), blk3)],
        out_specs=pl.BlockSpec((seq, LANES), lambda j, b: (b, j)),
        out_shape=jax.ShapeDtypeStruct((bsz * seq, S5_WIDTH), F32),
        scratch_shapes=[pltpu.VMEM((L * LANES, L * LANES), BF16), pltpu.VMEM((L * LANES, ns), BF16),
                        pltpu.VMEM((n_rows, ns), F32)],
        compiler_params=_cparams("arbitrary", "arbitrary"),
        name="s5_scan",
    )(u, dlag, emap, qmap, lam_a, lam_b, dvec)


def _sb_kernel(q_ref, k_ref, v_ref, tri_ref, o_ref, qh_ref, hl_ref, z_ref, w_ref, acc_ref, car_ref, *, t):
    qb = pl.program_id(1)
    heads = SB_WIDTH // SB_HEAD
    lane = lax.broadcasted_iota(jnp.int32, (t, LANES), 1)
    low = lane < SB_HEAD
    for p in range(heads // 2):
        q = (q_ref[:, p * LANES:(p + 1) * LANES].astype(F32) * (0.125 * math.log2(math.e))).astype(BF16)
        zero = jnp.zeros_like(q)
        qh_ref[2 * p] = jnp.where(low, q, zero)
        qh_ref[2 * p + 1] = jnp.where(low, zero, q)
    acc_ref[...] = jnp.zeros_like(acc_ref)
    car_ref[...] = jnp.zeros_like(car_ref)

    def scores(kb, h, slot, diagonal):
        rows = pl.ds(pl.multiple_of(kb * t, t), t)
        kblk = k_ref[rows, (h // 2) * LANES:(h // 2 + 1) * LANES]
        z = lax.dot_general(qh_ref[h], kblk, (((1,), (1,)), ((), ())), preferred_element_type=F32)
        sp = jnp.maximum(z, 0.0) + jnp.log2(1.0 + jnp.exp2(-jnp.abs(z)))
        if diagonal:
            mask = lax.broadcasted_iota(jnp.int32, (t, t), 0) > lax.broadcasted_iota(jnp.int32, (t, t), 1)
            sp = jnp.where(mask, sp, 0.0)
            z = jnp.where(mask, z, -1e30)
        hi = sp.astype(BF16)
        hl_ref[slot, :, :t] = hi
        hl_ref[slot, :, t:] = (sp - hi.astype(F32)).astype(BF16)
        z_ref[slot] = z

    def weights(h, src, dst):
        cs = jnp.dot(hl_ref[src], tri_ref[...], preferred_element_type=F32)
        car = car_ref[h]
        s = z_ref[src] - cs - jnp.concatenate([car, car], axis=1)
        w_ref[dst] = jnp.exp2(s).astype(BF16)
        car_ref[h] = car + jnp.broadcast_to(cs[:, 0:1], car.shape)

    def values(kb, h, slot):
        rows = pl.ds(pl.multiple_of(kb * t, t), t)
        vblk = v_ref[rows, (h // 2) * LANES:(h // 2 + 1) * LANES]
        acc_ref[h] += jnp.dot(w_ref[slot], vblk, preferred_element_type=F32)

    def key_tile(kb, diagonal):
        for j in range(heads + 2):
            if j < heads:
                scores(kb, j, j % 2, diagonal)
            if 1 <= j <= heads:
                weights(j - 1, (j - 1) % 2, j % 2)
            if j >= 2:
                values(kb, j - 2, (j - 1) % 2)

    def faded():
        return jnp.min(car_ref[...]) >= SB_FADE

    key_tile(qb, True)

    def more(state):
        kb, _ = state
        key_tile(kb, False)
        return kb - 1, faded()

    lax.while_loop(lambda s: (s[0] >= 0) & jnp.logical_not(s[1]), more, (qb - 1, faded()))
    for p in range(heads // 2):
        o_ref[:, p * LANES:(p + 1) * LANES] = jnp.where(low, acc_ref[2 * p], acc_ref[2 * p + 1]).astype(o_ref.dtype)


def _sb_attention(q, k, v, bsz, seq):
    t = SB_T
    nq = seq // t
    heads = SB_WIDTH // SB_HEAD
    tri = (jnp.arange(t)[:, None] >= jnp.arange(t)[None, :]).astype(BF16)
    tri2 = jnp.concatenate([tri, tri], axis=0)
    return pl.pallas_call(
        functools.partial(_sb_kernel, t=t),
        grid=(bsz, nq),
        in_specs=[pl.BlockSpec((t, SB_WIDTH), lambda b, i: (b * nq + i, 0)),
                  pl.BlockSpec((seq, SB_WIDTH), lambda b, i: (b, 0)),
                  pl.BlockSpec((seq, SB_WIDTH), lambda b, i: (b, 0)),
                  pl.BlockSpec((2 * t, t), lambda b, i: (0, 0))],
        out_specs=pl.BlockSpec((t, SB_WIDTH), lambda b, i: (b * nq + i, 0)),
        out_shape=jax.ShapeDtypeStruct((bsz * seq, SB_WIDTH), BF16),
        scratch_shapes=[pltpu.VMEM((heads, t, LANES), BF16), pltpu.VMEM((2, t, 2 * t), BF16),
                        pltpu.VMEM((2, t, t), F32), pltpu.VMEM((2, t, t), BF16),
                        pltpu.VMEM((heads, t, LANES), F32), pltpu.VMEM((heads, t, LANES), F32)],
        compiler_params=_cparams("arbitrary", "arbitrary"),
        name="sb_attention",
    )(q, k, v, tri2)


def _mix_out_kernel(*refs, glu):
    if glu:
        ya_ref, yb_ref, x_ref, gate_ref, wglu_ref, wo_ref, nw_ref, sh_ref, sc_ref, wr_ref, br_ref = refs[:11]
        outs = refs[11:]
        a = _gelu_tanh(ya_ref[...].astype(F32)).astype(BF16)
        g = jnp.dot(a, wglu_ref[...], preferred_element_type=F32)
        half = g.shape[1] // 2
        ya = (g[:, :half] * _sigmoid(g[:, half:])).astype(BF16)
    else:
        ya_ref, yb_ref, x_ref, gate_ref, wo_ref, nw_ref, sh_ref, sc_ref, wr_ref, br_ref = refs[:10]
        outs = refs[10:]
        ya = ya_ref[...]
    x1_ref, h2_ref, lg_ref = outs
    ka = ya.shape[1]
    mix = (jnp.dot(ya, wo_ref[:ka, :], preferred_element_type=F32)
           + jnp.dot(yb_ref[...], wo_ref[ka:, :], preferred_element_type=F32))
    x1 = x_ref[...] + gate_ref[0] * mix
    x1_ref[...] = x1
    h2 = _rms_mod(x1, nw_ref[...], sh_ref[0], sc_ref[0])
    h2_ref[...] = h2.astype(BF16)
    lg_ref[...] = _dot3(h2, wr_ref[...]) + br_ref[...]


def _mix_out(ya, yb, x, gate, w_glu, w_out, nw, shift, scale, w_router, b_router, seq, tm):
    n_tok, d = x.shape
    tpb = seq // tm
    row = lambda i: (i, 0)
    bat = lambda i: (i // tpb, 0, 0)
    fix = lambda i: (0, 0)
    in_specs = [pl.BlockSpec((tm, ya.shape[1]), row), pl.BlockSpec((tm, yb.shape[1]), row),
                pl.BlockSpec((tm, d), row), pl.BlockSpec((1, 1, d), bat)]
    args = [ya, yb, x, gate]
    if w_glu is not None:
        in_specs.append(pl.BlockSpec(w_glu.shape, fix))
        args.append(w_glu)
    in_specs += [pl.BlockSpec(w_out.shape, fix), pl.BlockSpec((1, d), fix), pl.BlockSpec((1, 1, d), bat),
                 pl.BlockSpec((1, 1, d), bat), pl.BlockSpec(w_router.shape, fix), pl.BlockSpec(b_router.shape, fix)]
    args += [w_out, nw, shift, scale, w_router, b_router]
    return pl.pallas_call(
        functools.partial(_mix_out_kernel, glu=w_glu is not None),
        grid=(n_tok // tm,),
        in_specs=in_specs,
        out_specs=[pl.BlockSpec((tm, d), row), pl.BlockSpec((tm, d), row), pl.BlockSpec((tm, LANES), row)],
        out_shape=[jax.ShapeDtypeStruct((n_tok, d), F32), jax.ShapeDtypeStruct((n_tok, d), BF16),
                   jax.ShapeDtypeStruct((n_tok, LANES), F32)],
        compiler_params=_cparams("arbitrary"),
        name="mix_out",
    )(*args)


def _route_kernel(lg_ref, tri_ref, o_ref, ot_ref, cnt_ref, run_ref):
    @pl.when(pl.program_id(0) == 0)
    def _():
        run_ref[...] = jnp.zeros_like(run_ref)

    lg = lg_ref[...]
    lane = lax.broadcasted_iota(jnp.int32, lg.shape, 1)
    neg = jnp.float32(-jnp.inf)
    far = jnp.int32(LANES)

    def top(x):
        m = jnp.max(x, axis=1, keepdims=True)
        return m, jnp.min(jnp.where(x == m, lane, far), axis=1, keepdims=True)

    is_grp = lane < MOE_GROUPS
    gmax, gidx = top(jnp.where(is_grp, lg, neg))
    grp_p = 1.0 / jnp.sum(jnp.where(is_grp, jnp.exp(lg - gmax), 0.0), axis=1, keepdims=True)
    first = MOE_GROUPS + MOE_PER_GROUP * gidx
    el = jnp.where((lane >= first) & (lane < first + MOE_PER_GROUP), lg, neg)
    l0, i0 = top(el)
    l1, i1 = top(jnp.where(lane == i0, neg, el))
    e1 = jnp.exp(l1 - l0)
    g0 = grp_p / (1.0 + e1)
    g1 = grp_p * e1 / (1.0 + e1)
    pick0 = lane == i0
    pick1 = lane == i1
    onehot = jnp.where(pick0 | pick1, 1.0, 0.0)
    before = jnp.dot(tri_ref[...], onehot.astype(BF16), preferred_element_type=F32) + run_ref[...]
    r0 = jnp.sum(jnp.where(pick0, before, 0.0), axis=1, keepdims=True)
    r1 = jnp.sum(jnp.where(pick1, before, 0.0), axis=1, keepdims=True)
    run = run_ref[...] + jnp.sum(onehot, axis=0, keepdims=True)
    run_ref[...] = run
    cnt_ref[...] = run
    cols = ((i0 - MOE_GROUPS).astype(F32), (i1 - MOE_GROUPS).astype(F32), r0, r1, g0, g1)
    out = jnp.zeros(lg.shape, F32)
    for j, col in enumerate(cols):
        out = jnp.where(lane == j, col, out)
    o_ref[...] = out
    ot_ref[...] = jnp.transpose(out)[0:8, :]


def _route(logits, tm):
    n_tok = logits.shape[0]
    tri = (jnp.arange(tm)[:, None] > jnp.arange(tm)[None, :]).astype(BF16)
    return pl.pallas_call(
        _route_kernel,
        grid=(n_tok // tm,),
        in_specs=[pl.BlockSpec((tm, LANES), lambda i: (i, 0)), pl.BlockSpec((tm, tm), lambda i: (0, 0))],
        out_specs=[pl.BlockSpec((tm, LANES), lambda i: (i, 0)), pl.BlockSpec((8, tm), lambda i: (0, i)),
                   pl.BlockSpec((1, LANES), lambda i: (0, 0))],
        out_shape=[jax.ShapeDtypeStruct((n_tok, LANES), F32), jax.ShapeDtypeStruct((8, n_tok), F32),
                   jax.ShapeDtypeStruct((1, LANES), F32)],
        scratch_shapes=[pltpu.VMEM((1, LANES), F32)],
        compiler_params=_cparams("arbitrary"),
        name="route",
    )(logits, tri)


def _dispatch(fields, counts, n_tok):
    n_assign = n_tok * MOE_TOPK
    n_blocks = (n_assign + MOE_EXPERTS * (MOE_BLK - 1) + MOE_BLK - 1) // MOE_BLK
    n_rows = n_blocks * MOE_BLK
    experts = fields[0:2].astype(jnp.int32)
    rank = fields[2:4].astype(jnp.int32)
    counts = counts[0, MOE_GROUPS:MOE_GROUPS + MOE_EXPERTS].astype(jnp.int32)
    padded = ((counts + MOE_BLK - 1) // MOE_BLK) * MOE_BLK
    pad_end = jnp.cumsum(padded)
    pad_start = pad_end - padded
    dest = jnp.take(pad_start, experts, mode='clip') + rank
    blk_start = jnp.arange(n_blocks, dtype=jnp.int32) * MOE_BLK
    blk_expert = jnp.minimum(jnp.sum((pad_end[None, :] <= blk_start[:, None]).astype(jnp.int32), axis=1),
                             MOE_EXPERTS - 1).astype(jnp.int32)
    n_used = (pad_end[-1] // MOE_BLK).astype(jnp.int32).reshape(1)
    n_pad = n_rows - n_assign
    gap_end = jnp.cumsum(padded - counts)
    i = jnp.arange(n_pad, dtype=jnp.int32)
    owner = jnp.sum((gap_end[None, :] <= i[:, None]).astype(jnp.int32), axis=1)
    hot = owner[:, None] == jnp.arange(MOE_EXPERTS + 1, dtype=jnp.int32)[None, :]
    first_gap = jnp.concatenate([pad_start + counts, pad_end[-1:]])
    before = jnp.concatenate([gap_end - (padded - counts), gap_end[-1:]])
    pad_row = i + jnp.sum(jnp.where(hot, (first_gap - before)[None, :], 0), axis=1)
    keys = jnp.concatenate([dest.reshape(-1), pad_row])
    vals = jnp.concatenate([jnp.tile(jnp.arange(n_tok, dtype=jnp.int32), MOE_TOPK), pad_row % n_tok])
    _, row_tok = lax.sort_key_val(keys, vals)
    return dest, row_tok, blk_expert, n_used, n_blocks


def _moe_kernel(be_ref, nu_ref, x_ref, wg_ref, wu_ref, wd_ref, o_ref, wg_s, wu_s, wd_s):
    i = pl.program_id(0)
    e = be_ref[i]
    prev = be_ref[jnp.maximum(i - 1, 0)]

    @pl.when((i == 0) | (e != prev))
    def _():
        wg_s[...] = wg_ref[0, 0].astype(BF16)
        wu_s[...] = wu_ref[0, 0].astype(BF16)
        wd_s[...] = wd_ref[0, 0].astype(BF16)

    @pl.when(i < nu_ref[0])
    def _():
        x = x_ref[...]
        a = jnp.dot(x, wg_s[...], preferred_element_type=F32)
        b = jnp.dot(x, wu_s[...], preferred_element_type=F32)
        hid = (_silu(a) * b).astype(BF16)
        o_ref[...] = jnp.dot(hid, wd_s[...], preferred_element_type=F32).astype(o_ref.dtype)

    @pl.when(i >= nu_ref[0])
    def _():
        o_ref[...] = jnp.zeros_like(o_ref)


def _moe_experts(xs, blk_expert, n_used, w_gate, w_up, w_down, layer, n_blocks):
    d = xs.shape[1]
    hid = w_gate.shape[3]
    grid_spec = pltpu.PrefetchScalarGridSpec(
        num_scalar_prefetch=2,
        grid=(n_blocks,),
        in_specs=[pl.BlockSpec((MOE_BLK, d), lambda i, be, nu: (i, 0)),
                  pl.BlockSpec((1, 1, d, hid), lambda i, be, nu: (layer, be[i], 0, 0)),
                  pl.BlockSpec((1, 1, d, hid), lambda i, be, nu: (layer, be[i], 0, 0)),
                  pl.BlockSpec((1, 1, hid, d), lambda i, be, nu: (layer, be[i], 0, 0))],
        out_specs=pl.BlockSpec((MOE_BLK, d), lambda i, be, nu: (i, 0)),
        scratch_shapes=[pltpu.VMEM((d, hid), BF16), pltpu.VMEM((d, hid), BF16), pltpu.VMEM((hid, d), BF16)],
    )
    return pl.pallas_call(
        _moe_kernel,
        grid_spec=grid_spec,
        out_shape=jax.ShapeDtypeStruct((n_blocks * MOE_BLK, d), BF16),
        compiler_params=_cparams("arbitrary"),
        name="moe_experts",
    )(blk_expert, n_used, xs, w_gate, w_up, w_down)


def _moe(h2, logits, w_gate, w_up, w_down, layer):
    n_tok, d = h2.shape
    routed, fields, counts = _route(logits, 512)
    dest, row_tok, blk_expert, n_used, n_blocks = _dispatch(fields, counts, n_tok)
    rows = lambda a, idx: a.at[idx].get(mode='promise_in_bounds')
    xs = rows(h2, row_tok)
    ys = _moe_experts(xs, blk_expert, n_used, w_gate, w_up, w_down, layer, n_blocks)
    return rows(ys, dest[0]), rows(ys, dest[1]), routed


def _rw_kernel(p_ref, mu_ref, pv_ref, wwa_ref, g2_ref, bd_ref, tri_ref, o_ref, st_ref, last_ref, *, tt):
    i = pl.program_id(1)
    L = RW_L
    W = RW_WIDTH

    @pl.when(i == 0)
    def _():
        st_ref[...] = jnp.zeros_like(st_ref)
        last_ref[...] = jnp.zeros_like(last_ref)

    p = p_ref[...].astype(F32)
    row = lax.broadcasted_iota(jnp.int32, p.shape, 0)
    prev = jnp.where(row == 0, jnp.broadcast_to(last_ref[0:1, :], p.shape), pltpu.roll(p, 1, axis=0))
    last_ref[0:1, :] = p[tt - 1:tt, :]
    p = p + (prev - p) * mu_ref[...]
    r = p[:, 0:W]
    k = p[:, W:2 * W]
    v = p[:, 2 * W:3 * W]
    lw = p[:, 3 * W:3 * W + LANES]
    xg = p[:, 3 * W + LANES:3 * W + 2 * LANES]
    w0, a0, k_k, k_a, r_k, ln_w, ln_b = (pv_ref[j:j + 1, :] for j in range(7))
    lane = lax.broadcasted_iota(jnp.int32, lw.shape, 1)
    wa = _bdot(jnp.where(lane < 64, jnp.tanh(lw), lw), wwa_ref[...])
    w = -_softplus(-(w0 + wa[:, :W])) - 0.5
    logd = -jnp.exp(w)
    lr = _sigmoid(a0 + wa[:, W:])
    gate = _bdot(_sigmoid(xg), g2_ref[...])
    bd = bd_ref[...]
    kk = k * k_k
    kk = kk / jnp.maximum(jnp.sqrt(_bdot(kk * kk, bd)), 1e-12)
    k = k * (1.0 + (lr - 1.0) * k_a)
    av = -kk
    bv = kk * lr

    tri = tri_ref[...]
    lane_p = lax.broadcasted_iota(jnp.int32, (L, LANES), 1)
    m0 = lane_p < RW_HEAD
    ri = lax.broadcasted_iota(jnp.int32, (2 * L, 2 * L), 0)
    ci = lax.broadcasted_iota(jnp.int32, (2 * L, 2 * L), 1)
    same = (ri // L) == (ci // L)
    mask_s = same & (ci < ri)
    mask_i = same & (ci <= ri)
    mask_s2 = jnp.concatenate([mask_s, mask_s], axis=1)
    mask_i2 = jnp.concatenate([mask_i, mask_i], axis=1)
    eye = (ri == ci).astype(F32)

    def stack(x):
        return jnp.concatenate([jnp.where(m0, x, 0.0), jnp.where(m0, 0.0, x)], axis=0)

    n_chunks = tt // L
    n_pairs = RW_HEADS // 2
    aa, rr, vv, bk2, bkg, gam = [], [], [], [], [], []
    for c in range(n_chunks):
        rows = slice(c * L, (c + 1) * L)
        ld = logd[rows]
        cs = _dot_exact_lhs(tri, ld)
        total = cs[L - 1:L, :]
        e_out = jnp.exp(-cs)
        e_end = jnp.exp(total - cs)
        at = av[rows] * jnp.exp(cs - ld)
        rt = r[rows] * jnp.exp(cs)
        bt = bv[rows] * e_out
        kt = k[rows] * e_out
        bg = bv[rows] * e_end
        kg = k[rows] * e_end
        g_all = jnp.exp(total)
        for hp in range(n_pairs):
            ls = slice(hp * LANES, (hp + 1) * LANES)
            aa.append(stack(at[:, ls]).astype(BF16))
            rr.append(stack(rt[:, ls]))
            vv.append(stack(v[rows, ls]).astype(BF16))
            bk2.append(jnp.concatenate([bt[:, ls], bt[:, ls], kt[:, ls], kt[:, ls]], axis=0).astype(BF16))
            bkg.append(jnp.concatenate([stack(bg[:, ls]), stack(kg[:, ls])], axis=0).astype(BF16))
            gam.append(g_all[:, ls])
    n_sys = len(aa)
    rng = range(n_sys)
    n_a = [jnp.where(mask_s2, _bdot_nt(aa[i], bk2[i]), 0.0) for i in rng]
    n_r = [jnp.where(mask_i2, _bdot_nt(rr[i], bk2[i]), 0.0).astype(BF16) for i in rng]
    n_ab = [n_a[i][:, :2 * L] for i in rng]
    n_ak = [n_a[i][:, 2 * L:].astype(BF16) for i in rng]
    n_rb = [n_r[i][:, :2 * L] for i in rng]
    tinv = [eye + n_ab[i] for i in rng]
    pw = n_ab
    for _ in range(int(math.log2(L)) - 1):
        pw = [_bdot(pw[i], pw[i]) for i in rng]
        tinv = [tinv[i] + _bdot(tinv[i], pw[i]) for i in rng]
    tinv = [tinv[i].astype(BF16) for i in rng]
    tav = [_bdot(tinv[i], _bdot(n_ak[i], vv[i])).astype(BF16) for i in rng]
    uv = [jnp.concatenate([tav[i], vv[i]], axis=0) for i in rng]
    g_op = [_bdot_tn(tinv[i], bkg[i][:2 * L]) for i in rng]
    mg = [_bdot_tn(aa[i], g_op[i]).astype(BF16) for i in rng]
    w1 = [_bdot(n_rb[i], tinv[i]) for i in rng]
    ra = [(rr[i] + _bdot(w1[i], aa[i])).astype(BF16) for i in rng]
    y0 = [_bdot(n_r[i], uv[i]) for i in rng]
    s0 = [_bdot_tn(uv[i], bkg[i]) for i in rng]
    for c in range(n_chunks):
        rows = slice(c * L, (c + 1) * L)
        y_parts = []
        for hp in range(n_pairs):
            i = c * n_pairs + hp
            s = st_ref[hp]
            sb = s.astype(BF16)
            yst = _bdot_nt(ra[i], sb) + y0[i]
            y_parts.append(yst[:L] + yst[L:])
            st_ref[hp] = s * gam[i] + jnp.dot(sb, mg[i], preferred_element_type=F32) + s0[i]
        y = jnp.concatenate(y_parts, axis=1)
        mean = _bdot(y, bd) * (1.0 / RW_HEAD)
        yc = y - mean
        var = _bdot(yc * yc, bd) * (1.0 / RW_HEAD)
        yn = yc * lax.rsqrt(var + RW_LN_EPS) * ln_w + ln_b
        bonus = _bdot(r[rows] * k[rows] * r_k, bd) * v[rows]
        o_ref[rows, :] = ((yn + bonus) * gate[rows]).astype(o_ref.dtype)


def _rwkv7(p, mu, pv, wwa, g2, bsz, seq):
    tt = RW_TT
    nt = seq // tt
    L = RW_L
    hid = jnp.arange(RW_WIDTH) // RW_HEAD
    bd = (hid[:, None] == hid[None, :]).astype(BF16)
    tri = (jnp.arange(L)[:, None] >= jnp.arange(L)[None, :]).astype(BF16)
    fix = lambda b, i: (0, 0)
    return pl.pallas_call(
        functools.partial(_rw_kernel, tt=tt),
        grid=(bsz, nt),
        in_specs=[pl.BlockSpec((tt, RW_IN), lambda b, i: (b * nt + i, 0)),
                  pl.BlockSpec(mu.shape, fix), pl.BlockSpec(pv.shape, fix), pl.BlockSpec(wwa.shape, fix),
                  pl.BlockSpec(g2.shape, fix), pl.BlockSpec(bd.shape, fix), pl.BlockSpec(tri.shape, fix)],
        out_specs=pl.BlockSpec((tt, RW_WIDTH), lambda b, i: (b * nt + i, 0)),
        out_shape=jax.ShapeDtypeStruct((bsz * seq, RW_WIDTH), BF16),
        scratch_shapes=[pltpu.VMEM((RW_HEADS // 2, 2 * RW_HEAD, LANES), F32), pltpu.VMEM((8, RW_IN), F32)],
        compiler_params=_cparams("arbitrary", "arbitrary"),
        name="rwkv7",
    )(p, mu, pv, wwa, g2, bd, tri)


def _ssd_kernel(p_ref, cw_ref, cb_ref, hv_ref, dl_ref, nw_ref, tri_ref, o_ref, st_ref, tail_ref, *, L):
    i = pl.program_id(1)

    @pl.when(i == 0)
    def _():
        st_ref[...] = jnp.zeros_like(st_ref)
        tail_ref[...] = jnp.zeros_like(tail_ref)

    z = p_ref[:, 0:M2_INNER].astype(F32)
    xin = p_ref[:, M2_INNER:M2_INNER + M2_CONV_DIM].astype(F32)
    dt_raw = p_ref[:, M2_INNER + M2_CONV_DIM:M2_INNER + M2_CONV_DIM + LANES].astype(F32)
    tail = tail_ref[...]
    tail_ref[...] = xin[L - 8:L, :]
    row8 = lax.broadcasted_iota(jnp.int32, (8, M2_CONV_DIM), 0)
    conv = xin * cw_ref[M2_CONV - 1:M2_CONV, :] + cb_ref[...]
    for j in range(1, M2_CONV):
        rolled = pltpu.roll(xin, j, axis=0)
        head = jnp.where(row8 < j, pltpu.roll(tail, j, axis=0), rolled[0:8])
        shifted = jnp.concatenate([head, rolled[8:]], axis=0)
        conv = conv + shifted * cw_ref[M2_CONV - 1 - j:M2_CONV - j, :]
    xbc = _silu(conv)
    xs = xbc[:, 0:M2_INNER]
    dt = _softplus(dt_raw + hv_ref[0:1, :])
    adt = dt * hv_ref[1:2, :]
    cs = _dot_exact_lhs(tri_ref[...], adt)
    cs_t = jnp.transpose(cs)
    total = cs[L - 1:L, :]
    dec_in = jnp.exp(cs)
    dec_out = jnp.exp(total - cs)
    dec_all = jnp.exp(total)
    lane = lax.broadcasted_iota(jnp.int32, (L, LANES), 1)
    first = lane < 64
    rowp = lax.broadcasted_iota(jnp.int32, (LANES, M2_STATE), 0) < 64
    li = lax.broadcasted_iota(jnp.int32, (L, L), 0)
    si = lax.broadcasted_iota(jnp.int32, (L, L), 1)
    causal = li >= si
    y_parts = []
    for hp in range(M2_HEADS // 2):
        g = hp // 2
        h0, h1 = 2 * hp, 2 * hp + 1
        bm = xbc[:, M2_INNER + g * M2_STATE:M2_INNER + (g + 1) * M2_STATE]
        cm = xbc[:, M2_INNER + 2 * M2_STATE + g * M2_STATE:M2_INNER + 2 * M2_STATE + (g + 1) * M2_STATE]
        x_p = xs[:, hp * LANES:(hp + 1) * LANES]
        sel = lambda a: jnp.where(first, a[:, h0:h0 + 1], a[:, h1:h1 + 1])
        xd = x_p * sel(dt)
        cb = _bdot_nt(cm, bm)
        yd = []
        for h in (h0, h1):
            lmat = jnp.where(causal, jnp.exp(cs[:, h:h + 1] - cs_t[h:h + 1, :]), 0.0)
            yd.append(_bdot(cb * lmat, xd))
        s = st_ref[hp]
        y_off = _bdot_nt(cm, s) * sel(dec_in)
        y_parts.append(jnp.where(first, yd[0], yd[1]) + y_off)
        s_new = _bdot_tn(xd * sel(dec_out), bm)
        st_ref[hp] = s * jnp.where(rowp, dec_all[:, h0:h0 + 1], dec_all[:, h1:h1 + 1]) + s_new
    y = jnp.concatenate(y_parts, axis=1) + dl_ref[...] * xs
    y = y * _silu(z)
    half = M2_INNER // 2
    outs = []
    for g in range(2):
        yg = y[:, g * half:(g + 1) * half]
        outs.append(yg * lax.rsqrt(jnp.mean(yg * yg, axis=-1, keepdims=True) + RMS_EPS))
    o_ref[...] = (jnp.concatenate(outs, axis=1) * nw_ref[...]).astype(o_ref.dtype)


def _ssd(p, conv_w, conv_b, hv, d_lanes, norm_w, bsz, seq):
    L = M2_L
    nt = seq // L
    tri = (jnp.arange(L)[:, None] >= jnp.arange(L)[None, :]).astype(BF16)
    fix = lambda b, i: (0, 0)
    return pl.pallas_call(
        functools.partial(_ssd_kernel, L=L),
        grid=(bsz, nt),
        in_specs=[pl.BlockSpec((L, M2_IN_PAD), lambda b, i: (b * nt + i, 0)),
                  pl.BlockSpec(conv_w.shape, fix), pl.BlockSpec(conv_b.shape, fix), pl.BlockSpec(hv.shape, fix),
                  pl.BlockSpec(d_lanes.shape, fix), pl.BlockSpec(norm_w.shape, fix), pl.BlockSpec(tri.shape, fix)],
        out_specs=pl.BlockSpec((L, M2_INNER), lambda b, i: (b * nt + i, 0)),
        out_shape=jax.ShapeDtypeStruct((bsz * seq, M2_INNER), BF16),
        scratch_shapes=[pltpu.VMEM((M2_HEADS // 2, LANES, M2_STATE), F32), pltpu.VMEM((8, M2_CONV_DIM), F32)],
        compiler_params=_cparams("arbitrary", "arbitrary"),
        name="ssd",
    )(p, conv_w, conv_b, hv, d_lanes, norm_w, tri)


def _final_kernel(x_ref, ya_ref, yb_ref, g_ref, gate_ref, nw_ref, o_ref):
    g = g_ref[...]
    moe = g[:, 4:5] * ya_ref[...].astype(F32) + g[:, 5:6] * yb_ref[...].astype(F32)
    x = x_ref[...] + gate_ref[0] * moe
    ms = jnp.mean(x * x, axis=-1, keepdims=True)
    o_ref[...] = x * lax.rsqrt(ms + RMS_EPS) * nw_ref[...]


def _final(x, ya, yb, g, gate, nw, seq, tm):
    n_tok, d = x.shape
    tpb = seq // tm
    row = lambda i: (i, 0)
    return pl.pallas_call(
        _final_kernel,
        grid=(n_tok // tm,),
        in_specs=[pl.BlockSpec((tm, d), row), pl.BlockSpec((tm, d), row), pl.BlockSpec((tm, d), row),
                  pl.BlockSpec((tm, g.shape[1]), row), pl.BlockSpec((1, 1, d), lambda i: (i // tpb, 0, 0)),
                  pl.BlockSpec((1, d), lambda i: (0, 0))],
        out_specs=pl.BlockSpec((tm, d), row),
        out_shape=jax.ShapeDtypeStruct((n_tok, d), F32),
        compiler_params=_cparams("arbitrary"),
        name="final_norm",
    )(x, ya, yb, g, gate, nw)


def _router_weights(w_grp, b_grp, w_exp, b_exp):
    d = w_grp.shape[0]
    pad = LANES - MOE_GROUPS - MOE_EXPERTS
    w = jnp.concatenate([w_grp, w_exp, jnp.zeros((d, pad), F32)], axis=1)
    b = jnp.concatenate([b_grp, b_exp, jnp.zeros((pad,), F32)])[None, :]
    return w, b


def kernel(x, c, ada_w, ada_b, norm_mix_w, norm_ffn_w, even_w_in, even_w_out, s5_a_re, s5_a_im, s5_log_dt, s5_b_re, s5_b_im, s5_c_re, s5_c_im, s5_d, s5_w_glu, odd_w_in, odd_w_out, rw_mu, rw_w0, rw_w2, rw_a0, rw_a2, rw_g2, rw_k_k, rw_k_a, rw_r_k, rw_ln_w, rw_ln_b, m2_conv_w, m2_conv_b, m2_dt_bias, m2_a_log, m2_d, m2_norm_w, moe_w_grp, moe_b_grp, moe_w_exp, moe_b_exp, moe_w_gate, moe_w_up, moe_w_down, final_norm_w):
    bsz, seq, d = x.shape
    n_tok = bsz * seq
    tm = 512
    xt = x.reshape(n_tok, d)
    ada = _ada_params(c, ada_w, ada_b)
    mods = [[ada[i, :, j * d:(j + 1) * d].reshape(bsz, 1, d) for j in range(6)] for i in range(2)]

    sh1, sc1, g1, sh2, sc2, g2 = mods[0]
    u, q, k, v = _in_proj(xt, norm_mix_w[0][None, :], sh1, sc1, even_w_in[0].astype(BF16),
                          ((0, 512), (512, 1024), (1024, 1536), (1536, 2048)), (F32, BF16, BF16, BF16), seq, tm)
    tables = _s5_tables(s5_a_re[0], s5_a_im[0], s5_log_dt[0], s5_b_re[0], s5_b_im[0], s5_c_re[0], s5_c_im[0], s5_d[0])
    y_s5 = _s5_scan(u, tables, bsz, seq)
    y_sb = _sb_attention(q, k, v, bsz, seq)
    wr, br = _router_weights(moe_w_grp[0], moe_b_grp[0], moe_w_exp[0], moe_b_exp[0])
    x1, h2, logits = _mix_out(y_s5, y_sb, xt, g1, s5_w_glu[0].astype(BF16), even_w_out[0].astype(BF16),
                              norm_ffn_w[0][None, :], sh2, sc2, wr, br, seq, tm)
    ya, yb, gates = _moe(h2, logits, moe_w_gate, moe_w_up, moe_w_down, 0)

    gate_prev = g2
    sh1, sc1, g1, sh2, sc2, g2 = mods[1]
    w_in = jnp.concatenate([odd_w_in[0], jnp.zeros((d, M2_IN_PAD - M2_IN), F32)], axis=1).astype(BF16)
    x2, p_rw, p_m2 = _in_proj(x1, norm_mix_w[1][None, :], sh1, sc1, w_in,
                              ((0, RW_IN), (RW_IN, RW_IN + M2_IN_PAD)), (BF16, BF16), seq, tm,
                              add=(ya, yb, gates, gate_prev))
    pv = jnp.stack([rw_w0[0], rw_a0[0], rw_k_k[0], rw_k_a[0], rw_r_k[0].reshape(-1), rw_ln_w[0], rw_ln_b[0],
                    jnp.zeros((RW_WIDTH,), F32)])
    zl = jnp.zeros((64, RW_WIDTH), F32)
    wwa = jnp.concatenate([jnp.concatenate([rw_w2[0], zl], axis=1),
                           jnp.concatenate([zl, rw_a2[0]], axis=1)], axis=0).astype(BF16)
    y_rw = _rwkv7(p_rw, rw_mu[0][None, :], pv, wwa, rw_g2[0].astype(BF16), bsz, seq)
    hv = jnp.zeros((8, LANES), F32)
    hv = hv.at[0, :M2_HEADS].set(m2_dt_bias[0]).at[1, :M2_HEADS].set(-jnp.exp(m2_a_log[0]))
    y_m2 = _ssd(p_m2, m2_conv_w[0], m2_conv_b[0][None, :], hv, jnp.repeat(m2_d[0], 64)[None, :],
                m2_norm_w[0][None, :], bsz, seq)
    wr, br = _router_weights(moe_w_grp[1], moe_b_grp[1], moe_w_exp[1], moe_b_exp[1])
    x3, h2, logits = _mix_out(y_rw, y_m2, x2, g1, None, odd_w_out[0].astype(BF16),
                              norm_ffn_w[1][None, :], sh2, sc2, wr, br, seq, tm)
    ya, yb, gates = _moe(h2, logits, moe_w_gate, moe_w_up, moe_w_down, 1)
    out = _final(x3, ya, yb, gates, g2, final_norm_w[None, :], seq, tm)
    return out.reshape(bsz, seq, d)
```

```python
import functools
import math

import jax
import jax.numpy as jnp
from jax import lax
from jax.experimental import pallas as pl
from jax.experimental.pallas import tpu as pltpu

F32 = jnp.float32
BF16 = jnp.bfloat16

D_MODEL = 1024
RMS_EPS = 1e-6
LANES = 128
VMEM_LIMIT = 56 * 1024 * 1024

S5_WIDTH = 512
S5_P = 16
S5_G = 32
S5_N = 64
S5_L = 16
S5_GB = 8
SB_WIDTH = 512
SB_HEAD = 64
SB_T = 256
SB_FADE = 160.0

RW_WIDTH = 512
RW_HEAD = 64
RW_HEADS = 8
RW_IN = 1792
RW_LN_EPS = 64e-5
RW_L = 64
RW_TT = 256

M2_INNER = 512
M2_HEADS = 8
M2_STATE = 128
M2_CONV = 4
M2_CONV_DIM = 1024
M2_IN = 1544
M2_IN_PAD = 1664
M2_L = 256

MOE_GROUPS = 4
MOE_PER_GROUP = 8
MOE_EXPERTS = 32
MOE_TOPK = 2
MOE_HIDDEN = 512
MOE_BLK = 512


def _cparams(*sem):
    return pltpu.CompilerParams(dimension_semantics=sem, vmem_limit_bytes=VMEM_LIMIT)


def _bdot(a, b):
    return jnp.dot(a.astype(BF16), b.astype(BF16), preferred_element_type=F32)


def _bdot_nt(a, b):
    return lax.dot_general(a.astype(BF16), b.astype(BF16), (((1,), (1,)), ((), ())),
                           preferred_element_type=F32)


def _bdot_tn(a, b):
    return lax.dot_general(a.astype(BF16), b.astype(BF16), (((0,), (0,)), ((), ())),
                           preferred_element_type=F32)


def _split2(a):
    hi = a.astype(BF16)
    lo = (a - hi.astype(F32)).astype(BF16)
    return hi, lo


def _dot3(a, b):
    ah, al = _split2(a)
    bh, bl = _split2(b)
    d = functools.partial(jnp.dot, preferred_element_type=F32)
    return d(ah, bh) + d(ah, bl) + d(al, bh)


def _dot_exact_rhs(a, b01):
    ah, al = _split2(a)
    d = functools.partial(jnp.dot, preferred_element_type=F32)
    return d(ah, b01) + d(al, b01)


def _dot_exact_lhs(a01, b):
    bh, bl = _split2(b)
    d = functools.partial(jnp.dot, preferred_element_type=F32)
    return d(a01, bh) + d(a01, bl)


def _sigmoid(x):
    return 1.0 / (1.0 + jnp.exp(-x))


def _softplus(x):
    return jnp.maximum(x, 0.0) + jnp.log(1.0 + jnp.exp(-jnp.abs(x)))


def _silu(x):
    return x * _sigmoid(x)


def _gelu_tanh(x):
    c = math.sqrt(2.0 / math.pi)
    return 0.5 * x * (1.0 + jnp.tanh(c * (x + 0.044715 * (x * x * x))))


def _rms_mod(x, nw, shift, scale):
    ms = jnp.mean(x * x, axis=-1, keepdims=True)
    return (x * lax.rsqrt(ms + RMS_EPS)) * nw * (1.0 + scale) + shift


def _ada_kernel(c_ref, w_ref, b_ref, o_ref):
    cond = _silu(c_ref[...])
    o_ref[0] = _dot3(cond, w_ref[0]) + b_ref[0]


def _ada_params(c, ada_w, ada_b):
    depth, d, n = ada_w.shape
    bsz = c.shape[0]
    tn = 1024
    return pl.pallas_call(
        _ada_kernel,
        grid=(depth, n // tn),
        in_specs=[pl.BlockSpec((bsz, d), lambda i, j: (0, 0)),
                  pl.BlockSpec((1, d, tn), lambda i, j: (i, 0, j)),
                  pl.BlockSpec((1, 1, tn), lambda i, j: (i, 0, j))],
        out_specs=pl.BlockSpec((1, bsz, tn), lambda i, j: (i, 0, j)),
        out_shape=jax.ShapeDtypeStruct((depth, bsz, n), F32),
        compiler_params=_cparams("arbitrary", "arbitrary"),
        name="ada_params",
    )(c, ada_w, ada_b.reshape(depth, 1, n))


def _in_proj_kernel(*refs, has_add, splits):
    if has_add:
        x_ref, ya_ref, yb_ref, g_ref, gate_ref, nw_ref, sh_ref, sc_ref, w_ref = refs[:9]
        outs = refs[9:]
        x = x_ref[...]
        g = g_ref[...]
        moe = g[:, 4:5] * ya_ref[...].astype(F32) + g[:, 5:6] * yb_ref[...].astype(F32)
        x = x + gate_ref[0] * moe
        outs[0][...] = x
        outs = outs[1:]
    else:
        x_ref, nw_ref, sh_ref, sc_ref, w_ref = refs[:5]
        outs = refs[5:]
        x = x_ref[...]
    h = _rms_mod(x, nw_ref[...], sh_ref[0], sc_ref[0]).astype(BF16)
    for o_ref, (c0, c1) in zip(outs, splits):
        o_ref[...] = jnp.dot(h, w_ref[:, c0:c1], preferred_element_type=F32).astype(o_ref.dtype)


def _in_proj(x, nw, shift, scale, w, splits, dtypes, seq, tm, add=None):
    n_tok, d = x.shape
    tpb = seq // tm
    row = lambda i: (i, 0)
    bat = lambda i: (i // tpb, 0, 0)
    fix = lambda i: (0, 0)
    in_specs = [pl.BlockSpec((tm, d), row)]
    args = [x]
    out_shape = []
    out_specs = []
    if add is not None:
        ya, yb, g, gate = add
        in_specs += [pl.BlockSpec((tm, d), row), pl.BlockSpec((tm, d), row),
                     pl.BlockSpec((tm, g.shape[1]), row), pl.BlockSpec((1, 1, d), bat)]
        args += [ya, yb, g, gate]
        out_shape.append(jax.ShapeDtypeStruct((n_tok, d), F32))
        out_specs.append(pl.BlockSpec((tm, d), row))
    in_specs += [pl.BlockSpec((1, d), fix), pl.BlockSpec((1, 1, d), bat), pl.BlockSpec((1, 1, d), bat),
                 pl.BlockSpec(w.shape, fix)]
    args += [nw, shift, scale, w]
    for (c0, c1), dt in zip(splits, dtypes):
        out_shape.append(jax.ShapeDtypeStruct((n_tok, c1 - c0), dt))
        out_specs.append(pl.BlockSpec((tm, c1 - c0), row))
    return pl.pallas_call(
        functools.partial(_in_proj_kernel, has_add=add is not None, splits=splits),
        grid=(n_tok // tm,),
        in_specs=in_specs, out_specs=out_specs, out_shape=out_shape,
        compiler_params=_cparams("arbitrary"),
        name="in_proj",
    )(*args)


def _s5_tables(a_re, a_im, log_dt, b_re, b_im, c_re, c_im, d_skip):
    hp = lax.Precision.HIGHEST
    L = S5_L
    a_re = jnp.minimum(a_re.astype(F32), -1e-4)
    a_im = a_im.astype(F32)
    dt = jnp.exp(log_dt.astype(F32))[:, None]
    mag = jnp.exp(dt * a_re)
    abar_re, abar_im = mag * jnp.cos(dt * a_im), mag * jnp.sin(dt * a_im)
    den = a_re * a_re + a_im * a_im
    num_re, num_im = abar_re - 1.0, abar_im
    coef_re = (num_re * a_re + num_im * a_im) / den
    coef_im = (num_im * a_re - num_re * a_im) / den
    b_re = b_re.astype(F32)
    b_im = b_im.astype(F32)
    bb_re = coef_re[..., None] * b_re - coef_im[..., None] * b_im
    bb_im = coef_re[..., None] * b_im + coef_im[..., None] * b_re
    c_re = c_re.astype(F32)
    c_im = c_im.astype(F32)
    tau = jnp.arange(L + 1, dtype=F32)[None, :, None]
    pmag = jnp.exp(tau * (dt * a_re)[:, None, :])
    pw_re = pmag * jnp.cos(tau * (dt * a_im)[:, None, :])
    pw_im = pmag * jnp.sin(tau * (dt * a_im)[:, None, :])
    cl_re = c_re[:, None] * pw_re[:, :, None, :] - c_im[:, None] * pw_im[:, :, None, :]
    cl_im = c_re[:, None] * pw_im[:, :, None, :] + c_im[:, None] * pw_re[:, :, None, :]
    taps = (jnp.einsum('gtpn,gnq->gtpq', cl_re[:, :L], bb_re, precision=hp)
            - jnp.einsum('gtpn,gnq->gtpq', cl_im[:, :L], bb_im, precision=hp))
    nb = S5_G // S5_GB

    def compact(x):
        return x.reshape(nb, S5_GB, L, S5_P, 2 * S5_N).transpose(0, 2, 1, 3, 4).reshape(nb, L, LANES, 2 * S5_N)

    eye = jnp.eye(S5_GB, dtype=F32)
    dlag = jnp.einsum('bgtpq,gh->btgqhp', taps.reshape(nb, S5_GB, L, S5_P, S5_P), eye)
    dlag = dlag.reshape(nb, L, LANES, LANES)
    rev = L - 1 - jnp.arange(L)
    e_re = pw_re[:, rev][:, :, :, None] * bb_re[:, None] - pw_im[:, rev][:, :, :, None] * bb_im[:, None]
    e_im = pw_re[:, rev][:, :, :, None] * bb_im[:, None] + pw_im[:, rev][:, :, :, None] * bb_re[:, None]
    e_re = e_re.transpose(0, 1, 3, 2)
    e_im = e_im.transpose(0, 1, 3, 2)
    emap = jnp.stack([compact(jnp.concatenate([e_re, e_im], axis=-1)),
                      compact(jnp.concatenate([e_im, e_re], axis=-1))], axis=1)
    q_c = jnp.stack([cl_re[:, 1:L + 1], -cl_im[:, 1:L + 1]], axis=1).reshape(nb, S5_GB, 2, L, S5_P, S5_N)
    qmap = jnp.einsum('bgctpn,gh->bcgnthp', q_c, eye).reshape(nb, 2 * S5_GB * S5_N, L * LANES)
    lr = pw_re[:, L].reshape(nb, 1, S5_GB * S5_N)
    li = pw_im[:, L].reshape(nb, 1, S5_GB * S5_N)
    lam_a = jnp.concatenate([lr, lr], axis=-1)
    lam_b = jnp.concatenate([-li, li], axis=-1)
    dvec = jnp.tile(d_skip.astype(F32).reshape(nb, 1, LANES), (1, 1, L))
    return dlag.astype(BF16), emap.astype(BF16), qmap.astype(BF16), lam_a, lam_b, dvec


def _s5_kernel(u_ref, dlag_ref, emap_ref, qmap_ref, la_ref, lb_ref, dv_ref, o_ref, toep_ref, esp_ref, h_ref, *, n_rows):
    L = S5_L
    ns = 2 * S5_GB * S5_N

    @pl.when(pl.program_id(1) == 0)
    def _():
        toep_ref[...] = jnp.zeros_like(toep_ref)
        esp_ref[...] = jnp.zeros_like(esp_ref)
        for s in range(L):
            for t in range(s, L):
                toep_ref[s * LANES:(s + 1) * LANES, t * LANES:(t + 1) * LANES] = dlag_ref[0, t - s]
            for g in range(S5_GB):
                rows = slice(s * LANES + g * S5_P, s * LANES + (g + 1) * S5_P)
                src = slice(g * S5_P, (g + 1) * S5_P)
                half = slice((g % 2) * S5_N, (g % 2 + 1) * S5_N)
                re_at = g * S5_N
                im_at = ns // 2 + g * S5_N
                esp_ref[rows, re_at:re_at + S5_N] = emap_ref[0, g % 2, s, src, half]
                esp_ref[rows, im_at:im_at + S5_N] = emap_ref[0, 1 - g % 2, s, src, half]

    up = jnp.concatenate([u_ref[pl.ds(s, n_rows, stride=L), :] for s in range(L)], axis=1)
    ub = up.astype(BF16)
    h_ref[...] = jnp.dot(ub, esp_ref[...], preferred_element_type=F32)
    la = la_ref[0]
    lb = lb_ref[0]

    def step(c8, h):
        rows = pl.ds(pl.multiple_of(c8 * 8, 8), 8)
        e = h_ref[rows, :]
        ent = []
        for j in range(8):
            ent.append(h)
            h = h * la + pltpu.roll(h, ns // 2, axis=1) * lb + e[j:j + 1, :]
        h_ref[rows, :] = jnp.concatenate(ent, axis=0)
        return h

    lax.fori_loop(0, n_rows // 8, step, jnp.zeros((1, ns), F32))
    hi, lo = _split2(h_ref[...])
    qm = qmap_ref[0]
    y_off = jnp.dot(hi, qm, preferred_element_type=F32) + jnp.dot(lo, qm, preferred_element_type=F32)
    wide = 2 * LANES
    for j in range(L * LANES // wide):
        k_end = (j + 1) * wide
        cols = slice(j * wide, k_end)
        y = jnp.dot(ub[:, :k_end], toep_ref[:k_end, cols], preferred_element_type=F32)
        y = y + y_off[:, cols] + up[:, cols] * dv_ref[0, :, cols]
        for t in range(2 * j, 2 * j + 2):
            o_ref[pl.ds(t, n_rows, stride=L), :] = y[:, (t - 2 * j) * LANES:(t - 2 * j + 1) * LANES]


def _s5_scan(u, tables, bsz, seq):
    dlag, emap, qmap, lam_a, lam_b, dvec = tables
    L = S5_L
    n_rows = seq // L
    nb = S5_WIDTH // LANES
    ns = 2 * S5_GB * S5_N
    blk3 = lambda j, b: (j, 0, 0)
    blk4 = lambda j, b: (j, 0, 0, 0)
    return pl.pallas_call(
        functools.partial(_s5_kernel, n_rows=n_rows),
        grid=(nb, bsz),
        in_specs=[pl.BlockSpec((seq, LANES), lambda j, b: (b, j)),
                  pl.BlockSpec((1, L, LANES, LANES), blk4),
                  pl.BlockSpec((1, 2, L, LANES, 2 * S5_N), lambda j, b: (j, 0, 0, 0, 0)),
                  pl.BlockSpec((1, ns, L * LANES), blk3),
                  pl.BlockSpec((1, 1, ns), blk3), pl.BlockSpec((1, 1, ns), blk3), pl.BlockSpec((1, 1, L * LANES), blk3)],
        out_specs=pl.BlockSpec((seq, LANES), lambda j, b: (b, j)),
        out_shape=jax.ShapeDtypeStruct((bsz * seq, S5_WIDTH), F32),
        scratch_shapes=[pltpu.VMEM((L * LANES, L * LANES), BF16), pltpu.VMEM((L * LANES, ns), BF16),
                        pltpu.VMEM((n_rows, ns), F32)],
        compiler_params=_cparams("arbitrary", "arbitrary"),
        name="s5_scan",
    )(u, dlag, emap, qmap, lam_a, lam_b, dvec)


def _sb_kernel(q_ref, k_ref, v_ref, tri_ref, o_ref, qh_ref, hl_ref, z_ref, w_ref, acc_ref, car_ref, *, t):
    qb = pl.program_id(1)
    heads = SB_WIDTH // SB_HEAD
    lane = lax.broadcasted_iota(jnp.int32, (t, LANES), 1)
    low = lane < SB_HEAD
    for p in range(heads // 2):
        q = (q_ref[:, p * LANES:(p + 1) * LANES].astype(F32) * (0.125 * math.log2(math.e))).astype(BF16)
        zero = jnp.zeros_like(q)
        qh_ref[2 * p] = jnp.where(low, q, zero)
        qh_ref[2 * p + 1] = jnp.where(low, zero, q)
    acc_ref[...] = jnp.zeros_like(acc_ref)
    car_ref[...] = jnp.zeros_like(car_ref)

    def scores(kb, h, slot, diagonal):
        rows = pl.ds(pl.multiple_of(kb * t, t), t)
        kblk = k_ref[rows, (h // 2) * LANES:(h // 2 + 1) * LANES]
        z = lax.dot_general(qh_ref[h], kblk, (((1,), (1,)), ((), ())), preferred_element_type=F32)
        sp = jnp.maximum(z, 0.0) + jnp.log2(1.0 + jnp.exp2(-jnp.abs(z)))
        if diagonal:
            mask = lax.broadcasted_iota(jnp.int32, (t, t), 0) > lax.broadcasted_iota(jnp.int32, (t, t), 1)
            sp = jnp.where(mask, sp, 0.0)
            z = jnp.where(mask, z, -1e30)
        hi = sp.astype(BF16)
        hl_ref[slot, :, :t] = hi
        hl_ref[slot, :, t:] = (sp - hi.astype(F32)).astype(BF16)
        z_ref[slot] = z

    def weights(h, src, dst):
        cs = jnp.dot(hl_ref[src], tri_ref[...], preferred_element_type=F32)
        car = car_ref[h]
        s = z_ref[src] - cs - jnp.concatenate([car, car], axis=1)
        w_ref[dst] = jnp.exp2(s).astype(BF16)
        car_ref[h] = car + jnp.broadcast_to(cs[:, 0:1], car.shape)

    def values(kb, h, slot):
        rows = pl.ds(pl.multiple_of(kb * t, t), t)
        vblk = v_ref[rows, (h // 2) * LANES:(h // 2 + 1) * LANES]
        acc_ref[h] += jnp.dot(w_ref[slot], vblk, preferred_element_type=F32)

    def key_tile(kb, diagonal):
        for j in range(heads + 2):
            if j < heads:
                scores(kb, j, j % 2, diagonal)
            if 1 <= j <= heads:
                weights(j - 1, (j - 1) % 2, j % 2)
            if j >= 2:
                values(kb, j - 2, (j - 1) % 2)

    def faded():
        return jnp.min(car_ref[...]) >= SB_FADE

    key_tile(qb, True)

    def more(state):
        kb, _ = state
        key_tile(kb, False)
        return kb - 1, faded()

    lax.while_loop(lambda s: (s[0] >= 0) & jnp.logical_not(s[1]), more, (qb - 1, faded()))
    for p in range(heads // 2):
        o_ref[:, p * LANES:(p + 1) * LANES] = jnp.where(low, acc_ref[2 * p], acc_ref[2 * p + 1]).astype(o_ref.dtype)


def _sb_attention(q, k, v, bsz, seq):
    t = SB_T
    nq = seq // t
    heads = SB_WIDTH // SB_HEAD
    tri = (jnp.arange(t)[:, None] >= jnp.arange(t)[None, :]).astype(BF16)
    tri2 = jnp.concatenate([tri, tri], axis=0)
    return pl.pallas_call(
        functools.partial(_sb_kernel, t=t),
        grid=(bsz, nq),
        in_specs=[pl.BlockSpec((t, SB_WIDTH), lambda b, i: (b * nq + i, 0)),
                  pl.BlockSpec((seq, SB_WIDTH), lambda b, i: (b, 0)),
                  pl.BlockSpec((seq, SB_WIDTH), lambda b, i: (b, 0)),
                  pl.BlockSpec((2 * t, t), lambda b, i: (0, 0))],
        out_specs=pl.BlockSpec((t, SB_WIDTH), lambda b, i: (b * nq + i, 0)),
        out_shape=jax.ShapeDtypeStruct((bsz * seq, SB_WIDTH), BF16),
        scratch_shapes=[pltpu.VMEM((heads, t, LANES), BF16), pltpu.VMEM((2, t, 2 * t), BF16),
                        pltpu.VMEM((2, t, t), F32), pltpu.VMEM((2, t, t), BF16),
                        pltpu.VMEM((heads, t, LANES), F32), pltpu.VMEM((heads, t, LANES), F32)],
        compiler_params=_cparams("arbitrary", "arbitrary"),
        name="sb_attention",
    )(q, k, v, tri2)


def _mix_out_kernel(*refs, glu):
    if glu:
        ya_ref, yb_ref, x_ref, gate_ref, wglu_ref, wo_ref, nw_ref, sh_ref, sc_ref, wr_ref, br_ref = refs[:11]
        outs = refs[11:]
        a = _gelu_tanh(ya_ref[...].astype(F32)).astype(BF16)
        g = jnp.dot(a, wglu_ref[...], preferred_element_type=F32)
        half = g.shape[1] // 2
        ya = (g[:, :half] * _sigmoid(g[:, half:])).astype(BF16)
    else:
        ya_ref, yb_ref, x_ref, gate_ref, wo_ref, nw_ref, sh_ref, sc_ref, wr_ref, br_ref = refs[:10]
        outs = refs[10:]
        ya = ya_ref[...]
    x1_ref, h2_ref, lg_ref = outs
    ka = ya.shape[1]
    mix = (jnp.dot(ya, wo_ref[:ka, :], preferred_element_type=F32)
           + jnp.dot(yb_ref[...], wo_ref[ka:, :], preferred_element_type=F32))
    x1 = x_ref[...] + gate_ref[0] * mix
    x1_ref[...] = x1
    h2 = _rms_mod(x1, nw_ref[...], sh_ref[0], sc_ref[0])
    h2_ref[...] = h2.astype(BF16)
    lg_ref[...] = _dot3(h2, wr_ref[...]) + br_ref[...]


def _mix_out(ya, yb, x, gate, w_glu, w_out, nw, shift, scale, w_router, b_router, seq, tm):
    n_tok, d = x.shape
    tpb = seq // tm
    row = lambda i: (i, 0)
    bat = lambda i: (i // tpb, 0, 0)
    fix = lambda i: (0, 0)
    in_specs = [pl.BlockSpec((tm, ya.shape[1]), row), pl.BlockSpec((tm, yb.shape[1]), row),
                pl.BlockSpec((tm, d), row), pl.BlockSpec((1, 1, d), bat)]
    args = [ya, yb, x, gate]
    if w_glu is not None:
        in_specs.append(pl.BlockSpec(w_glu.shape, fix))
        args.append(w_glu)
    in_specs += [pl.BlockSpec(w_out.shape, fix), pl.BlockSpec((1, d), fix), pl.BlockSpec((1, 1, d), bat),
                 pl.BlockSpec((1, 1, d), bat), pl.BlockSpec(w_router.shape, fix), pl.BlockSpec(b_router.shape, fix)]
    args += [w_out, nw, shift, scale, w_router, b_router]
    return pl.pallas_call(
        functools.partial(_mix_out_kernel, glu=w_glu is not None),
        grid=(n_tok // tm,),
        in_specs=in_specs,
        out_specs=[pl.BlockSpec((tm, d), row), pl.BlockSpec((tm, d), row), pl.BlockSpec((tm, LANES), row)],
        out_shape=[jax.ShapeDtypeStruct((n_tok, d), F32), jax.ShapeDtypeStruct((n_tok, d), BF16),
                   jax.ShapeDtypeStruct((n_tok, LANES), F32)],
        compiler_params=_cparams("arbitrary"),
        name="mix_out",
    )(*args)


def _route_kernel(lg_ref, tri_ref, o_ref, ot_ref, cnt_ref, run_ref):
    @pl.when(pl.program_id(0) == 0)
    def _():
        run_ref[...] = jnp.zeros_like(run_ref)

    lg = lg_ref[...]
    lane = lax.broadcasted_iota(jnp.int32, lg.shape, 1)
    neg = jnp.float32(-jnp.inf)
    far = jnp.int32(LANES)

    def top(x):
        m = jnp.max(x, axis=1, keepdims=True)
        return m, jnp.min(jnp.where(x == m, lane, far), axis=1, keepdims=True)

    is_grp = lane < MOE_GROUPS
    gmax, gidx = top(jnp.where(is_grp, lg, neg))
    grp_p = 1.0 / jnp.sum(jnp.where(is_grp, jnp.exp(lg - gmax), 0.0), axis=1, keepdims=True)
    first = MOE_GROUPS + MOE_PER_GROUP * gidx
    el = jnp.where((lane >= first) & (lane < first + MOE_PER_GROUP), lg, neg)
    l0, i0 = top(el)
    l1, i1 = top(jnp.where(lane == i0, neg, el))
    e1 = jnp.exp(l1 - l0)
    g0 = grp_p / (1.0 + e1)
    g1 = grp_p * e1 / (1.0 + e1)
    pick0 = lane == i0
    pick1 = lane == i1
    onehot = jnp.where(pick0 | pick1, 1.0, 0.0)
    before = jnp.dot(tri_ref[...], onehot.astype(BF16), preferred_element_type=F32) + run_ref[...]
    r0 = jnp.sum(jnp.where(pick0, before, 0.0), axis=1, keepdims=True)
    r1 = jnp.sum(jnp.where(pick1, before, 0.0), axis=1, keepdims=True)
    run = run_ref[...] + jnp.sum(onehot, axis=0, keepdims=True)
    run_ref[...] = run
    cnt_ref[...] = run
    cols = ((i0 - MOE_GROUPS).astype(F32), (i1 - MOE_GROUPS).astype(F32), r0, r1, g0, g1)
    out = jnp.zeros(lg.shape, F32)
    for j, col in enumerate(cols):
        out = jnp.where(lane == j, col, out)
    o_ref[...] = out
    ot_ref[...] = jnp.transpose(out)[0:8, :]


def _route(logits, tm):
    n_tok = logits.shape[0]
    tri = (jnp.arange(tm)[:, None] > jnp.arange(tm)[None, :]).astype(BF16)
    return pl.pallas_call(
        _route_kernel,
        grid=(n_tok // tm,),
        in_specs=[pl.BlockSpec((tm, LANES), lambda i: (i, 0)), pl.BlockSpec((tm, tm), lambda i: (0, 0))],
        out_specs=[pl.BlockSpec((tm, LANES), lambda i: (i, 0)), pl.BlockSpec((8, tm), lambda i: (0, i)),
                   pl.BlockSpec((1, LANES), lambda i: (0, 0))],
        out_shape=[jax.ShapeDtypeStruct((n_tok, LANES), F32), jax.ShapeDtypeStruct((8, n_tok), F32),
                   jax.ShapeDtypeStruct((1, LANES), F32)],
        scratch_shapes=[pltpu.VMEM((1, LANES), F32)],
        compiler_params=_cparams("arbitrary"),
        name="route",
    )(logits, tri)


def _dispatch(fields, counts, n_tok):
    n_assign = n_tok * MOE_TOPK
    n_blocks = (n_assign + MOE_EXPERTS * (MOE_BLK - 1) + MOE_BLK - 1) // MOE_BLK
    n_rows = n_blocks * MOE_BLK
    experts = fields[0:2].astype(jnp.int32)
    rank = fields[2:4].astype(jnp.int32)
    counts = counts[0, MOE_GROUPS:MOE_GROUPS + MOE_EXPERTS].astype(jnp.int32)
    padded = ((counts + MOE_BLK - 1) // MOE_BLK) * MOE_BLK
    pad_end = jnp.cumsum(padded)
    pad_start = pad_end - padded
    dest = rank
    for e in range(MOE_EXPERTS):
        dest = dest + jnp.where(experts == e, pad_start[e], 0)
    blk_start = jnp.arange(n_blocks, dtype=jnp.int32) * MOE_BLK
    blk_expert = jnp.minimum(jnp.sum((pad_end[None, :] <= blk_start[:, None]).astype(jnp.int32), axis=1),
                             MOE_EXPERTS - 1).astype(jnp.int32)
    n_used = (pad_end[-1] // MOE_BLK).astype(jnp.int32).reshape(1)
    n_pad = n_rows - n_assign
    gap_end = jnp.cumsum(padded - counts)
    i = jnp.arange(n_pad, dtype=jnp.int32)
    owner = jnp.sum((gap_end[None, :] <= i[:, None]).astype(jnp.int32), axis=1)
    hot = owner[:, None] == jnp.arange(MOE_EXPERTS + 1, dtype=jnp.int32)[None, :]
    first_gap = jnp.concatenate([pad_start + counts, pad_end[-1:]])
    before = jnp.concatenate([gap_end - (padded - counts), gap_end[-1:]])
    pad_row = i + jnp.sum(jnp.where(hot, (first_gap - before)[None, :], 0), axis=1)
    keys = jnp.concatenate([dest.reshape(-1), pad_row])
    vals = jnp.concatenate([jnp.tile(jnp.arange(n_tok, dtype=jnp.int32), MOE_TOPK), pad_row % n_tok])
    _, row_tok = lax.sort_key_val(keys, vals)
    return dest, row_tok, blk_expert, n_used, n_blocks


def _moe_kernel(be_ref, nu_ref, x_ref, wg_ref, wu_ref, wd_ref, o_ref, wg_s, wu_s, wd_s):
    i = pl.program_id(0)
    e = be_ref[i]
    prev = be_ref[jnp.maximum(i - 1, 0)]

    @pl.when((i == 0) | (e != prev))
    def _():
        wg_s[...] = wg_ref[0, 0].astype(BF16)
        wu_s[...] = wu_ref[0, 0].astype(BF16)
        wd_s[...] = wd_ref[0, 0].astype(BF16)

    @pl.when(i < nu_ref[0])
    def _():
        x = x_ref[...]
        a = jnp.dot(x, wg_s[...], preferred_element_type=F32)
        b = jnp.dot(x, wu_s[...], preferred_element_type=F32)
        hid = (_silu(a) * b).astype(BF16)
        o_ref[...] = jnp.dot(hid, wd_s[...], preferred_element_type=F32).astype(o_ref.dtype)

    @pl.when(i >= nu_ref[0])
    def _():
        o_ref[...] = jnp.zeros_like(o_ref)


def _moe_experts(xs, blk_expert, n_used, w_gate, w_up, w_down, layer, n_blocks):
    d = xs.shape[1]
    hid = w_gate.shape[3]
    grid_spec = pltpu.PrefetchScalarGridSpec(
        num_scalar_prefetch=2,
        grid=(n_blocks,),
        in_specs=[pl.BlockSpec((MOE_BLK, d), lambda i, be, nu: (i, 0)),
                  pl.BlockSpec((1, 1, d, hid), lambda i, be, nu: (layer, be[i], 0, 0)),
                  pl.BlockSpec((1, 1, d, hid), lambda i, be, nu: (layer, be[i], 0, 0)),
                  pl.BlockSpec((1, 1, hid, d), lambda i, be, nu: (layer, be[i], 0, 0))],
        out_specs=pl.BlockSpec((MOE_BLK, d), lambda i, be, nu: (i, 0)),
        scratch_shapes=[pltpu.VMEM((d, hid), BF16), pltpu.VMEM((d, hid), BF16), pltpu.VMEM((hid, d), BF16)],
    )
    return pl.pallas_call(
        _moe_kernel,
        grid_spec=grid_spec,
        out_shape=jax.ShapeDtypeStruct((n_blocks * MOE_BLK, d), BF16),
        compiler_params=_cparams("arbitrary"),
        name="moe_experts",
    )(blk_expert, n_used, xs, w_gate, w_up, w_down)


def _moe(h2, logits, w_gate, w_up, w_down, layer):
    n_tok, d = h2.shape
    routed, fields, counts = _route(logits, 512)
    dest, row_tok, blk_expert, n_used, n_blocks = _dispatch(fields, counts, n_tok)
    rows = lambda a, idx: a.at[idx].get(mode='promise_in_bounds')
    xs = rows(h2, row_tok)
    ys = _moe_experts(xs, blk_expert, n_used, w_gate, w_up, w_down, layer, n_blocks)
    return rows(ys, dest[0]), rows(ys, dest[1]), routed


def _rw_kernel(p_ref, mu_ref, pv_ref, wwa_ref, g2_ref, bd_ref, tri_ref, o_ref, st_ref, last_ref, *, tt):
    i = pl.program_id(1)
    L = RW_L
    W = RW_WIDTH

    @pl.when(i == 0)
    def _():
        st_ref[...] = jnp.zeros_like(st_ref)
        last_ref[...] = jnp.zeros_like(last_ref)

    p = p_ref[...].astype(F32)
    row = lax.broadcasted_iota(jnp.int32, p.shape, 0)
    prev = jnp.where(row == 0, jnp.broadcast_to(last_ref[0:1, :], p.shape), pltpu.roll(p, 1, axis=0))
    last_ref[0:1, :] = p[tt - 1:tt, :]
    p = p + (prev - p) * mu_ref[...]
    r = p[:, 0:W]
    k = p[:, W:2 * W]
    v = p[:, 2 * W:3 * W]
    lw = p[:, 3 * W:3 * W + LANES]
    xg = p[:, 3 * W + LANES:3 * W + 2 * LANES]
    w0, a0, k_k, k_a, r_k, ln_w, ln_b = (pv_ref[j:j + 1, :] for j in range(7))
    lane = lax.broadcasted_iota(jnp.int32, lw.shape, 1)
    wa = _bdot(jnp.where(lane < 64, jnp.tanh(lw), lw), wwa_ref[...])
    w = -_softplus(-(w0 + wa[:, :W])) - 0.5
    logd = -jnp.exp(w)
    lr = _sigmoid(a0 + wa[:, W:])
    gate = _bdot(_sigmoid(xg), g2_ref[...])
    bd = bd_ref[...]
    kk = k * k_k
    kk = kk / jnp.maximum(jnp.sqrt(_bdot(kk * kk, bd)), 1e-12)
    k = k * (1.0 + (lr - 1.0) * k_a)
    av = -kk
    bv = kk * lr

    tri = tri_ref[...]
    lane_p = lax.broadcasted_iota(jnp.int32, (L, LANES), 1)
    m0 = lane_p < RW_HEAD
    ri = lax.broadcasted_iota(jnp.int32, (2 * L, 2 * L), 0)
    ci = lax.broadcasted_iota(jnp.int32, (2 * L, 2 * L), 1)
    same = (ri // L) == (ci // L)
    mask_s = same & (ci < ri)
    mask_i = same & (ci <= ri)
    mask_s2 = jnp.concatenate([mask_s, mask_s], axis=1)
    mask_i2 = jnp.concatenate([mask_i, mask_i], axis=1)
    eye = (ri == ci).astype(F32)

    def stack(x):
        return jnp.concatenate([jnp.where(m0, x, 0.0), jnp.where(m0, 0.0, x)], axis=0)

    n_chunks = tt // L
    n_pairs = RW_HEADS // 2
    aa, rr, vv, bk2, bkg, gam = [], [], [], [], [], []
    for c in range(n_chunks):
        rows = slice(c * L, (c + 1) * L)
        ld = logd[rows]
        cs = _dot_exact_lhs(tri, ld)
        total = cs[L - 1:L, :]
        e_out = jnp.exp(-cs)
        e_end = jnp.exp(total - cs)
        at = av[rows] * jnp.exp(cs - ld)
        rt = r[rows] * jnp.exp(cs)
        bt = bv[rows] * e_out
        kt = k[rows] * e_out
        bg = bv[rows] * e_end
        kg = k[rows] * e_end
        g_all = jnp.exp(total)
        for hp in range(n_pairs):
            ls = slice(hp * LANES, (hp + 1) * LANES)
            aa.append(stack(at[:, ls]).astype(BF16))
            rr.append(stack(rt[:, ls]))
            vv.append(stack(v[rows, ls]).astype(BF16))
            bk2.append(jnp.concatenate([bt[:, ls], bt[:, ls], kt[:, ls], kt[:, ls]], axis=0).astype(BF16))
            bkg.append(jnp.concatenate([stack(bg[:, ls]), stack(kg[:, ls])], axis=0).astype(BF16))
            gam.append(g_all[:, ls])
    n_sys = len(aa)
    rng = range(n_sys)
    n_a = [jnp.where(mask_s2, _bdot_nt(aa[i], bk2[i]), 0.0) for i in rng]
    n_r = [jnp.where(mask_i2, _bdot_nt(rr[i], bk2[i]), 0.0).astype(BF16) for i in rng]
    n_ab = [n_a[i][:, :2 * L] for i in rng]
    n_ak = [n_a[i][:, 2 * L:].astype(BF16) for i in rng]
    n_rb = [n_r[i][:, :2 * L] for i in rng]
    tinv = [eye + n_ab[i] for i in rng]
    pw = n_ab
    for _ in range(int(math.log2(L)) - 1):
        pw = [_bdot(pw[i], pw[i]) for i in rng]
        tinv = [tinv[i] + _bdot(tinv[i], pw[i]) for i in rng]
    tinv = [tinv[i].astype(BF16) for i in rng]
    tav = [_bdot(tinv[i], _bdot(n_ak[i], vv[i])).astype(BF16) for i in rng]
    uv = [jnp.concatenate([tav[i], vv[i]], axis=0) for i in rng]
    g_op = [_bdot_tn(tinv[i], bkg[i][:2 * L]) for i in rng]
    mg = [_bdot_tn(aa[i], g_op[i]).astype(BF16) for i in rng]
    w1 = [_bdot(n_rb[i], tinv[i]) for i in rng]
    ra = [(rr[i] + _bdot(w1[i], aa[i])).astype(BF16) for i in rng]
    y0 = [_bdot(n_r[i], uv[i]) for i in rng]
    s0 = [_bdot_tn(uv[i], bkg[i]) for i in rng]
    for c in range(n_chunks):
        rows = slice(c * L, (c + 1) * L)
        y_parts = []
        for hp in range(n_pairs):
            i = c * n_pairs + hp
            s = st_ref[hp]
            sb = s.astype(BF16)
            yst = _bdot_nt(ra[i], sb) + y0[i]
            y_parts.append(yst[:L] + yst[L:])
            st_ref[hp] = s * gam[i] + jnp.dot(sb, mg[i], preferred_element_type=F32) + s0[i]
        y = jnp.concatenate(y_parts, axis=1)
        mean = _bdot(y, bd) * (1.0 / RW_HEAD)
        yc = y - mean
        var = _bdot(yc * yc, bd) * (1.0 / RW_HEAD)
        yn = yc * lax.rsqrt(var + RW_LN_EPS) * ln_w + ln_b
        bonus = _bdot(r[rows] * k[rows] * r_k, bd) * v[rows]
        o_ref[rows, :] = ((yn + bonus) * gate[rows]).astype(o_ref.dtype)


def _rwkv7(p, mu, pv, wwa, g2, bsz, seq):
    tt = RW_TT
    nt = seq // tt
    L = RW_L
    hid = jnp.arange(RW_WIDTH) // RW_HEAD
    bd = (hid[:, None] == hid[None, :]).astype(BF16)
    tri = (jnp.arange(L)[:, None] >= jnp.arange(L)[None, :]).astype(BF16)
    fix = lambda b, i: (0, 0)
    return pl.pallas_call(
        functools.partial(_rw_kernel, tt=tt),
        grid=(bsz, nt),
        in_specs=[pl.BlockSpec((tt, RW_IN), lambda b, i: (b * nt + i, 0)),
                  pl.BlockSpec(mu.shape, fix), pl.BlockSpec(pv.shape, fix), pl.BlockSpec(wwa.shape, fix),
                  pl.BlockSpec(g2.shape, fix), pl.BlockSpec(bd.shape, fix), pl.BlockSpec(tri.shape, fix)],
        out_specs=pl.BlockSpec((tt, RW_WIDTH), lambda b, i: (b * nt + i, 0)),
        out_shape=jax.ShapeDtypeStruct((bsz * seq, RW_WIDTH), BF16),
        scratch_shapes=[pltpu.VMEM((RW_HEADS // 2, 2 * RW_HEAD, LANES), F32), pltpu.VMEM((8, RW_IN), F32)],
        compiler_params=_cparams("arbitrary", "arbitrary"),
        name="rwkv7",
    )(p, mu, pv, wwa, g2, bd, tri)


def _ssd_kernel(p_ref, cw_ref, cb_ref, hv_ref, dl_ref, nw_ref, tri_ref, o_ref, st_ref, tail_ref, *, L):
    i = pl.program_id(1)

    @pl.when(i == 0)
    def _():
        st_ref[...] = jnp.zeros_like(st_ref)
        tail_ref[...] = jnp.zeros_like(tail_ref)

    z = p_ref[:, 0:M2_INNER].astype(F32)
    xin = p_ref[:, M2_INNER:M2_INNER + M2_CONV_DIM].astype(F32)
    dt_raw = p_ref[:, M2_INNER + M2_CONV_DIM:M2_INNER + M2_CONV_DIM + LANES].astype(F32)
    tail = tail_ref[...]
    tail_ref[...] = xin[L - 8:L, :]
    row8 = lax.broadcasted_iota(jnp.int32, (8, M2_CONV_DIM), 0)
    conv = xin * cw_ref[M2_CONV - 1:M2_CONV, :] + cb_ref[...]
    for j in range(1, M2_CONV):
        rolled = pltpu.roll(xin, j, axis=0)
        head = jnp.where(row8 < j, pltpu.roll(tail, j, axis=0), rolled[0:8])
        shifted = jnp.concatenate([head, rolled[8:]], axis=0)
        conv = conv + shifted * cw_ref[M2_CONV - 1 - j:M2_CONV - j, :]
    xbc = _silu(conv)
    xs = xbc[:, 0:M2_INNER]
    dt = _softplus(dt_raw + hv_ref[0:1, :])
    adt = dt * hv_ref[1:2, :]
    cs = _dot_exact_lhs(tri_ref[...], adt)
    cs_t = jnp.transpose(cs)
    total = cs[L - 1:L, :]
    dec_in = jnp.exp(cs)
    dec_out = jnp.exp(total - cs)
    dec_all = jnp.exp(total)
    lane = lax.broadcasted_iota(jnp.int32, (L, LANES), 1)
    first = lane < 64
    rowp = lax.broadcasted_iota(jnp.int32, (LANES, M2_STATE), 0) < 64
    li = lax.broadcasted_iota(jnp.int32, (L, L), 0)
    si = lax.broadcasted_iota(jnp.int32, (L, L), 1)
    causal = li >= si
    y_parts = []
    for hp in range(M2_HEADS // 2):
        g = hp // 2
        h0, h1 = 2 * hp, 2 * hp + 1
        bm = xbc[:, M2_INNER + g * M2_STATE:M2_INNER + (g + 1) * M2_STATE]
        cm = xbc[:, M2_INNER + 2 * M2_STATE + g * M2_STATE:M2_INNER + 2 * M2_STATE + (g + 1) * M2_STATE]
        x_p = xs[:, hp * LANES:(hp + 1) * LANES]
        sel = lambda a: jnp.where(first, a[:, h0:h0 + 1], a[:, h1:h1 + 1])
        xd = x_p * sel(dt)
        cb = _bdot_nt(cm, bm)
        yd = []
        for h in (h0, h1):
            lmat = jnp.where(causal, jnp.exp(cs[:, h:h + 1] - cs_t[h:h + 1, :]), 0.0)
            yd.append(_bdot(cb * lmat, xd))
        s = st_ref[hp]
        y_off = _bdot_nt(cm, s) * sel(dec_in)
        y_parts.append(jnp.where(first, yd[0], yd[1]) + y_off)
        s_new = _bdot_tn(xd * sel(dec_out), bm)
        st_ref[hp] = s * jnp.where(rowp, dec_all[:, h0:h0 + 1], dec_all[:, h1:h1 + 1]) + s_new
    y = jnp.concatenate(y_parts, axis=1) + dl_ref[...] * xs
    y = y * _silu(z)
    half = M2_INNER // 2
    outs = []
    for g in range(2):
        yg = y[:, g * half:(g + 1) * half]
        outs.append(yg * lax.rsqrt(jnp.mean(yg * yg, axis=-1, keepdims=True) + RMS_EPS))
    o_ref[...] = (jnp.concatenate(outs, axis=1) * nw_ref[...]).astype(o_ref.dtype)


def _ssd(p, conv_w, conv_b, hv, d_lanes, norm_w, bsz, seq):
    L = M2_L
    nt = seq // L
    tri = (jnp.arange(L)[:, None] >= jnp.arange(L)[None, :]).astype(BF16)
    fix = lambda b, i: (0, 0)
    return pl.pallas_call(
        functools.partial(_ssd_kernel, L=L),
        grid=(bsz, nt),
        in_specs=[pl.BlockSpec((L, M2_IN_PAD), lambda b, i: (b * nt + i, 0)),
                  pl.BlockSpec(conv_w.shape, fix), pl.BlockSpec(conv_b.shape, fix), pl.BlockSpec(hv.shape, fix),
                  pl.BlockSpec(d_lanes.shape, fix), pl.BlockSpec(norm_w.shape, fix), pl.BlockSpec(tri.shape, fix)],
        out_specs=pl.BlockSpec((L, M2_INNER), lambda b, i: (b * nt + i, 0)),
        out_shape=jax.ShapeDtypeStruct((bsz * seq, M2_INNER), BF16),
        scratch_shapes=[pltpu.VMEM((M2_HEADS // 2, LANES, M2_STATE), F32), pltpu.VMEM((8, M2_CONV_DIM), F32)],
        compiler_params=_cparams("arbitrary", "arbitrary"),
        name="ssd",
    )(p, conv_w, conv_b, hv, d_lanes, norm_w, tri)


def _final_kernel(x_ref, ya_ref, yb_ref, g_ref, gate_ref, nw_ref, o_ref):
    g = g_ref[...]
    moe = g[:, 4:5] * ya_ref[...].astype(F32) + g[:, 5:6] * yb_ref[...].astype(F32)
    x = x_ref[...] + gate_ref[0] * moe
    ms = jnp.mean(x * x, axis=-1, keepdims=True)
    o_ref[...] = x * lax.rsqrt(ms + RMS_EPS) * nw_ref[...]


def _final(x, ya, yb, g, gate, nw, seq, tm):
    n_tok, d = x.shape
    tpb = seq // tm
    row = lambda i: (i, 0)
    return pl.pallas_call(
        _final_kernel,
        grid=(n_tok // tm,),
        in_specs=[pl.BlockSpec((tm, d), row), pl.BlockSpec((tm, d), row), pl.BlockSpec((tm, d), row),
                  pl.BlockSpec((tm, g.shape[1]), row), pl.BlockSpec((1, 1, d), lambda i: (i // tpb, 0, 0)),
                  pl.BlockSpec((1, d), lambda i: (0, 0))],
        out_specs=pl.BlockSpec((tm, d), row),
        out_shape=jax.ShapeDtypeStruct((n_tok, d), F32),
        compiler_params=_cparams("arbitrary"),
        name="final_norm",
    )(x, ya, yb, g, gate, nw)


def _router_weights(w_grp, b_grp, w_exp, b_exp):
    d = w_grp.shape[0]
    pad = LANES - MOE_GROUPS - MOE_EXPERTS
    w = jnp.concatenate([w_grp, w_exp, jnp.zeros((d, pad), F32)], axis=1)
    b = jnp.concatenate([b_grp, b_exp, jnp.zeros((pad,), F32)])[None, :]
    return w, b


def kernel(x, c, ada_w, ada_b, norm_mix_w, norm_ffn_w, even_w_in, even_w_out, s5_a_re, s5_a_im, s5_log_dt, s5_b_re, s5_b_im, s5_c_re, s5_c_im, s5_d, s5_w_glu, odd_w_in, odd_w_out, rw_mu, rw_w0, rw_w2, rw_a0, rw_a2, rw_g2, rw_k_k, rw_k_a, rw_r_k, rw_ln_w, rw_ln_b, m2_conv_w, m2_conv_b, m2_dt_bias, m2_a_log, m2_d, m2_norm_w, moe_w_grp, moe_b_grp, moe_w_exp, moe_b_exp, moe_w_gate, moe_w_up, moe_w_down, final_norm_w):
    bsz, seq, d = x.shape
    n_tok = bsz * seq
    tm = 512
    xt = x.reshape(n_tok, d)
    ada = _ada_params(c, ada_w, ada_b)
    mods = [[ada[i, :, j * d:(j + 1) * d].reshape(bsz, 1, d) for j in range(6)] for i in range(2)]

    sh1, sc1, g1, sh2, sc2, g2 = mods[0]
    u, q, k, v = _in_proj(xt, norm_mix_w[0][None, :], sh1, sc1, even_w_in[0].astype(BF16),
                          ((0, 512), (512, 1024), (1024, 1536), (1536, 2048)), (F32, BF16, BF16, BF16), seq, tm)
    tables = _s5_tables(s5_a_re[0], s5_a_im[0], s5_log_dt[0], s5_b_re[0], s5_b_im[0], s5_c_re[0], s5_c_im[0], s5_d[0])
    y_s5 = _s5_scan(u, tables, bsz, seq)
    y_sb = _sb_attention(q, k, v, bsz, seq)
    wr, br = _router_weights(moe_w_grp[0], moe_b_grp[0], moe_w_exp[0], moe_b_exp[0])
    x1, h2, logits = _mix_out(y_s5, y_sb, xt, g1, s5_w_glu[0].astype(BF16), even_w_out[0].astype(BF16),
                              norm_ffn_w[0][None, :], sh2, sc2, wr, br, seq, tm)
    ya, yb, gates = _moe(h2, logits, moe_w_gate, moe_w_up, moe_w_down, 0)

    gate_prev = g2
    sh1, sc1, g1, sh2, sc2, g2 = mods[1]
    w_in = jnp.concatenate([odd_w_in[0], jnp.zeros((d, M2_IN_PAD - M2_IN), F32)], axis=1).astype(BF16)
    x2, p_rw, p_m2 = _in_proj(x1, norm_mix_w[1][None, :], sh1, sc1, w_in,
                              ((0, RW_IN), (RW_IN, RW_IN + M2_IN_PAD)), (BF16, BF16), seq, tm,
                              add=(ya, yb, gates, gate_prev))
    pv = jnp.stack([rw_w0[0], rw_a0[0], rw_k_k[0], rw_k_a[0], rw_r_k[0].reshape(-1), rw_ln_w[0], rw_ln_b[0],
                    jnp.zeros((RW_WIDTH,), F32)])
    zl = jnp.zeros((64, RW_WIDTH), F32)
    wwa = jnp.concatenate([jnp.concatenate([rw_w2[0], zl], axis=1),
                           jnp.concatenate([zl, rw_a2[0]], axis=1)], axis=0).astype(BF16)
    y_rw = _rwkv7(p_rw, rw_mu[0][None, :], pv, wwa, rw_g2[0].astype(BF16), bsz, seq)
    hv = jnp.zeros((8, LANES), F32)
    hv = hv.at[0, :M2_HEADS].set(m2_dt_bias[0]).at[1, :M2_HEADS].set(-jnp.exp(m2_a_log[0]))
    y_m2 = _ssd(p_m2, m2_conv_w[0], m2_conv_b[0][None, :], hv, jnp.repeat(m2_d[0], 64)[None, :],
                m2_norm_w[0][None, :], bsz, seq)
    wr, br = _router_weights(moe_w_grp[1], moe_b_grp[1], moe_w_exp[1], moe_b_exp[1])
    x3, h2, logits = _mix_out(y_rw, y_m2, x2, g1, None, odd_w_out[0].astype(BF16),
                              norm_ffn_w[1][None, :], sh2, sc2, wr, br, seq, tm)
    ya, yb, gates = _moe(h2, logits, moe_w_gate, moe_w_up, moe_w_down, 1)
    out = _final(x3, ya, yb, gates, g2, final_norm_w[None, :], seq, tm)
    return out.reshape(bsz, seq, d)
```

```python
import functools
import math

import jax
import jax.numpy as jnp
from jax import lax
from jax.experimental import pallas as pl
from jax.experimental.pallas import tpu as pltpu

F32 = jnp.float32
BF16 = jnp.bfloat16

D_MODEL = 1024
RMS_EPS = 1e-6
LANES = 128
VMEM_LIMIT = 56 * 1024 * 1024

S5_WIDTH = 512
S5_P = 16
S5_G = 32
S5_N = 64
S5_L = 16
S5_GB = 8
SB_WIDTH = 512
SB_HEAD = 64
SB_T = 256
SB_FADE = 160.0

RW_WIDTH = 512
RW_HEAD = 64
RW_HEADS = 8
RW_IN = 1792
RW_LN_EPS = 64e-5
RW_L = 64
RW_TT = 256

M2_INNER = 512
M2_HEADS = 8
M2_STATE = 128
M2_CONV = 4
M2_CONV_DIM = 1024
M2_IN = 1544
M2_IN_PAD = 1664
M2_L = 256

MOE_GROUPS = 4
MOE_PER_GROUP = 8
MOE_EXPERTS = 32
MOE_TOPK = 2
MOE_HIDDEN = 512
MOE_BLK = 512


def _cparams(*sem):
    return pltpu.CompilerParams(dimension_semantics=sem, vmem_limit_bytes=VMEM_LIMIT)


def _bdot(a, b):
    return jnp.dot(a.astype(BF16), b.astype(BF16), preferred_element_type=F32)


def _bdot_nt(a, b):
    return lax.dot_general(a.astype(BF16), b.astype(BF16), (((1,), (1,)), ((), ())),
                           preferred_element_type=F32)


def _bdot_tn(a, b):
    return lax.dot_general(a.astype(BF16), b.astype(BF16), (((0,), (0,)), ((), ())),
                           preferred_element_type=F32)


def _split2(a):
    hi = a.astype(BF16)
    lo = (a - hi.astype(F32)).astype(BF16)
    return hi, lo


def _dot3(a, b):
    ah, al = _split2(a)
    bh, bl = _split2(b)
    d = functools.partial(jnp.dot, preferred_element_type=F32)
    return d(ah, bh) + d(ah, bl) + d(al, bh)


def _dot_exact_rhs(a, b01):
    ah, al = _split2(a)
    d = functools.partial(jnp.dot, preferred_element_type=F32)
    return d(ah, b01) + d(al, b01)


def _dot_exact_lhs(a01, b):
    bh, bl = _split2(b)
    d = functools.partial(jnp.dot, preferred_element_type=F32)
    return d(a01, bh) + d(a01, bl)


def _sigmoid(x):
    return 1.0 / (1.0 + jnp.exp(-x))


def _softplus(x):
    return jnp.maximum(x, 0.0) + jnp.log(1.0 + jnp.exp(-jnp.abs(x)))


def _silu(x):
    return x * _sigmoid(x)


def _gelu_tanh(x):
    c = math.sqrt(2.0 / math.pi)
    return 0.5 * x * (1.0 + jnp.tanh(c * (x + 0.044715 * (x * x * x))))


def _rms_mod(x, nw, shift, scale):
    ms = jnp.mean(x * x, axis=-1, keepdims=True)
    return (x * lax.rsqrt(ms + RMS_EPS)) * nw * (1.0 + scale) + shift


def _ada_kernel(c_ref, w_ref, b_ref, o_ref):
    cond = _silu(c_ref[...])
    o_ref[0] = _dot3(cond, w_ref[0]) + b_ref[0]


def _ada_params(c, ada_w, ada_b):
    depth, d, n = ada_w.shape
    bsz = c.shape[0]
    tn = 1024
    return pl.pallas_call(
        _ada_kernel,
        grid=(depth, n // tn),
        in_specs=[pl.BlockSpec((bsz, d), lambda i, j: (0, 0)),
                  pl.BlockSpec((1, d, tn), lambda i, j: (i, 0, j)),
                  pl.BlockSpec((1, 1, tn), lambda i, j: (i, 0, j))],
        out_specs=pl.BlockSpec((1, bsz, tn), lambda i, j: (i, 0, j)),
        out_shape=jax.ShapeDtypeStruct((depth, bsz, n), F32),
        compiler_params=_cparams("arbitrary", "arbitrary"),
        name="ada_params",
    )(c, ada_w, ada_b.reshape(depth, 1, n))


def _in_proj_kernel(*refs, has_add, splits):
    if has_add:
        x_ref, ya_ref, yb_ref, g_ref, gate_ref, nw_ref, sh_ref, sc_ref, w_ref = refs[:9]
        outs = refs[9:]
        x = x_ref[...]
        g = g_ref[...]
        moe = g[:, 4:5] * ya_ref[...].astype(F32) + g[:, 5:6] * yb_ref[...].astype(F32)
        x = x + gate_ref[0] * moe
        outs[0][...] = x
        outs = outs[1:]
    else:
        x_ref, nw_ref, sh_ref, sc_ref, w_ref = refs[:5]
        outs = refs[5:]
        x = x_ref[...]
    h = _rms_mod(x, nw_ref[...], sh_ref[0], sc_ref[0]).astype(BF16)
    for o_ref, (c0, c1) in zip(outs, splits):
        o_ref[...] = jnp.dot(h, w_ref[:, c0:c1], preferred_element_type=F32).astype(o_ref.dtype)


def _in_proj(x, nw, shift, scale, w, splits, dtypes, seq, tm, add=None):
    n_tok, d = x.shape
    tpb = seq // tm
    row = lambda i: (i, 0)
    bat = lambda i: (i // tpb, 0, 0)
    fix = lambda i: (0, 0)
    in_specs = [pl.BlockSpec((tm, d), row)]
    args = [x]
    out_shape = []
    out_specs = []
    if add is not None:
        yy, g, gate = add
        in_specs += [pl.BlockSpec((tm, d), row), pl.BlockSpec((tm, d), lambda i: (i + n_tok // tm, 0)),
                     pl.BlockSpec((tm, g.shape[1]), row), pl.BlockSpec((1, 1, d), bat)]
        args += [yy, yy, g, gate]
        out_shape.append(jax.ShapeDtypeStruct((n_tok, d), F32))
        out_specs.append(pl.BlockSpec((tm, d), row))
    in_specs += [pl.BlockSpec((1, d), fix), pl.BlockSpec((1, 1, d), bat), pl.BlockSpec((1, 1, d), bat),
                 pl.BlockSpec(w.shape, fix)]
    args += [nw, shift, scale, w]
    for (c0, c1), dt in zip(splits, dtypes):
        out_shape.append(jax.ShapeDtypeStruct((n_tok, c1 - c0), dt))
        out_specs.append(pl.BlockSpec((tm, c1 - c0), row))
    return pl.pallas_call(
        functools.partial(_in_proj_kernel, has_add=add is not None, splits=splits),
        grid=(n_tok // tm,),
        in_specs=in_specs, out_specs=out_specs, out_shape=out_shape,
        compiler_params=_cparams("arbitrary"),
        name="in_proj",
    )(*args)


def _s5_tables(a_re, a_im, log_dt, b_re, b_im, c_re, c_im, d_skip):
    hp = lax.Precision.HIGHEST
    L = S5_L
    a_re = jnp.minimum(a_re.astype(F32), -1e-4)
    a_im = a_im.astype(F32)
    dt = jnp.exp(log_dt.astype(F32))[:, None]
    mag = jnp.exp(dt * a_re)
    abar_re, abar_im = mag * jnp.cos(dt * a_im), mag * jnp.sin(dt * a_im)
    den = a_re * a_re + a_im * a_im
    num_re, num_im = abar_re - 1.0, abar_im
    coef_re = (num_re * a_re + num_im * a_im) / den
    coef_im = (num_im * a_re - num_re * a_im) / den
    b_re = b_re.astype(F32)
    b_im = b_im.astype(F32)
    bb_re = coef_re[..., None] * b_re - coef_im[..., None] * b_im
    bb_im = coef_re[..., None] * b_im + coef_im[..., None] * b_re
    c_re = c_re.astype(F32)
    c_im = c_im.astype(F32)
    tau = jnp.arange(L + 1, dtype=F32)[None, :, None]
    pmag = jnp.exp(tau * (dt * a_re)[:, None, :])
    pw_re = pmag * jnp.cos(tau * (dt * a_im)[:, None, :])
    pw_im = pmag * jnp.sin(tau * (dt * a_im)[:, None, :])
    cl_re = c_re[:, None] * pw_re[:, :, None, :] - c_im[:, None] * pw_im[:, :, None, :]
    cl_im = c_re[:, None] * pw_im[:, :, None, :] + c_im[:, None] * pw_re[:, :, None, :]
    taps = (jnp.einsum('gtpn,gnq->gtpq', cl_re[:, :L], bb_re, precision=hp)
            - jnp.einsum('gtpn,gnq->gtpq', cl_im[:, :L], bb_im, precision=hp))
    nb = S5_G // S5_GB

    def compact(x):
        return x.reshape(nb, S5_GB, L, S5_P, 2 * S5_N).transpose(0, 2, 1, 3, 4).reshape(nb, L, LANES, 2 * S5_N)

    eye = jnp.eye(S5_GB, dtype=F32)
    dlag = jnp.einsum('bgtpq,gh->btgqhp', taps.reshape(nb, S5_GB, L, S5_P, S5_P), eye)
    dlag = dlag.reshape(nb, L, LANES, LANES)
    rev = L - 1 - jnp.arange(L)
    e_re = pw_re[:, rev][:, :, :, None] * bb_re[:, None] - pw_im[:, rev][:, :, :, None] * bb_im[:, None]
    e_im = pw_re[:, rev][:, :, :, None] * bb_im[:, None] + pw_im[:, rev][:, :, :, None] * bb_re[:, None]
    e_re = e_re.transpose(0, 1, 3, 2)
    e_im = e_im.transpose(0, 1, 3, 2)
    emap = jnp.stack([compact(jnp.concatenate([e_re, e_im], axis=-1)),
                      compact(jnp.concatenate([e_im, e_re], axis=-1))], axis=1)
    q_c = jnp.stack([cl_re[:, 1:L + 1], -cl_im[:, 1:L + 1]], axis=1).reshape(nb, S5_GB, 2, L, S5_P, S5_N)
    qmap = jnp.einsum('bgctpn,gh->bcgnthp', q_c, eye).reshape(nb, 2 * S5_GB * S5_N, L * LANES)
    lr = pw_re[:, L].reshape(nb, 1, S5_GB * S5_N)
    li = pw_im[:, L].reshape(nb, 1, S5_GB * S5_N)
    lam_a = jnp.concatenate([lr, lr], axis=-1)
    lam_b = jnp.concatenate([-li, li], axis=-1)
    dvec = jnp.tile(d_skip.astype(F32).reshape(nb, 1, LANES), (1, 1, L))
    return dlag.astype(BF16), emap.astype(BF16), qmap.astype(BF16), lam_a, lam_b, dvec


def _s5_kernel(u_ref, dlag_ref, emap_ref, qmap_ref, la_ref, lb_ref, dv_ref, o_ref, toep_ref, esp_ref, h_ref, *, n_rows):
    L = S5_L
    ns = 2 * S5_GB * S5_N

    @pl.when(pl.program_id(1) == 0)
    def _():
        toep_ref[...] = jnp.zeros_like(toep_ref)
        esp_ref[...] = jnp.zeros_like(esp_ref)
        for s in range(L):
            for t in range(s, L):
                toep_ref[s * LANES:(s + 1) * LANES, t * LANES:(t + 1) * LANES] = dlag_ref[0, t - s]
            for g in range(S5_GB):
                rows = slice(s * LANES + g * S5_P, s * LANES + (g + 1) * S5_P)
                src = slice(g * S5_P, (g + 1) * S5_P)
                half = slice((g % 2) * S5_N, (g % 2 + 1) * S5_N)
                re_at = g * S5_N
                im_at = ns // 2 + g * S5_N
                esp_ref[rows, re_at:re_at + S5_N] = emap_ref[0, g % 2, s, src, half]
                esp_ref[rows, im_at:im_at + S5_N] = emap_ref[0, 1 - g % 2, s, src, half]

    up = jnp.concatenate([u_ref[pl.ds(s, n_rows, stride=L), :] for s in range(L)], axis=1)
    ub = up.astype(BF16)
    h_ref[...] = jnp.dot(ub, esp_ref[...], preferred_element_type=F32)
    la = la_ref[0]
    lb = lb_ref[0]

    def step(c8, h):
        rows = pl.ds(pl.multiple_of(c8 * 8, 8), 8)
        e = h_ref[rows, :]
        ent = []
        for j in range(8):
            ent.append(h)
            h = h * la + pltpu.roll(h, ns // 2, axis=1) * lb + e[j:j + 1, :]
        h_ref[rows, :] = jnp.concatenate(ent, axis=0)
        return h

    lax.fori_loop(0, n_rows // 8, step, jnp.zeros((1, ns), F32))
    hi, lo = _split2(h_ref[...])
    qm = qmap_ref[0]
    y_off = jnp.dot(hi, qm, preferred_element_type=F32) + jnp.dot(lo, qm, preferred_element_type=F32)
    wide = 2 * LANES
    for j in range(L * LANES // wide):
        k_end = (j + 1) * wide
        cols = slice(j * wide, k_end)
        y = jnp.dot(ub[:, :k_end], toep_ref[:k_end, cols], preferred_element_type=F32)
        y = y + y_off[:, cols] + up[:, cols] * dv_ref[0, :, cols]
        for t in range(2 * j, 2 * j + 2):
            o_ref[pl.ds(t, n_rows, stride=L), :] = y[:, (t - 2 * j) * LANES:(t - 2 * j + 1) * LANES]


def _s5_scan(u, tables, bsz, seq):
    dlag, emap, qmap, lam_a, lam_b, dvec = tables
    L = S5_L
    n_rows = seq // L
    nb = S5_WIDTH // LANES
    ns = 2 * S5_GB * S5_N
    blk3 = lambda j, b: (j, 0, 0)
    blk4 = lambda j, b: (j, 0, 0, 0)
    return pl.pallas_call(
        functools.partial(_s5_kernel, n_rows=n_rows),
        grid=(nb, bsz),
        in_specs=[pl.BlockSpec((seq, LANES), lambda j, b: (b, j)),
                  pl.BlockSpec((1, L, LANES, LANES), blk4),
                  pl.BlockSpec((1, 2, L, LANES, 2 * S5_N), lambda j, b: (j, 0, 0, 0, 0)),
                  pl.BlockSpec((1, ns, L * LANES), blk3),
                  pl.BlockSpec((1, 1, ns), blk3), pl.BlockSpec((1, 1, ns), blk3), pl.BlockSpec((1, 1, L * LANES), blk3)],
        out_specs=pl.BlockSpec((seq, LANES), lambda j, b: (b, j)),
        out_shape=jax.ShapeDtypeStruct((bsz * seq, S5_WIDTH), F32),
        scratch_shapes=[pltpu.VMEM((L * LANES, L * LANES), BF16), pltpu.VMEM((L * LANES, ns), BF16),
                        pltpu.VMEM((n_rows, ns), F32)],
        compiler_params=_cparams("arbitrary", "arbitrary"),
        name="s5_scan",
    )(u, dlag, emap, qmap, lam_a, lam_b, dvec)


def _sb_kernel(q_ref, k_ref, v_ref, tri_ref, o_ref, qh_ref, hl_ref, z_ref, w_ref, acc_ref, car_ref, *, t):
    qb = pl.program_id(1)
    heads = SB_WIDTH // SB_HEAD
    lane = lax.broadcasted_iota(jnp.int32, (t, LANES), 1)
    low = lane < SB_HEAD
    for p in range(heads // 2):
        q = (q_ref[:, p * LANES:(p + 1) * LANES].astype(F32) * (0.125 * math.log2(math.e))).astype(BF16)
        zero = jnp.zeros_like(q)
        qh_ref[2 * p] = jnp.where(low, q, zero)
        qh_ref[2 * p + 1] = jnp.where(low, zero, q)
    acc_ref[...] = jnp.zeros_like(acc_ref)
    car_ref[...] = jnp.zeros_like(car_ref)

    def scores(kb, h, slot, diagonal):
        rows = pl.ds(pl.multiple_of(kb * t, t), t)
        kblk = k_ref[rows, (h // 2) * LANES:(h // 2 + 1) * LANES]
        z = lax.dot_general(qh_ref[h], kblk, (((1,), (1,)), ((), ())), preferred_element_type=F32)
        sp = jnp.maximum(z, 0.0) + jnp.log2(1.0 + jnp.exp2(-jnp.abs(z)))
        if diagonal:
            mask = lax.broadcasted_iota(jnp.int32, (t, t), 0) > lax.broadcasted_iota(jnp.int32, (t, t), 1)
            sp = jnp.where(mask, sp, 0.0)
            z = jnp.where(mask, z, -1e30)
        hi = sp.astype(BF16)
        hl_ref[slot, :, :t] = hi
        hl_ref[slot, :, t:] = (sp - hi.astype(F32)).astype(BF16)
        z_ref[slot] = z

    def weights(h, src, dst):
        cs = jnp.dot(hl_ref[src], tri_ref[...], preferred_element_type=F32)
        car = car_ref[h]
        s = z_ref[src] - cs - jnp.concatenate([car, car], axis=1)
        w_ref[dst] = jnp.exp2(s).astype(BF16)
        car_ref[h] = car + jnp.broadcast_to(cs[:, 0:1], car.shape)

    def values(kb, h, slot):
        rows = pl.ds(pl.multiple_of(kb * t, t), t)
        vblk = v_ref[rows, (h // 2) * LANES:(h // 2 + 1) * LANES]
        acc_ref[h] += jnp.dot(w_ref[slot], vblk, preferred_element_type=F32)

    def key_tile(kb, diagonal):
        for j in range(heads + 2):
            if j < heads:
                scores(kb, j, j % 2, diagonal)
            if 1 <= j <= heads:
                weights(j - 1, (j - 1) % 2, j % 2)
            if j >= 2:
                values(kb, j - 2, (j - 1) % 2)

    def faded():
        return jnp.min(car_ref[...]) >= SB_FADE

    key_tile(qb, True)

    def more(state):
        kb, _ = state
        key_tile(kb, False)
        return kb - 1, faded()

    lax.while_loop(lambda s: (s[0] >= 0) & jnp.logical_not(s[1]), more, (qb - 1, faded()))
    for p in range(heads // 2):
        o_ref[:, p * LANES:(p + 1) * LANES] = jnp.where(low, acc_ref[2 * p], acc_ref[2 * p + 1]).astype(o_ref.dtype)


def _sb_attention(q, k, v, bsz, seq):
    t = SB_T
    nq = seq // t
    heads = SB_WIDTH // SB_HEAD
    tri = (jnp.arange(t)[:, None] >= jnp.arange(t)[None, :]).astype(BF16)
    tri2 = jnp.concatenate([tri, tri], axis=0)
    return pl.pallas_call(
        functools.partial(_sb_kernel, t=t),
        grid=(bsz, nq),
        in_specs=[pl.BlockSpec((t, SB_WIDTH), lambda b, i: (b * nq + i, 0)),
                  pl.BlockSpec((seq, SB_WIDTH), lambda b, i: (b, 0)),
                  pl.BlockSpec((seq, SB_WIDTH), lambda b, i: (b, 0)),
                  pl.BlockSpec((2 * t, t), lambda b, i: (0, 0))],
        out_specs=pl.BlockSpec((t, SB_WIDTH), lambda b, i: (b * nq + i, 0)),
        out_shape=jax.ShapeDtypeStruct((bsz * seq, SB_WIDTH), BF16),
        scratch_shapes=[pltpu.VMEM((heads, t, LANES), BF16), pltpu.VMEM((2, t, 2 * t), BF16),
                        pltpu.VMEM((2, t, t), F32), pltpu.VMEM((2, t, t), BF16),
                        pltpu.VMEM((heads, t, LANES), F32), pltpu.VMEM((heads, t, LANES), F32)],
        compiler_params=_cparams("arbitrary", "arbitrary"),
        name="sb_attention",
    )(q, k, v, tri2)


def _mix_out_kernel(*refs, glu):
    if glu:
        ya_ref, yb_ref, x_ref, gate_ref, wglu_ref, wo_ref, nw_ref, sh_ref, sc_ref, wr_ref, br_ref = refs[:11]
        outs = refs[11:]
        a = _gelu_tanh(ya_ref[...].astype(F32)).astype(BF16)
        g = jnp.dot(a, wglu_ref[...], preferred_element_type=F32)
        half = g.shape[1] // 2
        ya = (g[:, :half] * _sigmoid(g[:, half:])).astype(BF16)
    else:
        ya_ref, yb_ref, x_ref, gate_ref, wo_ref, nw_ref, sh_ref, sc_ref, wr_ref, br_ref = refs[:10]
        outs = refs[10:]
        ya = ya_ref[...]
    x1_ref, h2_ref, lg_ref = outs
    ka = ya.shape[1]
    mix = (jnp.dot(ya, wo_ref[:ka, :], preferred_element_type=F32)
           + jnp.dot(yb_ref[...], wo_ref[ka:, :], preferred_element_type=F32))
    x1 = x_ref[...] + gate_ref[0] * mix
    x1_ref[...] = x1
    h2 = _rms_mod(x1, nw_ref[...], sh_ref[0], sc_ref[0])
    h2_ref[...] = h2.astype(BF16)
    lg_ref[...] = _dot3(h2, wr_ref[...]) + br_ref[...]


def _mix_out(ya, yb, x, gate, w_glu, w_out, nw, shift, scale, w_router, b_router, seq, tm):
    n_tok, d = x.shape
    tpb = seq // tm
    row = lambda i: (i, 0)
    bat = lambda i: (i // tpb, 0, 0)
    fix = lambda i: (0, 0)
    in_specs = [pl.BlockSpec((tm, ya.shape[1]), row), pl.BlockSpec((tm, yb.shape[1]), row),
                pl.BlockSpec((tm, d), row), pl.BlockSpec((1, 1, d), bat)]
    args = [ya, yb, x, gate]
    if w_glu is not None:
        in_specs.append(pl.BlockSpec(w_glu.shape, fix))
        args.append(w_glu)
    in_specs += [pl.BlockSpec(w_out.shape, fix), pl.BlockSpec((1, d), fix), pl.BlockSpec((1, 1, d), bat),
                 pl.BlockSpec((1, 1, d), bat), pl.BlockSpec(w_router.shape, fix), pl.BlockSpec(b_router.shape, fix)]
    args += [w_out, nw, shift, scale, w_router, b_router]
    return pl.pallas_call(
        functools.partial(_mix_out_kernel, glu=w_glu is not None),
        grid=(n_tok // tm,),
        in_specs=in_specs,
        out_specs=[pl.BlockSpec((tm, d), row), pl.BlockSpec((tm, d), row), pl.BlockSpec((tm, LANES), row)],
        out_shape=[jax.ShapeDtypeStruct((n_tok, d), F32), jax.ShapeDtypeStruct((n_tok, d), BF16),
                   jax.ShapeDtypeStruct((n_tok, LANES), F32)],
        compiler_params=_cparams("arbitrary"),
        name="mix_out",
    )(*args)


def _route_kernel(lg_ref, tri_ref, o_ref, ot_ref, cnt_ref, run_ref):
    @pl.when(pl.program_id(0) == 0)
    def _():
        run_ref[...] = jnp.zeros_like(run_ref)

    lg = lg_ref[...]
    lane = lax.broadcasted_iota(jnp.int32, lg.shape, 1)
    neg = jnp.float32(-jnp.inf)
    far = jnp.int32(LANES)

    def top(x):
        m = jnp.max(x, axis=1, keepdims=True)
        return m, jnp.min(jnp.where(x == m, lane, far), axis=1, keepdims=True)

    is_grp = lane < MOE_GROUPS
    gmax, gidx = top(jnp.where(is_grp, lg, neg))
    grp_p = 1.0 / jnp.sum(jnp.where(is_grp, jnp.exp(lg - gmax), 0.0), axis=1, keepdims=True)
    first = MOE_GROUPS + MOE_PER_GROUP * gidx
    el = jnp.where((lane >= first) & (lane < first + MOE_PER_GROUP), lg, neg)
    l0, i0 = top(el)
    l1, i1 = top(jnp.where(lane == i0, neg, el))
    e1 = jnp.exp(l1 - l0)
    g0 = grp_p / (1.0 + e1)
    g1 = grp_p * e1 / (1.0 + e1)
    pick0 = lane == i0
    pick1 = lane == i1
    onehot = jnp.where(pick0 | pick1, 1.0, 0.0)
    before = jnp.dot(tri_ref[...], onehot.astype(BF16), preferred_element_type=F32) + run_ref[...]
    r0 = jnp.sum(jnp.where(pick0, before, 0.0), axis=1, keepdims=True)
    r1 = jnp.sum(jnp.where(pick1, before, 0.0), axis=1, keepdims=True)
    run = run_ref[...] + jnp.sum(onehot, axis=0, keepdims=True)
    run_ref[...] = run
    cnt_ref[...] = run
    cols = ((i0 - MOE_GROUPS).astype(F32), (i1 - MOE_GROUPS).astype(F32), r0, r1, g0, g1)
    out = jnp.zeros(lg.shape, F32)
    for j, col in enumerate(cols):
        out = jnp.where(lane == j, col, out)
    o_ref[...] = out
    ot_ref[...] = jnp.transpose(out)[0:8, :]


def _route(logits, tm):
    n_tok = logits.shape[0]
    tri = (jnp.arange(tm)[:, None] > jnp.arange(tm)[None, :]).astype(BF16)
    return pl.pallas_call(
        _route_kernel,
        grid=(n_tok // tm,),
        in_specs=[pl.BlockSpec((tm, LANES), lambda i: (i, 0)), pl.BlockSpec((tm, tm), lambda i: (0, 0))],
        out_specs=[pl.BlockSpec((tm, LANES), lambda i: (i, 0)), pl.BlockSpec((8, tm), lambda i: (0, i)),
                   pl.BlockSpec((1, LANES), lambda i: (0, 0))],
        out_shape=[jax.ShapeDtypeStruct((n_tok, LANES), F32), jax.ShapeDtypeStruct((8, n_tok), F32),
                   jax.ShapeDtypeStruct((1, LANES), F32)],
        scratch_shapes=[pltpu.VMEM((1, LANES), F32)],
        compiler_params=_cparams("arbitrary"),
        name="route",
    )(logits, tri)


def _dispatch(fields, counts, n_tok):
    n_assign = n_tok * MOE_TOPK
    n_blocks = (n_assign + MOE_EXPERTS * (MOE_BLK - 1) + MOE_BLK - 1) // MOE_BLK
    n_rows = n_blocks * MOE_BLK
    experts = fields[0:2].astype(jnp.int32)
    rank = fields[2:4].astype(jnp.int32)
    counts = counts[0, MOE_GROUPS:MOE_GROUPS + MOE_EXPERTS].astype(jnp.int32)
    padded = ((counts + MOE_BLK - 1) // MOE_BLK) * MOE_BLK
    pad_end = jnp.cumsum(padded)
    pad_start = pad_end - padded
    dest = rank
    for e in range(MOE_EXPERTS):
        dest = dest + jnp.where(experts == e, pad_start[e], 0)
    blk_start = jnp.arange(n_blocks, dtype=jnp.int32) * MOE_BLK
    blk_expert = jnp.minimum(jnp.sum((pad_end[None, :] <= blk_start[:, None]).astype(jnp.int32), axis=1),
                             MOE_EXPERTS - 1).astype(jnp.int32)
    n_used = (pad_end[-1] // MOE_BLK).astype(jnp.int32).reshape(1)
    n_pad = n_rows - n_assign
    gap_end = jnp.cumsum(padded - counts)
    i = jnp.arange(n_pad, dtype=jnp.int32)
    owner = jnp.sum((gap_end[None, :] <= i[:, None]).astype(jnp.int32), axis=1)
    hot = owner[:, None] == jnp.arange(MOE_EXPERTS + 1, dtype=jnp.int32)[None, :]
    first_gap = jnp.concatenate([pad_start + counts, pad_end[-1:]])
    before = jnp.concatenate([gap_end - (padded - counts), gap_end[-1:]])
    pad_row = i + jnp.sum(jnp.where(hot, (first_gap - before)[None, :], 0), axis=1)
    keys = jnp.concatenate([dest.reshape(-1), pad_row])
    vals = jnp.concatenate([jnp.tile(jnp.arange(n_tok, dtype=jnp.int32), MOE_TOPK), pad_row % n_tok])
    _, row_tok = lax.sort((keys, vals), num_keys=1, is_stable=False)
    return dest, row_tok, blk_expert, n_used, n_blocks


def _moe_kernel(be_ref, nu_ref, x_ref, wg_ref, wu_ref, wd_ref, o_ref, wg_s, wu_s, wd_s):
    i = pl.program_id(0)
    e = be_ref[i]
    prev = be_ref[jnp.maximum(i - 1, 0)]

    @pl.when((i == 0) | (e != prev))
    def _():
        wg_s[...] = wg_ref[0, 0].astype(BF16)
        wu_s[...] = wu_ref[0, 0].astype(BF16)
        wd_s[...] = wd_ref[0, 0].astype(BF16)

    @pl.when(i < nu_ref[0])
    def _():
        x = x_ref[...]
        a = jnp.dot(x, wg_s[...], preferred_element_type=F32)
        b = jnp.dot(x, wu_s[...], preferred_element_type=F32)
        hid = (_silu(a) * b).astype(BF16)
        o_ref[...] = jnp.dot(hid, wd_s[...], preferred_element_type=F32).astype(o_ref.dtype)

    @pl.when(i >= nu_ref[0])
    def _():
        o_ref[...] = jnp.zeros_like(o_ref)


def _moe_experts(xs, blk_expert, n_used, w_gate, w_up, w_down, layer, n_blocks):
    d = xs.shape[1]
    hid = w_gate.shape[3]
    grid_spec = pltpu.PrefetchScalarGridSpec(
        num_scalar_prefetch=2,
        grid=(n_blocks,),
        in_specs=[pl.BlockSpec((MOE_BLK, d), lambda i, be, nu: (i, 0)),
                  pl.BlockSpec((1, 1, d, hid), lambda i, be, nu: (layer, be[i], 0, 0)),
                  pl.BlockSpec((1, 1, d, hid), lambda i, be, nu: (layer, be[i], 0, 0)),
                  pl.BlockSpec((1, 1, hid, d), lambda i, be, nu: (layer, be[i], 0, 0))],
        out_specs=pl.BlockSpec((MOE_BLK, d), lambda i, be, nu: (i, 0)),
        scratch_shapes=[pltpu.VMEM((d, hid), BF16), pltpu.VMEM((d, hid), BF16), pltpu.VMEM((hid, d), BF16)],
    )
    return pl.pallas_call(
        _moe_kernel,
        grid_spec=grid_spec,
        out_shape=jax.ShapeDtypeStruct((n_blocks * MOE_BLK, d), BF16),
        compiler_params=_cparams("arbitrary"),
        name="moe_experts",
    )(blk_expert, n_used, xs, w_gate, w_up, w_down)


def _moe(h2, logits, w_gate, w_up, w_down, layer):
    n_tok, d = h2.shape
    routed, fields, counts = _route(logits, 512)
    dest, row_tok, blk_expert, n_used, n_blocks = _dispatch(fields, counts, n_tok)
    rows = lambda a, idx: a.at[idx].get(mode='promise_in_bounds')
    xs = rows(h2, row_tok)
    ys = _moe_experts(xs, blk_expert, n_used, w_gate, w_up, w_down, layer, n_blocks)
    return rows(ys, dest.reshape(-1)), routed


def _rw_kernel(p_ref, mu_ref, pv_ref, wwa_ref, g2_ref, bd_ref, tri_ref, o_ref, st_ref, last_ref, *, tt):
    i = pl.program_id(1)
    L = RW_L
    W = RW_WIDTH

    @pl.when(i == 0)
    def _():
        st_ref[...] = jnp.zeros_like(st_ref)
        last_ref[...] = jnp.zeros_like(last_ref)

    p = p_ref[...].astype(F32)
    row = lax.broadcasted_iota(jnp.int32, p.shape, 0)
    prev = jnp.where(row == 0, jnp.broadcast_to(last_ref[0:1, :], p.shape), pltpu.roll(p, 1, axis=0))
    last_ref[0:1, :] = p[tt - 1:tt, :]
    p = p + (prev - p) * mu_ref[...]
    r = p[:, 0:W]
    k = p[:, W:2 * W]
    v = p[:, 2 * W:3 * W]
    lw = p[:, 3 * W:3 * W + LANES]
    xg = p[:, 3 * W + LANES:3 * W + 2 * LANES]
    w0, a0, k_k, k_a, r_k, ln_w, ln_b = (pv_ref[j:j + 1, :] for j in range(7))
    lane = lax.broadcasted_iota(jnp.int32, lw.shape, 1)
    wa = _bdot(jnp.where(lane < 64, jnp.tanh(lw), lw), wwa_ref[...])
    w = -_softplus(-(w0 + wa[:, :W])) - 0.5
    logd = -jnp.exp(w)
    lr = _sigmoid(a0 + wa[:, W:])
    gate = _bdot(_sigmoid(xg), g2_ref[...])
    bd = bd_ref[...]
    kk = k * k_k
    kk = kk / jnp.maximum(jnp.sqrt(_bdot(kk * kk, bd)), 1e-12)
    k = k * (1.0 + (lr - 1.0) * k_a)
    av = -kk
    bv = kk * lr

    tri = tri_ref[...]
    lane_p = lax.broadcasted_iota(jnp.int32, (L, LANES), 1)
    m0 = lane_p < RW_HEAD
    ri = lax.broadcasted_iota(jnp.int32, (2 * L, 2 * L), 0)
    ci = lax.broadcasted_iota(jnp.int32, (2 * L, 2 * L), 1)
    same = (ri // L) == (ci // L)
    mask_s = same & (ci < ri)
    mask_i = same & (ci <= ri)
    mask_s2 = jnp.concatenate([mask_s, mask_s], axis=1)
    mask_i2 = jnp.concatenate([mask_i, mask_i], axis=1)
    eye = (ri == ci).astype(F32)

    def stack(x):
        return jnp.concatenate([jnp.where(m0, x, 0.0), jnp.where(m0, 0.0, x)], axis=0)

    n_chunks = tt // L
    n_pairs = RW_HEADS // 2
    aa, rr, vv, bk2, bkg, gam = [], [], [], [], [], []
    for c in range(n_chunks):
        rows = slice(c * L, (c + 1) * L)
        ld = logd[rows]
        cs = _dot_exact_lhs(tri, ld)
        total = cs[L - 1:L, :]
        e_out = jnp.exp(-cs)
        e_end = jnp.exp(total - cs)
        at = av[rows] * jnp.exp(cs - ld)
        rt = r[rows] * jnp.exp(cs)
        bt = bv[rows] * e_out
        kt = k[rows] * e_out
        bg = bv[rows] * e_end
        kg = k[rows] * e_end
        g_all = jnp.exp(total)
        for hp in range(n_pairs):
            ls = slice(hp * LANES, (hp + 1) * LANES)
            aa.append(stack(at[:, ls]).astype(BF16))
            rr.append(stack(rt[:, ls]))
            vv.append(stack(v[rows, ls]).astype(BF16))
            bk2.append(jnp.concatenate([bt[:, ls], bt[:, ls], kt[:, ls], kt[:, ls]], axis=0).astype(BF16))
            bkg.append(jnp.concatenate([stack(bg[:, ls]), stack(kg[:, ls])], axis=0).astype(BF16))
            gam.append(g_all[:, ls])
    n_sys = len(aa)
    rng = range(n_sys)
    n_a = [jnp.where(mask_s2, _bdot_nt(aa[i], bk2[i]), 0.0) for i in rng]
    n_r = [jnp.where(mask_i2, _bdot_nt(rr[i], bk2[i]), 0.0).astype(BF16) for i in rng]
    n_ab = [n_a[i][:, :2 * L] for i in rng]
    n_ak = [n_a[i][:, 2 * L:].astype(BF16) for i in rng]
    n_rb = [n_r[i][:, :2 * L] for i in rng]
    tinv = [eye + n_ab[i] for i in rng]
    pw = n_ab
    for _ in range(int(math.log2(L)) - 1):
        pw = [_bdot(pw[i], pw[i]) for i in rng]
        tinv = [tinv[i] + _bdot(tinv[i], pw[i]) for i in rng]
    tinv = [tinv[i].astype(BF16) for i in rng]
    tav = [_bdot(tinv[i], _bdot(n_ak[i], vv[i])).astype(BF16) for i in rng]
    uv = [jnp.concatenate([tav[i], vv[i]], axis=0) for i in rng]
    g_op = [_bdot_tn(tinv[i], bkg[i][:2 * L]) for i in rng]
    mg = [_bdot_tn(aa[i], g_op[i]).astype(BF16) for i in rng]
    w1 = [_bdot(n_rb[i], tinv[i]) for i in rng]
    ra = [(rr[i] + _bdot(w1[i], aa[i])).astype(BF16) for i in rng]
    y0 = [_bdot(n_r[i], uv[i]) for i in rng]
    s0 = [_bdot_tn(uv[i], bkg[i]) for i in rng]
    for c in range(n_chunks):
        rows = slice(c * L, (c + 1) * L)
        y_parts = []
        for hp in range(n_pairs):
            i = c * n_pairs + hp
            s = st_ref[hp]
            sb = s.astype(BF16)
            yst = _bdot_nt(ra[i], sb) + y0[i]
            y_parts.append(yst[:L] + yst[L:])
            st_ref[hp] = s * gam[i] + jnp.dot(sb, mg[i], preferred_element_type=F32) + s0[i]
        y = jnp.concatenate(y_parts, axis=1)
        mean = _bdot(y, bd) * (1.0 / RW_HEAD)
        yc = y - mean
        var = _bdot(yc * yc, bd) * (1.0 / RW_HEAD)
        yn = yc * lax.rsqrt(var + RW_LN_EPS) * ln_w + ln_b
        bonus = _bdot(r[rows] * k[rows] * r_k, bd) * v[rows]
        o_ref[rows, :] = ((yn + bonus) * gate[rows]).astype(o_ref.dtype)


def _rwkv7(p, mu, pv, wwa, g2, bsz, seq):
    tt = RW_TT
    nt = seq // tt
    L = RW_L
    hid = jnp.arange(RW_WIDTH) // RW_HEAD
    bd = (hid[:, None] == hid[None, :]).astype(BF16)
    tri = (jnp.arange(L)[:, None] >= jnp.arange(L)[None, :]).astype(BF16)
    fix = lambda b, i: (0, 0)
    return pl.pallas_call(
        functools.partial(_rw_kernel, tt=tt),
        grid=(bsz, nt),
        in_specs=[pl.BlockSpec((tt, RW_IN), lambda b, i: (b * nt + i, 0)),
                  pl.BlockSpec(mu.shape, fix), pl.BlockSpec(pv.shape, fix), pl.BlockSpec(wwa.shape, fix),
                  pl.BlockSpec(g2.shape, fix), pl.BlockSpec(bd.shape, fix), pl.BlockSpec(tri.shape, fix)],
        out_specs=pl.BlockSpec((tt, RW_WIDTH), lambda b, i: (b * nt + i, 0)),
        out_shape=jax.ShapeDtypeStruct((bsz * seq, RW_WIDTH), BF16),
        scratch_shapes=[pltpu.VMEM((RW_HEADS // 2, 2 * RW_HEAD, LANES), F32), pltpu.VMEM((8, RW_IN), F32)],
        compiler_params=_cparams("arbitrary", "arbitrary"),
        name="rwkv7",
    )(p, mu, pv, wwa, g2, bd, tri)


def _ssd_kernel(p_ref, cw_ref, cb_ref, hv_ref, dl_ref, nw_ref, tri_ref, o_ref, st_ref, tail_ref, *, L):
    i = pl.program_id(1)

    @pl.when(i == 0)
    def _():
        st_ref[...] = jnp.zeros_like(st_ref)
        tail_ref[...] = jnp.zeros_like(tail_ref)

    z = p_ref[:, 0:M2_INNER].astype(F32)
    xin = p_ref[:, M2_INNER:M2_INNER + M2_CONV_DIM].astype(F32)
    dt_raw = p_ref[:, M2_INNER + M2_CONV_DIM:M2_INNER + M2_CONV_DIM + LANES].astype(F32)
    tail = tail_ref[...]
    tail_ref[...] = xin[L - 8:L, :]
    row8 = lax.broadcasted_iota(jnp.int32, (8, M2_CONV_DIM), 0)
    conv = xin * cw_ref[M2_CONV - 1:M2_CONV, :] + cb_ref[...]
    for j in range(1, M2_CONV):
        rolled = pltpu.roll(xin, j, axis=0)
        head = jnp.where(row8 < j, pltpu.roll(tail, j, axis=0), rolled[0:8])
        shifted = jnp.concatenate([head, rolled[8:]], axis=0)
        conv = conv + shifted * cw_ref[M2_CONV - 1 - j:M2_CONV - j, :]
    xbc = _silu(conv)
    xs = xbc[:, 0:M2_INNER]
    dt = _softplus(dt_raw + hv_ref[0:1, :])
    adt = dt * hv_ref[1:2, :]
    cs = _dot_exact_lhs(tri_ref[...], adt)
    cs_t = jnp.transpose(cs)
    total = cs[L - 1:L, :]
    dec_in = jnp.exp(cs)
    dec_out = jnp.exp(total - cs)
    dec_all = jnp.exp(total)
    lane = lax.broadcasted_iota(jnp.int32, (L, LANES), 1)
    first = lane < 64
    rowp = lax.broadcasted_iota(jnp.int32, (LANES, M2_STATE), 0) < 64
    li = lax.broadcasted_iota(jnp.int32, (L, L), 0)
    si = lax.broadcasted_iota(jnp.int32, (L, L), 1)
    causal = li >= si
    y_parts = []
    for hp in range(M2_HEADS // 2):
        g = hp // 2
        h0, h1 = 2 * hp, 2 * hp + 1
        bm = xbc[:, M2_INNER + g * M2_STATE:M2_INNER + (g + 1) * M2_STATE]
        cm = xbc[:, M2_INNER + 2 * M2_STATE + g * M2_STATE:M2_INNER + 2 * M2_STATE + (g + 1) * M2_STATE]
        x_p = xs[:, hp * LANES:(hp + 1) * LANES]
        sel = lambda a: jnp.where(first, a[:, h0:h0 + 1], a[:, h1:h1 + 1])
        xd = x_p * sel(dt)
        cb = _bdot_nt(cm, bm)
        yd = []
        for h in (h0, h1):
            lmat = jnp.where(causal, jnp.exp(cs[:, h:h + 1] - cs_t[h:h + 1, :]), 0.0)
            yd.append(_bdot(cb * lmat, xd))
        s = st_ref[hp]
        y_off = _bdot_nt(cm, s) * sel(dec_in)
        y_parts.append(jnp.where(first, yd[0], yd[1]) + y_off)
        s_new = _bdot_tn(xd * sel(dec_out), bm)
        st_ref[hp] = s * jnp.where(rowp, dec_all[:, h0:h0 + 1], dec_all[:, h1:h1 + 1]) + s_new
    y = jnp.concatenate(y_parts, axis=1) + dl_ref[...] * xs
    y = y * _silu(z)
    half = M2_INNER // 2
    outs = []
    for g in range(2):
        yg = y[:, g * half:(g + 1) * half]
        outs.append(yg * lax.rsqrt(jnp.mean(yg * yg, axis=-1, keepdims=True) + RMS_EPS))
    o_ref[...] = (jnp.concatenate(outs, axis=1) * nw_ref[...]).astype(o_ref.dtype)


def _ssd(p, conv_w, conv_b, hv, d_lanes, norm_w, bsz, seq):
    L = M2_L
    nt = seq // L
    tri = (jnp.arange(L)[:, None] >= jnp.arange(L)[None, :]).astype(BF16)
    fix = lambda b, i: (0, 0)
    return pl.pallas_call(
        functools.partial(_ssd_kernel, L=L),
        grid=(bsz, nt),
        in_specs=[pl.BlockSpec((L, M2_IN_PAD), lambda b, i: (b * nt + i, 0)),
                  pl.BlockSpec(conv_w.shape, fix), pl.BlockSpec(conv_b.shape, fix), pl.BlockSpec(hv.shape, fix),
                  pl.BlockSpec(d_lanes.shape, fix), pl.BlockSpec(norm_w.shape, fix), pl.BlockSpec(tri.shape, fix)],
        out_specs=pl.BlockSpec((L, M2_INNER), lambda b, i: (b * nt + i, 0)),
        out_shape=jax.ShapeDtypeStruct((bsz * seq, M2_INNER), BF16),
        scratch_shapes=[pltpu.VMEM((M2_HEADS // 2, LANES, M2_STATE), F32), pltpu.VMEM((8, M2_CONV_DIM), F32)],
        compiler_params=_cparams("arbitrary", "arbitrary"),
        name="ssd",
    )(p, conv_w, conv_b, hv, d_lanes, norm_w, tri)


def _final_kernel(x_ref, ya_ref, yb_ref, g_ref, gate_ref, nw_ref, o_ref):
    g = g_ref[...]
    moe = g[:, 4:5] * ya_ref[...].astype(F32) + g[:, 5:6] * yb_ref[...].astype(F32)
    x = x_ref[...] + gate_ref[0] * moe
    ms = jnp.mean(x * x, axis=-1, keepdims=True)
    o_ref[...] = x * lax.rsqrt(ms + RMS_EPS) * nw_ref[...]


def _final(x, yy, g, gate, nw, seq, tm):
    n_tok, d = x.shape
    tpb = seq // tm
    row = lambda i: (i, 0)
    return pl.pallas_call(
        _final_kernel,
        grid=(n_tok // tm,),
        in_specs=[pl.BlockSpec((tm, d), row), pl.BlockSpec((tm, d), row),
                  pl.BlockSpec((tm, d), lambda i: (i + n_tok // tm, 0)),
                  pl.BlockSpec((tm, g.shape[1]), row), pl.BlockSpec((1, 1, d), lambda i: (i // tpb, 0, 0)),
                  pl.BlockSpec((1, d), lambda i: (0, 0))],
        out_specs=pl.BlockSpec((tm, d), row),
        out_shape=jax.ShapeDtypeStruct((n_tok, d), F32),
        compiler_params=_cparams("arbitrary"),
        name="final_norm",
    )(x, yy, yy, g, gate, nw)


def _router_weights(w_grp, b_grp, w_exp, b_exp):
    d = w_grp.shape[0]
    pad = LANES - MOE_GROUPS - MOE_EXPERTS
    w = jnp.concatenate([w_grp, w_exp, jnp.zeros((d, pad), F32)], axis=1)
    b = jnp.concatenate([b_grp, b_exp, jnp.zeros((pad,), F32)])[None, :]
    return w, b


def kernel(x, c, ada_w, ada_b, norm_mix_w, norm_ffn_w, even_w_in, even_w_out, s5_a_re, s5_a_im, s5_log_dt, s5_b_re, s5_b_im, s5_c_re, s5_c_im, s5_d, s5_w_glu, odd_w_in, odd_w_out, rw_mu, rw_w0, rw_w2, rw_a0, rw_a2, rw_g2, rw_k_k, rw_k_a, rw_r_k, rw_ln_w, rw_ln_b, m2_conv_w, m2_conv_b, m2_dt_bias, m2_a_log, m2_d, m2_norm_w, moe_w_grp, moe_b_grp, moe_w_exp, moe_b_exp, moe_w_gate, moe_w_up, moe_w_down, final_norm_w):
    bsz, seq, d = x.shape
    n_tok = bsz * seq
    tm = 512
    xt = x.reshape(n_tok, d)
    ada = _ada_params(c, ada_w, ada_b)
    mods = [[ada[i, :, j * d:(j + 1) * d].reshape(bsz, 1, d) for j in range(6)] for i in range(2)]

    sh1, sc1, g1, sh2, sc2, g2 = mods[0]
    u, q, k, v = _in_proj(xt, norm_mix_w[0][None, :], sh1, sc1, even_w_in[0].astype(BF16),
                          ((0, 512), (512, 1024), (1024, 1536), (1536, 2048)), (F32, BF16, BF16, BF16), seq, tm)
    tables = _s5_tables(s5_a_re[0], s5_a_im[0], s5_log_dt[0], s5_b_re[0], s5_b_im[0], s5_c_re[0], s5_c_im[0], s5_d[0])
    y_s5 = _s5_scan(u, tables, bsz, seq)
    y_sb = _sb_attention(q, k, v, bsz, seq)
    wr, br = _router_weights(moe_w_grp[0], moe_b_grp[0], moe_w_exp[0], moe_b_exp[0])
    x1, h2, logits = _mix_out(y_s5, y_sb, xt, g1, s5_w_glu[0].astype(BF16), even_w_out[0].astype(BF16),
                              norm_ffn_w[0][None, :], sh2, sc2, wr, br, seq, tm)
    yy, gates = _moe(h2, logits, moe_w_gate, moe_w_up, moe_w_down, 0)

    gate_prev = g2
    sh1, sc1, g1, sh2, sc2, g2 = mods[1]
    w_in = jnp.concatenate([odd_w_in[0], jnp.zeros((d, M2_IN_PAD - M2_IN), F32)], axis=1).astype(BF16)
    x2, p_rw, p_m2 = _in_proj(x1, norm_mix_w[1][None, :], sh1, sc1, w_in,
                              ((0, RW_IN), (RW_IN, RW_IN + M2_IN_PAD)), (BF16, BF16), seq, tm,
                              add=(yy, gates, gate_prev))
    pv = jnp.stack([rw_w0[0], rw_a0[0], rw_k_k[0], rw_k_a[0], rw_r_k[0].reshape(-1), rw_ln_w[0], rw_ln_b[0],
                    jnp.zeros((RW_WIDTH,), F32)])
    zl = jnp.zeros((64, RW_WIDTH), F32)
    wwa = jnp.concatenate([jnp.concatenate([rw_w2[0], zl], axis=1),
                           jnp.concatenate([zl, rw_a2[0]], axis=1)], axis=0).astype(BF16)
    y_rw = _rwkv7(p_rw, rw_mu[0][None, :], pv, wwa, rw_g2[0].astype(BF16), bsz, seq)
    hv = jnp.zeros((8, LANES), F32)
    hv = hv.at[0, :M2_HEADS].set(m2_dt_bias[0]).at[1, :M2_HEADS].set(-jnp.exp(m2_a_log[0]))
    y_m2 = _ssd(p_m2, m2_conv_w[0], m2_conv_b[0][None, :], hv, jnp.repeat(m2_d[0], 64)[None, :],
                m2_norm_w[0][None, :], bsz, seq)
    wr, br = _router_weights(moe_w_grp[1], moe_b_grp[1], moe_w_exp[1], moe_b_exp[1])
    x3, h2, logits = _mix_out(y_rw, y_m2, x2, g1, None, odd_w_out[0].astype(BF16),
                              norm_ffn_w[1][None, :], sh2, sc2, wr, br, seq, tm)
    yy, gates = _moe(h2, logits, moe_w_gate, moe_w_up, moe_w_down, 1)
    out = _final(x3, yy, gates, g2, final_norm_w[None, :], seq, tm)
    return out.reshape(bsz, seq, d)
```

```python
import functools
import math

import jax
import jax.numpy as jnp
from jax import lax
from jax.experimental import pallas as pl
from jax.experimental.pallas import tpu as pltpu

F32 = jnp.float32
BF16 = jnp.bfloat16

D_MODEL = 1024
RMS_EPS = 1e-6
LANES = 128
VMEM_LIMIT = 56 * 1024 * 1024

S5_WIDTH = 512
S5_P = 16
S5_G = 32
S5_N = 64
S5_L = 16
S5_GB = 8
SB_WIDTH = 512
SB_HEAD = 64
SB_T = 256
SB_FADE = 160.0

RW_WIDTH = 512
RW_HEAD = 64
RW_HEADS = 8
RW_IN = 1792
RW_LN_EPS = 64e-5
RW_L = 64
RW_TT = 256

M2_INNER = 512
M2_HEADS = 8
M2_STATE = 128
M2_CONV = 4
M2_CONV_DIM = 1024
M2_IN = 1544
M2_IN_PAD = 1664
M2_L = 256

MOE_GROUPS = 4
MOE_PER_GROUP = 8
MOE_EXPERTS = 32
MOE_TOPK = 2
MOE_HIDDEN = 512
MOE_BLK = 512


def _cparams(*sem):
    return pltpu.CompilerParams(dimension_semantics=sem, vmem_limit_bytes=VMEM_LIMIT)


def _bdot(a, b):
    return jnp.dot(a.astype(BF16), b.astype(BF16), preferred_element_type=F32)


def _bdot_nt(a, b):
    return lax.dot_general(a.astype(BF16), b.astype(BF16), (((1,), (1,)), ((), ())),
                           preferred_element_type=F32)


def _bdot_tn(a, b):
    return lax.dot_general(a.astype(BF16), b.astype(BF16), (((0,), (0,)), ((), ())),
                           preferred_element_type=F32)


def _split2(a):
    hi = a.astype(BF16)
    lo = (a - hi.astype(F32)).astype(BF16)
    return hi, lo


def _dot3(a, b):
    ah, al = _split2(a)
    bh, bl = _split2(b)
    d = functools.partial(jnp.dot, preferred_element_type=F32)
    return d(ah, bh) + d(ah, bl) + d(al, bh)


def _dot_exact_rhs(a, b01):
    ah, al = _split2(a)
    d = functools.partial(jnp.dot, preferred_element_type=F32)
    return d(ah, b01) + d(al, b01)


def _dot_exact_lhs(a01, b):
    bh, bl = _split2(b)
    d = functools.partial(jnp.dot, preferred_element_type=F32)
    return d(a01, bh) + d(a01, bl)


def _sigmoid(x):
    return 1.0 / (1.0 + jnp.exp(-x))


def _softplus(x):
    return jnp.maximum(x, 0.0) + jnp.log(1.0 + jnp.exp(-jnp.abs(x)))


def _silu(x):
    return x * _sigmoid(x)


def _gelu_tanh(x):
    c = math.sqrt(2.0 / math.pi)
    return 0.5 * x * (1.0 + jnp.tanh(c * (x + 0.044715 * (x * x * x))))


def _rms_mod(x, nw, shift, scale):
    ms = jnp.mean(x * x, axis=-1, keepdims=True)
    return (x * lax.rsqrt(ms + RMS_EPS)) * nw * (1.0 + scale) + shift


def _ada_kernel(c_ref, w_ref, b_ref, o_ref):
    cond = _silu(c_ref[...])
    o_ref[0] = _dot3(cond, w_ref[0]) + b_ref[0]


def _ada_params(c, ada_w, ada_b):
    depth, d, n = ada_w.shape
    bsz = c.shape[0]
    tn = 1024
    return pl.pallas_call(
        _ada_kernel,
        grid=(depth, n // tn),
        in_specs=[pl.BlockSpec((bsz, d), lambda i, j: (0, 0)),
                  pl.BlockSpec((1, d, tn), lambda i, j: (i, 0, j)),
                  pl.BlockSpec((1, 1, tn), lambda i, j: (i, 0, j))],
        out_specs=pl.BlockSpec((1, bsz, tn), lambda i, j: (i, 0, j)),
        out_shape=jax.ShapeDtypeStruct((depth, bsz, n), F32),
        compiler_params=_cparams("arbitrary", "arbitrary"),
        name="ada_params",
    )(c, ada_w, ada_b.reshape(depth, 1, n))


def _in_proj_kernel(*refs, has_add, splits):
    if has_add:
        x_ref, ya_ref, yb_ref, g_ref, gate_ref, nw_ref, sh_ref, sc_ref, w_ref = refs[:9]
        outs = refs[9:]
        x = x_ref[...]
        g = g_ref[...]
        moe = g[:, 4:5] * ya_ref[...].astype(F32) + g[:, 5:6] * yb_ref[...].astype(F32)
        x = x + gate_ref[0] * moe
        outs[0][...] = x
        outs = outs[1:]
    else:
        x_ref, nw_ref, sh_ref, sc_ref, w_ref = refs[:5]
        outs = refs[5:]
        x = x_ref[...]
    h = _rms_mod(x, nw_ref[...], sh_ref[0], sc_ref[0]).astype(BF16)
    for o_ref, (c0, c1) in zip(outs, splits):
        o_ref[...] = jnp.dot(h, w_ref[:, c0:c1], preferred_element_type=F32).astype(o_ref.dtype)


def _in_proj(x, nw, shift, scale, w, splits, dtypes, seq, tm, add=None):
    n_tok, d = x.shape
    tpb = seq // tm
    row = lambda i: (i, 0)
    bat = lambda i: (i // tpb, 0, 0)
    fix = lambda i: (0, 0)
    in_specs = [pl.BlockSpec((tm, d), row)]
    args = [x]
    out_shape = []
    out_specs = []
    if add is not None:
        yy, g, gate = add
        in_specs += [pl.BlockSpec((tm, d), row), pl.BlockSpec((tm, d), lambda i: (i + n_tok // tm, 0)),
                     pl.BlockSpec((tm, g.shape[1]), row), pl.BlockSpec((1, 1, d), bat)]
        args += [yy, yy, g, gate]
        out_shape.append(jax.ShapeDtypeStruct((n_tok, d), F32))
        out_specs.append(pl.BlockSpec((tm, d), row))
    in_specs += [pl.BlockSpec((1, d), fix), pl.BlockSpec((1, 1, d), bat), pl.BlockSpec((1, 1, d), bat),
                 pl.BlockSpec(w.shape, fix)]
    args += [nw, shift, scale, w]
    for (c0, c1), dt in zip(splits, dtypes):
        out_shape.append(jax.ShapeDtypeStruct((n_tok, c1 - c0), dt))
        out_specs.append(pl.BlockSpec((tm, c1 - c0), row))
    return pl.pallas_call(
        functools.partial(_in_proj_kernel, has_add=add is not None, splits=splits),
        grid=(n_tok // tm,),
        in_specs=in_specs, out_specs=out_specs, out_shape=out_shape,
        compiler_params=_cparams("arbitrary"),
        name="in_proj",
    )(*args)


def _s5_tables(a_re, a_im, log_dt, b_re, b_im, c_re, c_im, d_skip):
    hp = lax.Precision.HIGHEST
    L = S5_L
    a_re = jnp.minimum(a_re.astype(F32), -1e-4)
    a_im = a_im.astype(F32)
    dt = jnp.exp(log_dt.astype(F32))[:, None]
    mag = jnp.exp(dt * a_re)
    abar_re, abar_im = mag * jnp.cos(dt * a_im), mag * jnp.sin(dt * a_im)
    den = a_re * a_re + a_im * a_im
    num_re, num_im = abar_re - 1.0, abar_im
    coef_re = (num_re * a_re + num_im * a_im) / den
    coef_im = (num_im * a_re - num_re * a_im) / den
    b_re = b_re.astype(F32)
    b_im = b_im.astype(F32)
    bb_re = coef_re[..., None] * b_re - coef_im[..., None] * b_im
    bb_im = coef_re[..., None] * b_im + coef_im[..., None] * b_re
    c_re = c_re.astype(F32)
    c_im = c_im.astype(F32)
    tau = jnp.arange(L + 1, dtype=F32)[None, :, None]
    pmag = jnp.exp(tau * (dt * a_re)[:, None, :])
    pw_re = pmag * jnp.cos(tau * (dt * a_im)[:, None, :])
    pw_im = pmag * jnp.sin(tau * (dt * a_im)[:, None, :])
    cl_re = c_re[:, None] * pw_re[:, :, None, :] - c_im[:, None] * pw_im[:, :, None, :]
    cl_im = c_re[:, None] * pw_im[:, :, None, :] + c_im[:, None] * pw_re[:, :, None, :]
    taps = (jnp.einsum('gtpn,gnq->gtpq', cl_re[:, :L], bb_re, precision=hp)
            - jnp.einsum('gtpn,gnq->gtpq', cl_im[:, :L], bb_im, precision=hp))
    nb = S5_G // S5_GB

    def compact(x):
        return x.reshape(nb, S5_GB, L, S5_P, 2 * S5_N).transpose(0, 2, 1, 3, 4).reshape(nb, L, LANES, 2 * S5_N)

    eye = jnp.eye(S5_GB, dtype=F32)
    dlag = jnp.einsum('bgtpq,gh->btgqhp', taps.reshape(nb, S5_GB, L, S5_P, S5_P), eye)
    dlag = dlag.reshape(nb, L, LANES, LANES)
    rev = L - 1 - jnp.arange(L)
    e_re = pw_re[:, rev][:, :, :, None] * bb_re[:, None] - pw_im[:, rev][:, :, :, None] * bb_im[:, None]
    e_im = pw_re[:, rev][:, :, :, None] * bb_im[:, None] + pw_im[:, rev][:, :, :, None] * bb_re[:, None]
    e_re = e_re.transpose(0, 1, 3, 2)
    e_im = e_im.transpose(0, 1, 3, 2)
    emap = jnp.stack([compact(jnp.concatenate([e_re, e_im], axis=-1)),
                      compact(jnp.concatenate([e_im, e_re], axis=-1))], axis=1)
    q_c = jnp.stack([cl_re[:, 1:L + 1], -cl_im[:, 1:L + 1]], axis=1).reshape(nb, S5_GB, 2, L, S5_P, S5_N)
    qmap = jnp.einsum('bgctpn,gh->bcgnthp', q_c, eye).reshape(nb, 2 * S5_GB * S5_N, L * LANES)
    lr = pw_re[:, L].reshape(nb, 1, S5_GB * S5_N)
    li = pw_im[:, L].reshape(nb, 1, S5_GB * S5_N)
    lam_a = jnp.concatenate([lr, lr], axis=-1)
    lam_b = jnp.concatenate([-li, li], axis=-1)
    dvec = jnp.tile(d_skip.astype(F32).reshape(nb, 1, LANES), (1, 1, L))
    return dlag.astype(BF16), emap.astype(BF16), qmap.astype(BF16), lam_a, lam_b, dvec


def _s5_kernel(u_ref, dlag_ref, emap_ref, qmap_ref, la_ref, lb_ref, dv_ref, o_ref, toep_ref, esp_ref, h_ref, *, n_rows):
    L = S5_L
    ns = 2 * S5_GB * S5_N

    @pl.when(pl.program_id(1) == 0)
    def _():
        toep_ref[...] = jnp.zeros_like(toep_ref)
        esp_ref[...] = jnp.zeros_like(esp_ref)
        for s in range(L):
            for t in range(s, L):
                toep_ref[s * LANES:(s + 1) * LANES, t * LANES:(t + 1) * LANES] = dlag_ref[0, t - s]
            for g in range(S5_GB):
                rows = slice(s * LANES + g * S5_P, s * LANES + (g + 1) * S5_P)
                src = slice(g * S5_P, (g + 1) * S5_P)
                half = slice((g % 2) * S5_N, (g % 2 + 1) * S5_N)
                re_at = g * S5_N
                im_at = ns // 2 + g * S5_N
                esp_ref[rows, re_at:re_at + S5_N] = emap_ref[0, g % 2, s, src, half]
                esp_ref[rows, im_at:im_at + S5_N] = emap_ref[0, 1 - g % 2, s, src, half]

    up = jnp.concatenate([u_ref[pl.ds(s, n_rows, stride=L), :] for s in range(L)], axis=1)
    ub = up.astype(BF16)
    h_ref[...] = jnp.dot(ub, esp_ref[...], preferred_element_type=F32)
    la = la_ref[0]
    lb = lb_ref[0]

    def step(c8, h):
        rows = pl.ds(pl.multiple_of(c8 * 8, 8), 8)
        e = h_ref[rows, :]
        ent = []
        for j in range(8):
            ent.append(h)
            h = h * la + pltpu.roll(h, ns // 2, axis=1) * lb + e[j:j + 1, :]
        h_ref[rows, :] = jnp.concatenate(ent, axis=0)
        return h

    lax.fori_loop(0, n_rows // 8, step, jnp.zeros((1, ns), F32))
    hi, lo = _split2(h_ref[...])
    qm = qmap_ref[0]
    y_off = jnp.dot(hi, qm, preferred_element_type=F32) + jnp.dot(lo, qm, preferred_element_type=F32)
    wide = 2 * LANES
    for j in range(L * LANES // wide):
        k_end = (j + 1) * wide
        cols = slice(j * wide, k_end)
        y = jnp.dot(ub[:, :k_end], toep_ref[:k_end, cols], preferred_element_type=F32)
        y = y + y_off[:, cols] + up[:, cols] * dv_ref[0, :, cols]
        for t in range(2 * j, 2 * j + 2):
            o_ref[pl.ds(t, n_rows, stride=L), :] = y[:, (t - 2 * j) * LANES:(t - 2 * j + 1) * LANES]


def _s5_scan(u, tables, bsz, seq):
    dlag, emap, qmap, lam_a, lam_b, dvec = tables
    L = S5_L
    n_rows = seq // L
    nb = S5_WIDTH // LANES
    ns = 2 * S5_GB * S5_N
    blk3 = lambda j, b: (j, 0, 0)
    blk4 = lambda j, b: (j, 0, 0, 0)
    return pl.pallas_call(
        functools.partial(_s5_kernel, n_rows=n_rows),
        grid=(nb, bsz),
        in_specs=[pl.BlockSpec((seq, LANES), lambda j, b: (b, j)),
                  pl.BlockSpec((1, L, LANES, LANES), blk4),
                  pl.BlockSpec((1, 2, L, LANES, 2 * S5_N), lambda j, b: (j, 0, 0, 0, 0)),
                  pl.BlockSpec((1, ns, L * LANES), blk3),
                  pl.BlockSpec((1, 1, ns), blk3), pl.BlockSpec((1, 1, ns), blk3), pl.BlockSpec((1, 1, L * LANES), blk3)],
        out_specs=pl.BlockSpec((seq, LANES), lambda j, b: (b, j)),
        out_shape=jax.ShapeDtypeStruct((bsz * seq, S5_WIDTH), F32),
        scratch_shapes=[pltpu.VMEM((L * LANES, L * LANES), BF16), pltpu.VMEM((L * LANES, ns), BF16),
                        pltpu.VMEM((n_rows, ns), F32)],
        compiler_params=_cparams("arbitrary", "arbitrary"),
        name="s5_scan",
    )(u, dlag, emap, qmap, lam_a, lam_b, dvec)


def _sb_kernel(q_ref, k_ref, v_ref, tri_ref, o_ref, qh_ref, hl_ref, z_ref, w_ref, acc_ref, car_ref, *, t):
    qb = pl.program_id(1)
    heads = SB_WIDTH // SB_HEAD
    lane = lax.broadcasted_iota(jnp.int32, (t, LANES), 1)
    low = lane < SB_HEAD
    for p in range(heads // 2):
        q = (q_ref[:, p * LANES:(p + 1) * LANES].astype(F32) * (0.125 * math.log2(math.e))).astype(BF16)
        zero = jnp.zeros_like(q)
        qh_ref[2 * p] = jnp.where(low, q, zero)
        qh_ref[2 * p + 1] = jnp.where(low, zero, q)
    acc_ref[...] = jnp.zeros_like(acc_ref)
    car_ref[...] = jnp.zeros_like(car_ref)

    def scores(kb, h, slot, diagonal):
        rows = pl.ds(pl.multiple_of(kb * t, t), t)
        kblk = k_ref[rows, (h // 2) * LANES:(h // 2 + 1) * LANES]
        z = lax.dot_general(qh_ref[h], kblk, (((1,), (1,)), ((), ())), preferred_element_type=F32)
        sp = jnp.maximum(z, 0.0) + jnp.log2(1.0 + jnp.exp2(-jnp.abs(z)))
        if diagonal:
            mask = lax.broadcasted_iota(jnp.int32, (t, t), 0) > lax.broadcasted_iota(jnp.int32, (t, t), 1)
            sp = jnp.where(mask, sp, 0.0)
            z = jnp.where(mask, z, -1e30)
        hi = sp.astype(BF16)
        hl_ref[slot, :, :t] = hi
        hl_ref[slot, :, t:] = (sp - hi.astype(F32)).astype(BF16)
        z_ref[slot] = z

    def weights(h, src, dst):
        cs = jnp.dot(hl_ref[src], tri_ref[...], preferred_element_type=F32)
        car = car_ref[h]
        s = z_ref[src] - cs - jnp.concatenate([car, car], axis=1)
        w_ref[dst] = jnp.exp2(s).astype(BF16)
        car_ref[h] = car + jnp.broadcast_to(cs[:, 0:1], car.shape)

    def values(kb, h, slot):
        rows = pl.ds(pl.multiple_of(kb * t, t), t)
        vblk = v_ref[rows, (h // 2) * LANES:(h // 2 + 1) * LANES]
        acc_ref[h] += jnp.dot(w_ref[slot], vblk, preferred_element_type=F32)

    def key_tile(kb, diagonal):
        for j in range(heads + 2):
            if j < heads:
                scores(kb, j, j % 2, diagonal)
            if 1 <= j <= heads:
                weights(j - 1, (j - 1) % 2, j % 2)
            if j >= 2:
                values(kb, j - 2, (j - 1) % 2)

    def faded():
        return jnp.min(car_ref[...]) >= SB_FADE

    key_tile(qb, True)

    def more(state):
        kb, _ = state
        key_tile(kb, False)
        return kb - 1, faded()

    lax.while_loop(lambda s: (s[0] >= 0) & jnp.logical_not(s[1]), more, (qb - 1, faded()))
    for p in range(heads // 2):
        o_ref[:, p * LANES:(p + 1) * LANES] = jnp.where(low, acc_ref[2 * p], acc_ref[2 * p + 1]).astype(o_ref.dtype)


def _sb_attention(q, k, v, bsz, seq):
    t = SB_T
    nq = seq // t
    heads = SB_WIDTH // SB_HEAD
    tri = (jnp.arange(t)[:, None] >= jnp.arange(t)[None, :]).astype(BF16)
    tri2 = jnp.concatenate([tri, tri], axis=0)
    return pl.pallas_call(
        functools.partial(_sb_kernel, t=t),
        grid=(bsz, nq),
        in_specs=[pl.BlockSpec((t, SB_WIDTH), lambda b, i: (b * nq + i, 0)),
                  pl.BlockSpec((seq, SB_WIDTH), lambda b, i: (b, 0)),
                  pl.BlockSpec((seq, SB_WIDTH), lambda b, i: (b, 0)),
                  pl.BlockSpec((2 * t, t), lambda b, i: (0, 0))],
        out_specs=pl.BlockSpec((t, SB_WIDTH), lambda b, i: (b * nq + i, 0)),
        out_shape=jax.ShapeDtypeStruct((bsz * seq, SB_WIDTH), BF16),
        scratch_shapes=[pltpu.VMEM((heads, t, LANES), BF16), pltpu.VMEM((2, t, 2 * t), BF16),
                        pltpu.VMEM((2, t, t), F32), pltpu.VMEM((2, t, t), BF16),
                        pltpu.VMEM((heads, t, LANES), F32), pltpu.VMEM((heads, t, LANES), F32)],
        compiler_params=_cparams("arbitrary", "arbitrary"),
        name="sb_attention",
    )(q, k, v, tri2)


def _mix_out_kernel(*refs, glu):
    if glu:
        ya_ref, yb_ref, x_ref, gate_ref, wglu_ref, wo_ref, nw_ref, sh_ref, sc_ref, wr_ref, br_ref = refs[:11]
        outs = refs[11:]
        a = _gelu_tanh(ya_ref[...].astype(F32)).astype(BF16)
        g = jnp.dot(a, wglu_ref[...], preferred_element_type=F32)
        half = g.shape[1] // 2
        ya = (g[:, :half] * _sigmoid(g[:, half:])).astype(BF16)
    else:
        ya_ref, yb_ref, x_ref, gate_ref, wo_ref, nw_ref, sh_ref, sc_ref, wr_ref, br_ref = refs[:10]
        outs = refs[10:]
        ya = ya_ref[...]
    x1_ref, h2_ref, lg_ref = outs
    ka = ya.shape[1]
    mix = (jnp.dot(ya, wo_ref[:ka, :], preferred_element_type=F32)
           + jnp.dot(yb_ref[...], wo_ref[ka:, :], preferred_element_type=F32))
    x1 = x_ref[...] + gate_ref[0] * mix
    x1_ref[...] = x1
    h2 = _rms_mod(x1, nw_ref[...], sh_ref[0], sc_ref[0])
    h2_ref[...] = h2.astype(BF16)
    lg_ref[...] = _dot3(h2, wr_ref[...]) + br_ref[...]


def _mix_out(ya, yb, x, gate, w_glu, w_out, nw, shift, scale, w_router, b_router, seq, tm):
    n_tok, d = x.shape
    tpb = seq // tm
    row = lambda i: (i, 0)
    bat = lambda i: (i // tpb, 0, 0)
    fix = lambda i: (0, 0)
    in_specs = [pl.BlockSpec((tm, ya.shape[1]), row), pl.BlockSpec((tm, yb.shape[1]), row),
                pl.BlockSpec((tm, d), row), pl.BlockSpec((1, 1, d), bat)]
    args = [ya, yb, x, gate]
    if w_glu is not None:
        in_specs.append(pl.BlockSpec(w_glu.shape, fix))
        args.append(w_glu)
    in_specs += [pl.BlockSpec(w_out.shape, fix), pl.BlockSpec((1, d), fix), pl.BlockSpec((1, 1, d), bat),
                 pl.BlockSpec((1, 1, d), bat), pl.BlockSpec(w_router.shape, fix), pl.BlockSpec(b_router.shape, fix)]
    args += [w_out, nw, shift, scale, w_router, b_router]
    return pl.pallas_call(
        functools.partial(_mix_out_kernel, glu=w_glu is not None),
        grid=(n_tok // tm,),
        in_specs=in_specs,
        out_specs=[pl.BlockSpec((tm, d), row), pl.BlockSpec((tm, d), row), pl.BlockSpec((tm, LANES), row)],
        out_shape=[jax.ShapeDtypeStruct((n_tok, d), F32), jax.ShapeDtypeStruct((n_tok, d), BF16),
                   jax.ShapeDtypeStruct((n_tok, LANES), F32)],
        compiler_params=_cparams("arbitrary"),
        name="mix_out",
    )(*args)


def _route_kernel(lg_ref, tri_ref, o_ref, ot_ref, cnt_ref, run_ref):
    @pl.when(pl.program_id(0) == 0)
    def _():
        run_ref[...] = jnp.zeros_like(run_ref)

    lg = lg_ref[...]
    lane = lax.broadcasted_iota(jnp.int32, lg.shape, 1)
    neg = jnp.float32(-jnp.inf)
    far = jnp.int32(LANES)

    def top(x):
        m = jnp.max(x, axis=1, keepdims=True)
        return m, jnp.min(jnp.where(x == m, lane, far), axis=1, keepdims=True)

    is_grp = lane < MOE_GROUPS
    gmax, gidx = top(jnp.where(is_grp, lg, neg))
    grp_p = 1.0 / jnp.sum(jnp.where(is_grp, jnp.exp(lg - gmax), 0.0), axis=1, keepdims=True)
    first = MOE_GROUPS + MOE_PER_GROUP * gidx
    el = jnp.where((lane >= first) & (lane < first + MOE_PER_GROUP), lg, neg)
    l0, i0 = top(el)
    l1, i1 = top(jnp.where(lane == i0, neg, el))
    e1 = jnp.exp(l1 - l0)
    g0 = grp_p / (1.0 + e1)
    g1 = grp_p * e1 / (1.0 + e1)
    pick0 = lane == i0
    pick1 = lane == i1
    onehot = jnp.where(pick0 | pick1, 1.0, 0.0)
    before = jnp.dot(tri_ref[...], onehot.astype(BF16), preferred_element_type=F32) + run_ref[...]
    r0 = jnp.sum(jnp.where(pick0, before, 0.0), axis=1, keepdims=True)
    r1 = jnp.sum(jnp.where(pick1, before, 0.0), axis=1, keepdims=True)
    run = run_ref[...] + jnp.sum(onehot, axis=0, keepdims=True)
    run_ref[...] = run
    cnt_ref[...] = run
    cols = ((i0 - MOE_GROUPS).astype(F32), (i1 - MOE_GROUPS).astype(F32), r0, r1, g0, g1)
    out = jnp.zeros(lg.shape, F32)
    for j, col in enumerate(cols):
        out = jnp.where(lane == j, col, out)
    o_ref[...] = out
    ot_ref[...] = jnp.transpose(out)[0:8, :]


def _route(logits, tm):
    n_tok = logits.shape[0]
    tri = (jnp.arange(tm)[:, None] > jnp.arange(tm)[None, :]).astype(BF16)
    return pl.pallas_call(
        _route_kernel,
        grid=(n_tok // tm,),
        in_specs=[pl.BlockSpec((tm, LANES), lambda i: (i, 0)), pl.BlockSpec((tm, tm), lambda i: (0, 0))],
        out_specs=[pl.BlockSpec((tm, LANES), lambda i: (i, 0)), pl.BlockSpec((8, tm), lambda i: (0, i)),
                   pl.BlockSpec((1, LANES), lambda i: (0, 0))],
        out_shape=[jax.ShapeDtypeStruct((n_tok, LANES), F32), jax.ShapeDtypeStruct((8, n_tok), F32),
                   jax.ShapeDtypeStruct((1, LANES), F32)],
        scratch_shapes=[pltpu.VMEM((1, LANES), F32)],
        compiler_params=_cparams("arbitrary"),
        name="route",
    )(logits, tri)


def _dispatch(fields, counts, n_tok):
    n_assign = n_tok * MOE_TOPK
    n_blocks = (n_assign + MOE_EXPERTS * (MOE_BLK - 1) + MOE_BLK - 1) // MOE_BLK
    n_rows = n_blocks * MOE_BLK
    experts = fields[0:2].astype(jnp.int32)
    rank = fields[2:4].astype(jnp.int32)
    counts = counts[0, MOE_GROUPS:MOE_GROUPS + MOE_EXPERTS].astype(jnp.int32)
    padded = ((counts + MOE_BLK - 1) // MOE_BLK) * MOE_BLK
    pad_end = jnp.cumsum(padded)
    pad_start = pad_end - padded
    dest = rank
    for e in range(MOE_EXPERTS):
        dest = dest + jnp.where(experts == e, pad_start[e], 0)
    blk_start = jnp.arange(n_blocks, dtype=jnp.int32) * MOE_BLK
    blk_expert = jnp.minimum(jnp.sum((pad_end[None, :] <= blk_start[:, None]).astype(jnp.int32), axis=1),
                             MOE_EXPERTS - 1).astype(jnp.int32)
    n_used = (pad_end[-1] // MOE_BLK).astype(jnp.int32).reshape(1)
    n_pad = n_rows - n_assign
    gap_end = jnp.cumsum(padded - counts)
    i = jnp.arange(n_pad, dtype=jnp.int32)
    owner = jnp.sum((gap_end[None, :] <= i[:, None]).astype(jnp.int32), axis=1)
    hot = owner[:, None] == jnp.arange(MOE_EXPERTS + 1, dtype=jnp.int32)[None, :]
    first_gap = jnp.concatenate([pad_start + counts, pad_end[-1:]])
    before = jnp.concatenate([gap_end - (padded - counts), gap_end[-1:]])
    pad_row = i + jnp.sum(jnp.where(hot, (first_gap - before)[None, :], 0), axis=1)
    keys = jnp.concatenate([dest.reshape(-1), pad_row])
    vals = jnp.concatenate([jnp.tile(jnp.arange(n_tok, dtype=jnp.int32), MOE_TOPK), pad_row % n_tok])
    _, row_tok = lax.sort((keys, vals), num_keys=1, is_stable=False)
    return dest, row_tok, blk_expert, n_used, n_blocks


def _moe_kernel(be_ref, nu_ref, x_ref, wg_ref, wu_ref, wd_ref, o_ref, wg_s, wu_s, wd_s):
    i = pl.program_id(0)
    e = be_ref[i]
    prev = be_ref[jnp.maximum(i - 1, 0)]

    @pl.when((i == 0) | (e != prev))
    def _():
        wg_s[...] = wg_ref[0, 0].astype(BF16)
        wu_s[...] = wu_ref[0, 0].astype(BF16)
        wd_s[...] = wd_ref[0, 0].astype(BF16)

    @pl.when(i < nu_ref[0])
    def _():
        x = x_ref[...]
        a = jnp.dot(x, wg_s[...], preferred_element_type=F32)
        b = jnp.dot(x, wu_s[...], preferred_element_type=F32)
        hid = (_silu(a) * b).astype(BF16)
        o_ref[...] = jnp.dot(hid, wd_s[...], preferred_element_type=F32).astype(o_ref.dtype)

    @pl.when(i >= nu_ref[0])
    def _():
        o_ref[...] = jnp.zeros_like(o_ref)


def _moe_experts(xs, blk_expert, n_used, w_gate, w_up, w_down, layer, n_blocks):
    d = xs.shape[1]
    hid = w_gate.shape[3]
    grid_spec = pltpu.PrefetchScalarGridSpec(
        num_scalar_prefetch=2,
        grid=(n_blocks,),
        in_specs=[pl.BlockSpec((MOE_BLK, d), lambda i, be, nu: (i, 0)),
                  pl.BlockSpec((1, 1, d, hid), lambda i, be, nu: (layer, be[i], 0, 0)),
                  pl.BlockSpec((1, 1, d, hid), lambda i, be, nu: (layer, be[i], 0, 0)),
                  pl.BlockSpec((1, 1, hid, d), lambda i, be, nu: (layer, be[i], 0, 0))],
        out_specs=pl.BlockSpec((MOE_BLK, d), lambda i, be, nu: (i, 0)),
        scratch_shapes=[pltpu.VMEM((d, hid), BF16), pltpu.VMEM((d, hid), BF16), pltpu.VMEM((hid, d), BF16)],
    )
    return pl.pallas_call(
        _moe_kernel,
        grid_spec=grid_spec,
        out_shape=jax.ShapeDtypeStruct((n_blocks * MOE_BLK, d), BF16),
        compiler_params=_cparams("arbitrary"),
        name="moe_experts",
    )(blk_expert, n_used, xs, w_gate, w_up, w_down)


def _moe(h2, logits, w_gate, w_up, w_down, layer):
    n_tok, d = h2.shape
    routed, fields, counts = _route(logits, 512)
    dest, row_tok, blk_expert, n_used, n_blocks = _dispatch(fields, counts, n_tok)
    rows = lambda a, idx: a.at[idx].get(mode='promise_in_bounds')
    xs = rows(h2, row_tok)
    ys = _moe_experts(xs, blk_expert, n_used, w_gate, w_up, w_down, layer, n_blocks)
    return rows(ys, dest.reshape(-1)), routed


def _rw_kernel(p_ref, mu_ref, pv_ref, wwa_ref, g2_ref, bd_ref, tri_ref, o_ref, st_ref, last_ref, *, tt):
    i = pl.program_id(1)
    L = RW_L
    W = RW_WIDTH

    @pl.when(i == 0)
    def _():
        st_ref[...] = jnp.zeros_like(st_ref)
        last_ref[...] = jnp.zeros_like(last_ref)

    p = p_ref[...].astype(F32)
    row = lax.broadcasted_iota(jnp.int32, p.shape, 0)
    prev = jnp.where(row == 0, jnp.broadcast_to(last_ref[0:1, :], p.shape), pltpu.roll(p, 1, axis=0))
    last_ref[0:1, :] = p[tt - 1:tt, :]
    p = p + (prev - p) * mu_ref[...]
    r = p[:, 0:W]
    k = p[:, W:2 * W]
    v = p[:, 2 * W:3 * W]
    lw = p[:, 3 * W:3 * W + LANES]
    xg = p[:, 3 * W + LANES:3 * W + 2 * LANES]
    w0, a0, k_k, k_a, r_k, ln_w, ln_b = (pv_ref[j:j + 1, :] for j in range(7))
    lane = lax.broadcasted_iota(jnp.int32, lw.shape, 1)
    wa = _bdot(jnp.where(lane < 64, jnp.tanh(lw), lw), wwa_ref[...])
    w = -_softplus(-(w0 + wa[:, :W])) - 0.5
    logd = -jnp.exp(w)
    lr = _sigmoid(a0 + wa[:, W:])
    gate = _bdot(_sigmoid(xg), g2_ref[...])
    bd = bd_ref[...]
    kk = k * k_k
    kk = kk / jnp.maximum(jnp.sqrt(_bdot(kk * kk, bd)), 1e-12)
    k = k * (1.0 + (lr - 1.0) * k_a)
    av = -kk
    bv = kk * lr

    tri = tri_ref[...]
    lane_p = lax.broadcasted_iota(jnp.int32, (L, LANES), 1)
    m0 = lane_p < RW_HEAD
    ri = lax.broadcasted_iota(jnp.int32, (2 * L, 2 * L), 0)
    ci = lax.broadcasted_iota(jnp.int32, (2 * L, 2 * L), 1)
    same = (ri // L) == (ci // L)
    mask_s = same & (ci < ri)
    mask_i = same & (ci <= ri)
    mask_s2 = jnp.concatenate([mask_s, mask_s], axis=1)
    mask_i2 = jnp.concatenate([mask_i, mask_i], axis=1)
    eye = (ri == ci).astype(F32)

    def stack(x):
        return jnp.concatenate([jnp.where(m0, x, 0.0), jnp.where(m0, 0.0, x)], axis=0)

    n_chunks = tt // L
    n_pairs = RW_HEADS // 2
    aa, rr, vv, bk2, bkg, gam = [], [], [], [], [], []
    for c in range(n_chunks):
        rows = slice(c * L, (c + 1) * L)
        ld = logd[rows]
        cs = _dot_exact_lhs(tri, ld)
        total = cs[L - 1:L, :]
        e_out = jnp.exp(-cs)
        e_end = jnp.exp(total - cs)
        at = av[rows] * jnp.exp(cs - ld)
        rt = r[rows] * jnp.exp(cs)
        bt = bv[rows] * e_out
        kt = k[rows] * e_out
        bg = bv[rows] * e_end
        kg = k[rows] * e_end
        g_all = jnp.exp(total)
        for hp in range(n_pairs):
            ls = slice(hp * LANES, (hp + 1) * LANES)
            aa.append(stack(at[:, ls]).astype(BF16))
            rr.append(stack(rt[:, ls]))
            vv.append(stack(v[rows, ls]).astype(BF16))
            bk2.append(jnp.concatenate([bt[:, ls], bt[:, ls], kt[:, ls], kt[:, ls]], axis=0).astype(BF16))
            bkg.append(jnp.concatenate([stack(bg[:, ls]), stack(kg[:, ls])], axis=0).astype(BF16))
            gam.append(g_all[:, ls])
    n_sys = len(aa)
    rng = range(n_sys)
    n_a = [jnp.where(mask_s2, _bdot_nt(aa[i], bk2[i]), 0.0) for i in rng]
    n_r = [jnp.where(mask_i2, _bdot_nt(rr[i], bk2[i]), 0.0).astype(BF16) for i in rng]
    n_ab = [n_a[i][:, :2 * L] for i in rng]
    n_ak = [n_a[i][:, 2 * L:].astype(BF16) for i in rng]
    n_rb = [n_r[i][:, :2 * L] for i in rng]
    tinv = [eye + n_ab[i] for i in rng]
    pw = n_ab
    for _ in range(int(math.log2(L)) - 1):
        pw = [_bdot(pw[i], pw[i]) for i in rng]
        tinv = [tinv[i] + _bdot(tinv[i], pw[i]) for i in rng]
    tinv = [tinv[i].astype(BF16) for i in rng]
    akv = [_bdot(n_ak[i], vv[i]).astype(BF16) for i in rng]
    tx = [_bdot(tinv[i], jnp.concatenate([akv[i], aa[i]], axis=1)).astype(BF16) for i in rng]
    tav = [tx[i][:, :LANES] for i in rng]
    ta = [tx[i][:, LANES:] for i in rng]
    uv = [jnp.concatenate([tav[i], vv[i]], axis=0) for i in rng]
    mg = [_bdot_tn(ta[i], bkg[i][:2 * L]).astype(BF16) for i in rng]
    ra = [(rr[i] + _bdot(n_rb[i], ta[i])).astype(BF16) for i in rng]
    y0 = [_bdot(n_r[i], uv[i]) for i in rng]
    s0 = [_bdot_tn(uv[i], bkg[i]) for i in rng]
    for c in range(n_chunks):
        rows = slice(c * L, (c + 1) * L)
        y_parts = []
        for hp in range(n_pairs):
            i = c * n_pairs + hp
            s = st_ref[hp]
            sb = s.astype(BF16)
            yst = _bdot_nt(ra[i], sb) + y0[i]
            y_parts.append(yst[:L] + yst[L:])
            st_ref[hp] = s * gam[i] + jnp.dot(sb, mg[i], preferred_element_type=F32) + s0[i]
        y = jnp.concatenate(y_parts, axis=1)
        mean = _bdot(y, bd) * (1.0 / RW_HEAD)
        yc = y - mean
        var = _bdot(yc * yc, bd) * (1.0 / RW_HEAD)
        yn = yc * lax.rsqrt(var + RW_LN_EPS) * ln_w + ln_b
        bonus = _bdot(r[rows] * k[rows] * r_k, bd) * v[rows]
        o_ref[rows, :] = ((yn + bonus) * gate[rows]).astype(o_ref.dtype)


def _rwkv7(p, mu, pv, wwa, g2, bsz, seq):
    tt = RW_TT
    nt = seq // tt
    L = RW_L
    hid = jnp.arange(RW_WIDTH) // RW_HEAD
    bd = (hid[:, None] == hid[None, :]).astype(BF16)
    tri = (jnp.arange(L)[:, None] >= jnp.arange(L)[None, :]).astype(BF16)
    fix = lambda b, i: (0, 0)
    return pl.pallas_call(
        functools.partial(_rw_kernel, tt=tt),
        grid=(bsz, nt),
        in_specs=[pl.BlockSpec((tt, RW_IN), lambda b, i: (b * nt + i, 0)),
                  pl.BlockSpec(mu.shape, fix), pl.BlockSpec(pv.shape, fix), pl.BlockSpec(wwa.shape, fix),
                  pl.BlockSpec(g2.shape, fix), pl.BlockSpec(bd.shape, fix), pl.BlockSpec(tri.shape, fix)],
        out_specs=pl.BlockSpec((tt, RW_WIDTH), lambda b, i: (b * nt + i, 0)),
        out_shape=jax.ShapeDtypeStruct((bsz * seq, RW_WIDTH), BF16),
        scratch_shapes=[pltpu.VMEM((RW_HEADS // 2, 2 * RW_HEAD, LANES), F32), pltpu.VMEM((8, RW_IN), F32)],
        compiler_params=_cparams("arbitrary", "arbitrary"),
        name="rwkv7",
    )(p, mu, pv, wwa, g2, bd, tri)


def _ssd_kernel(p_ref, cw_ref, cb_ref, hv_ref, dl_ref, nw_ref, tri_ref, o_ref, st_ref, tail_ref, *, L):
    i = pl.program_id(1)

    @pl.when(i == 0)
    def _():
        st_ref[...] = jnp.zeros_like(st_ref)
        tail_ref[...] = jnp.zeros_like(tail_ref)

    z = p_ref[:, 0:M2_INNER].astype(F32)
    xin = p_ref[:, M2_INNER:M2_INNER + M2_CONV_DIM].astype(F32)
    dt_raw = p_ref[:, M2_INNER + M2_CONV_DIM:M2_INNER + M2_CONV_DIM + LANES].astype(F32)
    tail = tail_ref[...]
    tail_ref[...] = xin[L - 8:L, :]
    row8 = lax.broadcasted_iota(jnp.int32, (8, M2_CONV_DIM), 0)
    conv = xin * cw_ref[M2_CONV - 1:M2_CONV, :] + cb_ref[...]
    for j in range(1, M2_CONV):
        rolled = pltpu.roll(xin, j, axis=0)
        head = jnp.where(row8 < j, pltpu.roll(tail, j, axis=0), rolled[0:8])
        shifted = jnp.concatenate([head, rolled[8:]], axis=0)
        conv = conv + shifted * cw_ref[M2_CONV - 1 - j:M2_CONV - j, :]
    xbc = _silu(conv)
    xs = xbc[:, 0:M2_INNER]
    dt = _softplus(dt_raw + hv_ref[0:1, :])
    adt = dt * hv_ref[1:2, :]
    cs = _dot_exact_lhs(tri_ref[...], adt)
    cs_t = jnp.transpose(cs)
    total = cs[L - 1:L, :]
    dec_in = jnp.exp(cs)
    dec_out = jnp.exp(total - cs)
    dec_all = jnp.exp(total)
    lane = lax.broadcasted_iota(jnp.int32, (L, LANES), 1)
    first = lane < 64
    rowp = lax.broadcasted_iota(jnp.int32, (LANES, M2_STATE), 0) < 64
    li = lax.broadcasted_iota(jnp.int32, (L, L), 0)
    si = lax.broadcasted_iota(jnp.int32, (L, L), 1)
    causal = li >= si
    y_parts = []
    for hp in range(M2_HEADS // 2):
        g = hp // 2
        h0, h1 = 2 * hp, 2 * hp + 1
        bm = xbc[:, M2_INNER + g * M2_STATE:M2_INNER + (g + 1) * M2_STATE]
        cm = xbc[:, M2_INNER + 2 * M2_STATE + g * M2_STATE:M2_INNER + 2 * M2_STATE + (g + 1) * M2_STATE]
        x_p = xs[:, hp * LANES:(hp + 1) * LANES]
        sel = lambda a: jnp.where(first, a[:, h0:h0 + 1], a[:, h1:h1 + 1])
        xd = x_p * sel(dt)
        cb = _bdot_nt(cm, bm)
        yd = []
        for h in (h0, h1):
            lmat = jnp.where(causal, jnp.exp(cs[:, h:h + 1] - cs_t[h:h + 1, :]), 0.0)
            yd.append(_bdot(cb * lmat, xd))
        s = st_ref[hp]
        y_off = _bdot_nt(cm, s) * sel(dec_in)
        y_parts.append(jnp.where(first, yd[0], yd[1]) + y_off)
        s_new = _bdot_tn(xd * sel(dec_out), bm)
        st_ref[hp] = s * jnp.where(rowp, dec_all[:, h0:h0 + 1], dec_all[:, h1:h1 + 1]) + s_new
    y = jnp.concatenate(y_parts, axis=1) + dl_ref[...] * xs
    y = y * _silu(z)
    half = M2_INNER // 2
    outs = []
    for g in range(2):
        yg = y[:, g * half:(g + 1) * half]
        outs.append(yg * lax.rsqrt(jnp.mean(yg * yg, axis=-1, keepdims=True) + RMS_EPS))
    o_ref[...] = (jnp.concatenate(outs, axis=1) * nw_ref[...]).astype(o_ref.dtype)


def _ssd(p, conv_w, conv_b, hv, d_lanes, norm_w, bsz, seq):
    L = M2_L
    nt = seq // L
    tri = (jnp.arange(L)[:, None] >= jnp.arange(L)[None, :]).astype(BF16)
    fix = lambda b, i: (0, 0)
    return pl.pallas_call(
        functools.partial(_ssd_kernel, L=L),
        grid=(bsz, nt),
        in_specs=[pl.BlockSpec((L, M2_IN_PAD), lambda b, i: (b * nt + i, 0)),
                  pl.BlockSpec(conv_w.shape, fix), pl.BlockSpec(conv_b.shape, fix), pl.BlockSpec(hv.shape, fix),
                  pl.BlockSpec(d_lanes.shape, fix), pl.BlockSpec(norm_w.shape, fix), pl.BlockSpec(tri.shape, fix)],
        out_specs=pl.BlockSpec((L, M2_INNER), lambda b, i: (b * nt + i, 0)),
        out_shape=jax.ShapeDtypeStruct((bsz * seq, M2_INNER), BF16),
        scratch_shapes=[pltpu.VMEM((M2_HEADS // 2, LANES, M2_STATE), F32), pltpu.VMEM((8, M2_CONV_DIM), F32)],
        compiler_params=_cparams("arbitrary", "arbitrary"),
        name="ssd",
    )(p, conv_w, conv_b, hv, d_lanes, norm_w, tri)


def _final_kernel(x_ref, ya_ref, yb_ref, g_ref, gate_ref, nw_ref, o_ref):
    g = g_ref[...]
    moe = g[:, 4:5] * ya_ref[...].astype(F32) + g[:, 5:6] * yb_ref[...].astype(F32)
    x = x_ref[...] + gate_ref[0] * moe
    ms = jnp.mean(x * x, axis=-1, keepdims=True)
    o_ref[...] = x * lax.rsqrt(ms + RMS_EPS) * nw_ref[...]


def _final(x, yy, g, gate, nw, seq, tm):
    n_tok, d = x.shape
    tpb = seq // tm
    row = lambda i: (i, 0)
    return pl.pallas_call(
        _final_kernel,
        grid=(n_tok // tm,),
        in_specs=[pl.BlockSpec((tm, d), row), pl.BlockSpec((tm, d), row),
                  pl.BlockSpec((tm, d), lambda i: (i + n_tok // tm, 0)),
                  pl.BlockSpec((tm, g.shape[1]), row), pl.BlockSpec((1, 1, d), lambda i: (i // tpb, 0, 0)),
                  pl.BlockSpec((1, d), lambda i: (0, 0))],
        out_specs=pl.BlockSpec((tm, d), row),
        out_shape=jax.ShapeDtypeStruct((n_tok, d), F32),
        compiler_params=_cparams("arbitrary"),
        name="final_norm",
    )(x, yy, yy, g, gate, nw)


def _router_weights(w_grp, b_grp, w_exp, b_exp):
    d = w_grp.shape[0]
    pad = LANES - MOE_GROUPS - MOE_EXPERTS
    w = jnp.concatenate([w_grp, w_exp, jnp.zeros((d, pad), F32)], axis=1)
    b = jnp.concatenate([b_grp, b_exp, jnp.zeros((pad,), F32)])[None, :]
    return w, b


def kernel(x, c, ada_w, ada_b, norm_mix_w, norm_ffn_w, even_w_in, even_w_out, s5_a_re, s5_a_im, s5_log_dt, s5_b_re, s5_b_im, s5_c_re, s5_c_im, s5_d, s5_w_glu, odd_w_in, odd_w_out, rw_mu, rw_w0, rw_w2, rw_a0, rw_a2, rw_g2, rw_k_k, rw_k_a, rw_r_k, rw_ln_w, rw_ln_b, m2_conv_w, m2_conv_b, m2_dt_bias, m2_a_log, m2_d, m2_norm_w, moe_w_grp, moe_b_grp, moe_w_exp, moe_b_exp, moe_w_gate, moe_w_up, moe_w_down, final_norm_w):
    bsz, seq, d = x.shape
    n_tok = bsz * seq
    tm = 512
    xt = x.reshape(n_tok, d)
    ada = _ada_params(c, ada_w, ada_b)
    mods = [[ada[i, :, j * d:(j + 1) * d].reshape(bsz, 1, d) for j in range(6)] for i in range(2)]

    sh1, sc1, g1, sh2, sc2, g2 = mods[0]
    u, q, k, v = _in_proj(xt, norm_mix_w[0][None, :], sh1, sc1, even_w_in[0].astype(BF16),
                          ((0, 512), (512, 1024), (1024, 1536), (1536, 2048)), (F32, BF16, BF16, BF16), seq, tm)
    tables = _s5_tables(s5_a_re[0], s5_a_im[0], s5_log_dt[0], s5_b_re[0], s5_b_im[0], s5_c_re[0], s5_c_im[0], s5_d[0])
    y_s5 = _s5_scan(u, tables, bsz, seq)
    y_sb = _sb_attention(q, k, v, bsz, seq)
    wr, br = _router_weights(moe_w_grp[0], moe_b_grp[0], moe_w_exp[0], moe_b_exp[0])
    x1, h2, logits = _mix_out(y_s5, y_sb, xt, g1, s5_w_glu[0].astype(BF16), even_w_out[0].astype(BF16),
                              norm_ffn_w[0][None, :], sh2, sc2, wr, br, seq, tm)
    yy, gates = _moe(h2, logits, moe_w_gate, moe_w_up, moe_w_down, 0)

    gate_prev = g2
    sh1, sc1, g1, sh2, sc2, g2 = mods[1]
    w_in = jnp.concatenate([odd_w_in[0], jnp.zeros((d, M2_IN_PAD - M2_IN), F32)], axis=1).astype(BF16)
    x2, p_rw, p_m2 = _in_proj(x1, norm_mix_w[1][None, :], sh1, sc1, w_in,
                              ((0, RW_IN), (RW_IN, RW_IN + M2_IN_PAD)), (BF16, BF16), seq, tm,
                              add=(yy, gates, gate_prev))
    pv = jnp.stack([rw_w0[0], rw_a0[0], rw_k_k[0], rw_k_a[0], rw_r_k[0].reshape(-1), rw_ln_w[0], rw_ln_b[0],
                    jnp.zeros((RW_WIDTH,), F32)])
    zl = jnp.zeros((64, RW_WIDTH), F32)
    wwa = jnp.concatenate([jnp.concatenate([rw_w2[0], zl], axis=1),
                           jnp.concatenate([zl, rw_a2[0]], axis=1)], axis=0).astype(BF16)
    y_rw = _rwkv7(p_rw, rw_mu[0][None, :], pv, wwa, rw_g2[0].astype(BF16), bsz, seq)
    hv = jnp.zeros((8, LANES), F32)
    hv = hv.at[0, :M2_HEADS].set(m2_dt_bias[0]).at[1, :M2_HEADS].set(-jnp.exp(m2_a_log[0]))
    y_m2 = _ssd(p_m2, m2_conv_w[0], m2_conv_b[0][None, :], hv, jnp.repeat(m2_d[0], 64)[None, :],
                m2_norm_w[0][None, :], bsz, seq)
    wr, br = _router_weights(moe_w_grp[1], moe_b_grp[1], moe_w_exp[1], moe_b_exp[1])
    x3, h2, logits = _mix_out(y_rw, y_m2, x2, g1, None, odd_w_out[0].astype(BF16),
                              norm_ffn_w[1][None, :], sh2, sc2, wr, br, seq, tm)
    yy, gates = _moe(h2, logits, moe_w_gate, moe_w_up, moe_w_down, 1)
    out = _final(x3, yy, gates, g2, final_norm_w[None, :], seq, tm)
    return out.reshape(bsz, seq, d)
```

```python
import functools
import math

import jax
import jax.numpy as jnp
from jax import lax
from jax.experimental import pallas as pl
from jax.experimental.pallas import tpu as pltpu

F32 = jnp.float32
BF16 = jnp.bfloat16

D_MODEL = 1024
RMS_EPS = 1e-6
LANES = 128
VMEM_LIMIT = 56 * 1024 * 1024

S5_WIDTH = 512
S5_P = 16
S5_G = 32
S5_N = 64
S5_L = 16
S5_GB = 8
SB_WIDTH = 512
SB_HEAD = 64
SB_T = 256
SB_FADE = 160.0

RW_WIDTH = 512
RW_HEAD = 64
RW_HEADS = 8
RW_IN = 1792
RW_LN_EPS = 64e-5
RW_L = 64
RW_TT = 256

M2_INNER = 512
M2_HEADS = 8
M2_STATE = 128
M2_CONV = 4
M2_CONV_DIM = 1024
M2_IN = 1544
M2_IN_PAD = 1664
M2_L = 256

MOE_GROUPS = 4
MOE_PER_GROUP = 8
MOE_EXPERTS = 32
MOE_TOPK = 2
MOE_HIDDEN = 512
MOE_BLK = 512


def _cparams(*sem):
    return pltpu.CompilerParams(dimension_semantics=sem, vmem_limit_bytes=VMEM_LIMIT)


def _bdot(a, b):
    return jnp.dot(a.astype(BF16), b.astype(BF16), preferred_element_type=F32)


def _bdot_nt(a, b):
    return lax.dot_general(a.astype(BF16), b.astype(BF16), (((1,), (1,)), ((), ())),
                           preferred_element_type=F32)


def _bdot_tn(a, b):
    return lax.dot_general(a.astype(BF16), b.astype(BF16), (((0,), (0,)), ((), ())),
                           preferred_element_type=F32)


def _split2(a):
    hi = a.astype(BF16)
    lo = (a - hi.astype(F32)).astype(BF16)
    return hi, lo


def _dot3(a, b):
    ah, al = _split2(a)
    bh, bl = _split2(b)
    d = functools.partial(jnp.dot, preferred_element_type=F32)
    return d(ah, bh) + d(ah, bl) + d(al, bh)


def _dot_exact_rhs(a, b01):
    ah, al = _split2(a)
    d = functools.partial(jnp.dot, preferred_element_type=F32)
    return d(ah, b01) + d(al, b01)


def _dot_exact_lhs(a01, b):
    bh, bl = _split2(b)
    d = functools.partial(jnp.dot, preferred_element_type=F32)
    return d(a01, bh) + d(a01, bl)


def _sigmoid(x):
    return 1.0 / (1.0 + jnp.exp(-x))


def _softplus(x):
    return jnp.maximum(x, 0.0) + jnp.log(1.0 + jnp.exp(-jnp.abs(x)))


def _silu(x):
    return x * _sigmoid(x)


def _gelu_tanh(x):
    c = math.sqrt(2.0 / math.pi)
    return 0.5 * x * (1.0 + jnp.tanh(c * (x + 0.044715 * (x * x * x))))


def _rms_mod(x, nw, shift, scale):
    ms = jnp.mean(x * x, axis=-1, keepdims=True)
    return (x * lax.rsqrt(ms + RMS_EPS)) * nw * (1.0 + scale) + shift


def _ada_kernel(c_ref, w_ref, b_ref, o_ref):
    cond = _silu(c_ref[...])
    o_ref[0] = _dot3(cond, w_ref[0]) + b_ref[0]


def _ada_params(c, ada_w, ada_b):
    depth, d, n = ada_w.shape
    bsz = c.shape[0]
    tn = 1024
    return pl.pallas_call(
        _ada_kernel,
        grid=(depth, n // tn),
        in_specs=[pl.BlockSpec((bsz, d), lambda i, j: (0, 0)),
                  pl.BlockSpec((1, d, tn), lambda i, j: (i, 0, j)),
                  pl.BlockSpec((1, 1, tn), lambda i, j: (i, 0, j))],
        out_specs=pl.BlockSpec((1, bsz, tn), lambda i, j: (i, 0, j)),
        out_shape=jax.ShapeDtypeStruct((depth, bsz, n), F32),
        compiler_params=_cparams("arbitrary", "arbitrary"),
        name="ada_params",
    )(c, ada_w, ada_b.reshape(depth, 1, n))


def _in_proj_kernel(*refs, has_add, splits):
    if has_add:
        x_ref, ya_ref, yb_ref, g_ref, gate_ref, nw_ref, sh_ref, sc_ref, w_ref = refs[:9]
        outs = refs[9:]
        x = x_ref[...]
        g = g_ref[...]
        moe = g[:, 4:5] * ya_ref[...].astype(F32) + g[:, 5:6] * yb_ref[...].astype(F32)
        x = x + gate_ref[0] * moe
        outs[0][...] = x
        outs = outs[1:]
    else:
        x_ref, nw_ref, sh_ref, sc_ref, w_ref = refs[:5]
        outs = refs[5:]
        x = x_ref[...]
    h = _rms_mod(x, nw_ref[...], sh_ref[0], sc_ref[0]).astype(BF16)
    for o_ref, (c0, c1) in zip(outs, splits):
        o_ref[...] = jnp.dot(h, w_ref[:, c0:c1], preferred_element_type=F32).astype(o_ref.dtype)


def _in_proj(x, nw, shift, scale, w, splits, dtypes, seq, tm, add=None):
    n_tok, d = x.shape
    tpb = seq // tm
    row = lambda i: (i, 0)
    bat = lambda i: (i // tpb, 0, 0)
    fix = lambda i: (0, 0)
    in_specs = [pl.BlockSpec((tm, d), row)]
    args = [x]
    out_shape = []
    out_specs = []
    if add is not None:
        yy, g, gate = add
        in_specs += [pl.BlockSpec((tm, d), row), pl.BlockSpec((tm, d), lambda i: (i + n_tok // tm, 0)),
                     pl.BlockSpec((tm, g.shape[1]), row), pl.BlockSpec((1, 1, d), bat)]
        args += [yy, yy, g, gate]
        out_shape.append(jax.ShapeDtypeStruct((n_tok, d), F32))
        out_specs.append(pl.BlockSpec((tm, d), row))
    in_specs += [pl.BlockSpec((1, d), fix), pl.BlockSpec((1, 1, d), bat), pl.BlockSpec((1, 1, d), bat),
                 pl.BlockSpec(w.shape, fix)]
    args += [nw, shift, scale, w]
    for (c0, c1), dt in zip(splits, dtypes):
        out_shape.append(jax.ShapeDtypeStruct((n_tok, c1 - c0), dt))
        out_specs.append(pl.BlockSpec((tm, c1 - c0), row))
    return pl.pallas_call(
        functools.partial(_in_proj_kernel, has_add=add is not None, splits=splits),
        grid=(n_tok // tm,),
        in_specs=in_specs, out_specs=out_specs, out_shape=out_shape,
        compiler_params=_cparams("arbitrary"),
        name="in_proj",
    )(*args)


def _s5_tables(a_re, a_im, log_dt, b_re, b_im, c_re, c_im, d_skip):
    hp = lax.Precision.HIGHEST
    L = S5_L
    a_re = jnp.minimum(a_re.astype(F32), -1e-4)
    a_im = a_im.astype(F32)
    dt = jnp.exp(log_dt.astype(F32))[:, None]
    mag = jnp.exp(dt * a_re)
    abar_re, abar_im = mag * jnp.cos(dt * a_im), mag * jnp.sin(dt * a_im)
    den = a_re * a_re + a_im * a_im
    num_re, num_im = abar_re - 1.0, abar_im
    coef_re = (num_re * a_re + num_im * a_im) / den
    coef_im = (num_im * a_re - num_re * a_im) / den
    b_re = b_re.astype(F32)
    b_im = b_im.astype(F32)
    bb_re = coef_re[..., None] * b_re - coef_im[..., None] * b_im
    bb_im = coef_re[..., None] * b_im + coef_im[..., None] * b_re
    c_re = c_re.astype(F32)
    c_im = c_im.astype(F32)
    tau = jnp.arange(L + 1, dtype=F32)[None, :, None]
    pmag = jnp.exp(tau * (dt * a_re)[:, None, :])
    pw_re = pmag * jnp.cos(tau * (dt * a_im)[:, None, :])
    pw_im = pmag * jnp.sin(tau * (dt * a_im)[:, None, :])
    cl_re = c_re[:, None] * pw_re[:, :, None, :] - c_im[:, None] * pw_im[:, :, None, :]
    cl_im = c_re[:, None] * pw_im[:, :, None, :] + c_im[:, None] * pw_re[:, :, None, :]
    taps = (jnp.einsum('gtpn,gnq->gtpq', cl_re[:, :L], bb_re, precision=hp)
            - jnp.einsum('gtpn,gnq->gtpq', cl_im[:, :L], bb_im, precision=hp))
    nb = S5_G // S5_GB

    def compact(x):
        return x.reshape(nb, S5_GB, L, S5_P, 2 * S5_N).transpose(0, 2, 1, 3, 4).reshape(nb, L, LANES, 2 * S5_N)

    eye = jnp.eye(S5_GB, dtype=F32)
    dlag = jnp.einsum('bgtpq,gh->btgqhp', taps.reshape(nb, S5_GB, L, S5_P, S5_P), eye)
    dlag = dlag.reshape(nb, L, LANES, LANES)
    rev = L - 1 - jnp.arange(L)
    e_re = pw_re[:, rev][:, :, :, None] * bb_re[:, None] - pw_im[:, rev][:, :, :, None] * bb_im[:, None]
    e_im = pw_re[:, rev][:, :, :, None] * bb_im[:, None] + pw_im[:, rev][:, :, :, None] * bb_re[:, None]
    e_re = e_re.transpose(0, 1, 3, 2)
    e_im = e_im.transpose(0, 1, 3, 2)
    emap = jnp.stack([compact(jnp.concatenate([e_re, e_im], axis=-1)),
                      compact(jnp.concatenate([e_im, e_re], axis=-1))], axis=1)
    q_c = jnp.stack([cl_re[:, 1:L + 1], -cl_im[:, 1:L + 1]], axis=1).reshape(nb, S5_GB, 2, L, S5_P, S5_N)
    qmap = jnp.einsum('bgctpn,gh->bcgnthp', q_c, eye).reshape(nb, 2 * S5_GB * S5_N, L * LANES)
    lr = pw_re[:, L].reshape(nb, 1, S5_GB * S5_N)
    li = pw_im[:, L].reshape(nb, 1, S5_GB * S5_N)
    lam_a = jnp.concatenate([lr, lr], axis=-1)
    lam_b = jnp.concatenate([-li, li], axis=-1)
    dvec = jnp.tile(d_skip.astype(F32).reshape(nb, 1, LANES), (1, 1, L))
    return dlag.astype(BF16), emap.astype(BF16), qmap.astype(BF16), lam_a, lam_b, dvec


def _s5_kernel(u_ref, dlag_ref, emap_ref, qmap_ref, la_ref, lb_ref, dv_ref, o_ref, toep_ref, esp_ref, h_ref, *, n_rows):
    L = S5_L
    ns = 2 * S5_GB * S5_N

    @pl.when(pl.program_id(1) == 0)
    def _():
        toep_ref[...] = jnp.zeros_like(toep_ref)
        esp_ref[...] = jnp.zeros_like(esp_ref)
        for s in range(L):
            for t in range(s, L):
                toep_ref[s * LANES:(s + 1) * LANES, t * LANES:(t + 1) * LANES] = dlag_ref[0, t - s]
            for g in range(S5_GB):
                rows = slice(s * LANES + g * S5_P, s * LANES + (g + 1) * S5_P)
                src = slice(g * S5_P, (g + 1) * S5_P)
                half = slice((g % 2) * S5_N, (g % 2 + 1) * S5_N)
                re_at = g * S5_N
                im_at = ns // 2 + g * S5_N
                esp_ref[rows, re_at:re_at + S5_N] = emap_ref[0, g % 2, s, src, half]
                esp_ref[rows, im_at:im_at + S5_N] = emap_ref[0, 1 - g % 2, s, src, half]

    up = jnp.concatenate([u_ref[pl.ds(s, n_rows, stride=L), :] for s in range(L)], axis=1)
    ub = up.astype(BF16)
    h_ref[...] = jnp.dot(ub, esp_ref[...], preferred_element_type=F32)
    la = la_ref[0]
    lb = lb_ref[0]

    def step(c8, h):
        rows = pl.ds(pl.multiple_of(c8 * 8, 8), 8)
        e = h_ref[rows, :]
        ent = []
        for j in range(8):
            ent.append(h)
            h = h * la + pltpu.roll(h, ns // 2, axis=1) * lb + e[j:j + 1, :]
        h_ref[rows, :] = jnp.concatenate(ent, axis=0)
        return h

    lax.fori_loop(0, n_rows // 8, step, jnp.zeros((1, ns), F32))
    hi, lo = _split2(h_ref[...])
    qm = qmap_ref[0]
    y_off = jnp.dot(hi, qm, preferred_element_type=F32) + jnp.dot(lo, qm, preferred_element_type=F32)
    wide = 2 * LANES
    for j in range(L * LANES // wide):
        k_end = (j + 1) * wide
        cols = slice(j * wide, k_end)
        y = jnp.dot(ub[:, :k_end], toep_ref[:k_end, cols], preferred_element_type=F32)
        y = y + y_off[:, cols] + up[:, cols] * dv_ref[0, :, cols]
        for t in range(2 * j, 2 * j + 2):
            o_ref[pl.ds(t, n_rows, stride=L), :] = y[:, (t - 2 * j) * LANES:(t - 2 * j + 1) * LANES]


def _s5_scan(u, tables, bsz, seq):
    dlag, emap, qmap, lam_a, lam_b, dvec = tables
    L = S5_L
    n_rows = seq // L
    nb = S5_WIDTH // LANES
    ns = 2 * S5_GB * S5_N
    blk3 = lambda j, b: (j, 0, 0)
    blk4 = lambda j, b: (j, 0, 0, 0)
    return pl.pallas_call(
        functools.partial(_s5_kernel, n_rows=n_rows),
        grid=(nb, bsz),
        in_specs=[pl.BlockSpec((seq, LANES), lambda j, b: (b, j)),
                  pl.BlockSpec((1, L, LANES, LANES), blk4),
                  pl.BlockSpec((1, 2, L, LANES, 2 * S5_N), lambda j, b: (j, 0, 0, 0, 0)),
                  pl.BlockSpec((1, ns, L * LANES), blk3),
                  pl.BlockSpec((1, 1, ns), blk3), pl.BlockSpec((1, 1, ns), blk3), pl.BlockSpec((1, 1, L * LANES), blk3)],
        out_specs=pl.BlockSpec((seq, LANES), lambda j, b: (b, j)),
        out_shape=jax.ShapeDtypeStruct((bsz * seq, S5_WIDTH), F32),
        scratch_shapes=[pltpu.VMEM((L * LANES, L * LANES), BF16), pltpu.VMEM((L * LANES, ns), BF16),
                        pltpu.VMEM((n_rows, ns), F32)],
        compiler_params=_cparams("arbitrary", "arbitrary"),
        name="s5_scan",
    )(u, dlag, emap, qmap, lam_a, lam_b, dvec)


def _sb_kernel(q_ref, k_ref, v_ref, tri_ref, o_ref, qh_ref, hl_ref, z_ref, w_ref, acc_ref, car_ref, *, t):
    qb = pl.program_id(1)
    heads = SB_WIDTH // SB_HEAD
    lane = lax.broadcasted_iota(jnp.int32, (t, LANES), 1)
    low = lane < SB_HEAD
    for p in range(heads // 2):
        q = (q_ref[:, p * LANES:(p + 1) * LANES].astype(F32) * (0.125 * math.log2(math.e))).astype(BF16)
        zero = jnp.zeros_like(q)
        qh_ref[2 * p] = jnp.where(low, q, zero)
        qh_ref[2 * p + 1] = jnp.where(low, zero, q)
    acc_ref[...] = jnp.zeros_like(acc_ref)
    car_ref[...] = jnp.zeros_like(car_ref)

    def scores(kb, h, slot, diagonal):
        rows = pl.ds(pl.multiple_of(kb * t, t), t)
        kblk = k_ref[(h // 2) * LANES:(h // 2 + 1) * LANES, rows]
        z = jnp.dot(qh_ref[h], kblk, preferred_element_type=F32)
        sp = jnp.maximum(z, 0.0) + jnp.log2(1.0 + jnp.exp2(-jnp.abs(z)))
        if diagonal:
            mask = lax.broadcasted_iota(jnp.int32, (t, t), 0) > lax.broadcasted_iota(jnp.int32, (t, t), 1)
            sp = jnp.where(mask, sp, 0.0)
            z = jnp.where(mask, z, -1e30)
        hi = sp.astype(BF16)
        hl_ref[slot, :, :t] = hi
        hl_ref[slot, :, t:] = (sp - hi.astype(F32)).astype(BF16)
        z_ref[slot] = z

    def weights(h, src, dst):
        cs = jnp.dot(hl_ref[src], tri_ref[...], preferred_element_type=F32)
        car = car_ref[h]
        s = z_ref[src] - cs - jnp.concatenate([car, car], axis=1)
        w_ref[dst] = jnp.exp2(s).astype(BF16)
        car_ref[h] = car + jnp.broadcast_to(cs[:, 0:1], car.shape)

    def values(kb, h, slot):
        rows = pl.ds(pl.multiple_of(kb * t, t), t)
        vblk = v_ref[rows, (h // 2) * LANES:(h // 2 + 1) * LANES]
        acc_ref[h] += jnp.dot(w_ref[slot], vblk, preferred_element_type=F32)

    def key_tile(kb, diagonal):
        for j in range(heads + 2):
            if j < heads:
                scores(kb, j, j % 2, diagonal)
            if 1 <= j <= heads:
                weights(j - 1, (j - 1) % 2, j % 2)
            if j >= 2:
                values(kb, j - 2, (j - 1) % 2)

    def faded():
        return jnp.min(car_ref[...]) >= SB_FADE

    key_tile(qb, True)

    def more(state):
        kb, _ = state
        key_tile(kb, False)
        return kb - 1, faded()

    lax.while_loop(lambda s: (s[0] >= 0) & jnp.logical_not(s[1]), more, (qb - 1, faded()))
    for p in range(heads // 2):
        o_ref[:, p * LANES:(p + 1) * LANES] = jnp.where(low, acc_ref[2 * p], acc_ref[2 * p + 1]).astype(o_ref.dtype)


def _sb_attention(q, k, v, bsz, seq):
    t = SB_T
    nq = seq // t
    heads = SB_WIDTH // SB_HEAD
    tri = (jnp.arange(t)[:, None] >= jnp.arange(t)[None, :]).astype(BF16)
    tri2 = jnp.concatenate([tri, tri], axis=0)
    return pl.pallas_call(
        functools.partial(_sb_kernel, t=t),
        grid=(bsz, nq),
        in_specs=[pl.BlockSpec((t, SB_WIDTH), lambda b, i: (b * nq + i, 0)),
                  pl.BlockSpec((SB_WIDTH, seq), lambda b, i: (b, 0)),
                  pl.BlockSpec((seq, SB_WIDTH), lambda b, i: (b, 0)),
                  pl.BlockSpec((2 * t, t), lambda b, i: (0, 0))],
        out_specs=pl.BlockSpec((t, SB_WIDTH), lambda b, i: (b * nq + i, 0)),
        out_shape=jax.ShapeDtypeStruct((bsz * seq, SB_WIDTH), BF16),
        scratch_shapes=[pltpu.VMEM((heads, t, LANES), BF16), pltpu.VMEM((2, t, 2 * t), BF16),
                        pltpu.VMEM((2, t, t), F32), pltpu.VMEM((2, t, t), BF16),
                        pltpu.VMEM((heads, t, LANES), F32), pltpu.VMEM((heads, t, LANES), F32)],
        compiler_params=_cparams("arbitrary", "arbitrary"),
        name="sb_attention",
    )(q, k.reshape(bsz, seq, SB_WIDTH).transpose(0, 2, 1).reshape(bsz * SB_WIDTH, seq), v, tri2)


def _mix_out_kernel(*refs, glu):
    if glu:
        ya_ref, yb_ref, x_ref, gate_ref, wglu_ref, wo_ref, nw_ref, sh_ref, sc_ref, wr_ref, br_ref = refs[:11]
        outs = refs[11:]
        a = _gelu_tanh(ya_ref[...].astype(F32)).astype(BF16)
        g = jnp.dot(a, wglu_ref[...], preferred_element_type=F32)
        half = g.shape[1] // 2
        ya = (g[:, :half] * _sigmoid(g[:, half:])).astype(BF16)
    else:
        ya_ref, yb_ref, x_ref, gate_ref, wo_ref, nw_ref, sh_ref, sc_ref, wr_ref, br_ref = refs[:10]
        outs = refs[10:]
        ya = ya_ref[...]
    x1_ref, h2_ref, lg_ref = outs
    ka = ya.shape[1]
    mix = (jnp.dot(ya, wo_ref[:ka, :], preferred_element_type=F32)
           + jnp.dot(yb_ref[...], wo_ref[ka:, :], preferred_element_type=F32))
    x1 = x_ref[...] + gate_ref[0] * mix
    x1_ref[...] = x1
    h2 = _rms_mod(x1, nw_ref[...], sh_ref[0], sc_ref[0])
    h2_ref[...] = h2.astype(BF16)
    lg_ref[...] = _dot3(h2, wr_ref[...]) + br_ref[...]


def _mix_out(ya, yb, x, gate, w_glu, w_out, nw, shift, scale, w_router, b_router, seq, tm):
    n_tok, d = x.shape
    tpb = seq // tm
    row = lambda i: (i, 0)
    bat = lambda i: (i // tpb, 0, 0)
    fix = lambda i: (0, 0)
    in_specs = [pl.BlockSpec((tm, ya.shape[1]), row), pl.BlockSpec((tm, yb.shape[1]), row),
                pl.BlockSpec((tm, d), row), pl.BlockSpec((1, 1, d), bat)]
    args = [ya, yb, x, gate]
    if w_glu is not None:
        in_specs.append(pl.BlockSpec(w_glu.shape, fix))
        args.append(w_glu)
    in_specs += [pl.BlockSpec(w_out.shape, fix), pl.BlockSpec((1, d), fix), pl.BlockSpec((1, 1, d), bat),
                 pl.BlockSpec((1, 1, d), bat), pl.BlockSpec(w_router.shape, fix), pl.BlockSpec(b_router.shape, fix)]
    args += [w_out, nw, shift, scale, w_router, b_router]
    return pl.pallas_call(
        functools.partial(_mix_out_kernel, glu=w_glu is not None),
        grid=(n_tok // tm,),
        in_specs=in_specs,
        out_specs=[pl.BlockSpec((tm, d), row), pl.BlockSpec((tm, d), row), pl.BlockSpec((tm, LANES), row)],
        out_shape=[jax.ShapeDtypeStruct((n_tok, d), F32), jax.ShapeDtypeStruct((n_tok, d), BF16),
                   jax.ShapeDtypeStruct((n_tok, LANES), F32)],
        compiler_params=_cparams("arbitrary"),
        name="mix_out",
    )(*args)


def _route_kernel(lg_ref, tri_ref, o_ref, ot_ref, cnt_ref, run_ref):
    @pl.when(pl.program_id(0) == 0)
    def _():
        run_ref[...] = jnp.zeros_like(run_ref)

    lg = lg_ref[...]
    lane = lax.broadcasted_iota(jnp.int32, lg.shape, 1)
    neg = jnp.float32(-jnp.inf)
    far = jnp.int32(LANES)

    def top(x):
        m = jnp.max(x, axis=1, keepdims=True)
        return m, jnp.min(jnp.where(x == m, lane, far), axis=1, keepdims=True)

    is_grp = lane < MOE_GROUPS
    gmax, gidx = top(jnp.where(is_grp, lg, neg))
    grp_p = 1.0 / jnp.sum(jnp.where(is_grp, jnp.exp(lg - gmax), 0.0), axis=1, keepdims=True)
    first = MOE_GROUPS + MOE_PER_GROUP * gidx
    el = jnp.where((lane >= first) & (lane < first + MOE_PER_GROUP), lg, neg)
    l0, i0 = top(el)
    l1, i1 = top(jnp.where(lane == i0, neg, el))
    e1 = jnp.exp(l1 - l0)
    g0 = grp_p / (1.0 + e1)
    g1 = grp_p * e1 / (1.0 + e1)
    pick0 = lane == i0
    pick1 = lane == i1
    onehot = jnp.where(pick0 | pick1, 1.0, 0.0)
    before = jnp.dot(tri_ref[...], onehot.astype(BF16), preferred_element_type=F32) + run_ref[...]
    r0 = jnp.sum(jnp.where(pick0, before, 0.0), axis=1, keepdims=True)
    r1 = jnp.sum(jnp.where(pick1, before, 0.0), axis=1, keepdims=True)
    run = run_ref[...] + jnp.sum(onehot, axis=0, keepdims=True)
    run_ref[...] = run
    cnt_ref[...] = run
    cols = ((i0 - MOE_GROUPS).astype(F32), (i1 - MOE_GROUPS).astype(F32), r0, r1, g0, g1)
    out = jnp.zeros(lg.shape, F32)
    for j, col in enumerate(cols):
        out = jnp.where(lane == j, col, out)
    o_ref[...] = out
    ot_ref[...] = jnp.transpose(out)[0:8, :]


def _route(logits, tm):
    n_tok = logits.shape[0]
    tri = (jnp.arange(tm)[:, None] > jnp.arange(tm)[None, :]).astype(BF16)
    return pl.pallas_call(
        _route_kernel,
        grid=(n_tok // tm,),
        in_specs=[pl.BlockSpec((tm, LANES), lambda i: (i, 0)), pl.BlockSpec((tm, tm), lambda i: (0, 0))],
        out_specs=[pl.BlockSpec((tm, LANES), lambda i: (i, 0)), pl.BlockSpec((8, tm), lambda i: (0, i)),
                   pl.BlockSpec((1, LANES), lambda i: (0, 0))],
        out_shape=[jax.ShapeDtypeStruct((n_tok, LANES), F32), jax.ShapeDtypeStruct((8, n_tok), F32),
                   jax.ShapeDtypeStruct((1, LANES), F32)],
        scratch_shapes=[pltpu.VMEM((1, LANES), F32)],
        compiler_params=_cparams("arbitrary"),
        name="route",
    )(logits, tri)


def _dispatch(fields, counts, n_tok):
    n_assign = n_tok * MOE_TOPK
    n_blocks = (n_assign + MOE_EXPERTS * (MOE_BLK - 1) + MOE_BLK - 1) // MOE_BLK
    n_rows = n_blocks * MOE_BLK
    experts = fields[0:2].astype(jnp.int32)
    rank = fields[2:4].astype(jnp.int32)
    counts = counts[0, MOE_GROUPS:MOE_GROUPS + MOE_EXPERTS].astype(jnp.int32)
    padded = ((counts + MOE_BLK - 1) // MOE_BLK) * MOE_BLK
    pad_end = jnp.cumsum(padded)
    pad_start = pad_end - padded
    dest = rank
    for e in range(MOE_EXPERTS):
        dest = dest + jnp.where(experts == e, pad_start[e], 0)
    blk_start = jnp.arange(n_blocks, dtype=jnp.int32) * MOE_BLK
    blk_expert = jnp.minimum(jnp.sum((pad_end[None, :] <= blk_start[:, None]).astype(jnp.int32), axis=1),
                             MOE_EXPERTS - 1).astype(jnp.int32)
    n_used = (pad_end[-1] // MOE_BLK).astype(jnp.int32).reshape(1)
    n_pad = n_rows - n_assign
    gap_end = jnp.cumsum(padded - counts)
    i = jnp.arange(n_pad, dtype=jnp.int32)
    owner = jnp.sum((gap_end[None, :] <= i[:, None]).astype(jnp.int32), axis=1)
    hot = owner[:, None] == jnp.arange(MOE_EXPERTS + 1, dtype=jnp.int32)[None, :]
    first_gap = jnp.concatenate([pad_start + counts, pad_end[-1:]])
    before = jnp.concatenate([gap_end - (padded - counts), gap_end[-1:]])
    pad_row = i + jnp.sum(jnp.where(hot, (first_gap - before)[None, :], 0), axis=1)
    keys = jnp.concatenate([dest.reshape(-1), pad_row])
    vals = jnp.concatenate([jnp.tile(jnp.arange(n_tok, dtype=jnp.int32), MOE_TOPK), pad_row % n_tok])
    _, row_tok = lax.sort((keys, vals), num_keys=1, is_stable=False)
    return dest, row_tok, blk_expert, n_used, n_blocks


def _moe_kernel(be_ref, nu_ref, x_ref, wg_ref, wu_ref, wd_ref, o_ref, wg_s, wu_s, wd_s):
    i = pl.program_id(0)
    e = be_ref[i]
    prev = be_ref[jnp.maximum(i - 1, 0)]

    @pl.when((i == 0) | (e != prev))
    def _():
        wg_s[...] = wg_ref[0, 0].astype(BF16)
        wu_s[...] = wu_ref[0, 0].astype(BF16)
        wd_s[...] = wd_ref[0, 0].astype(BF16)

    @pl.when(i < nu_ref[0])
    def _():
        x = x_ref[...]
        a = jnp.dot(x, wg_s[...], preferred_element_type=F32)
        b = jnp.dot(x, wu_s[...], preferred_element_type=F32)
        hid = (_silu(a) * b).astype(BF16)
        o_ref[...] = jnp.dot(hid, wd_s[...], preferred_element_type=F32).astype(o_ref.dtype)

    @pl.when(i >= nu_ref[0])
    def _():
        o_ref[...] = jnp.zeros_like(o_ref)


def _moe_experts(xs, blk_expert, n_used, w_gate, w_up, w_down, layer, n_blocks):
    d = xs.shape[1]
    hid = w_gate.shape[3]
    grid_spec = pltpu.PrefetchScalarGridSpec(
        num_scalar_prefetch=2,
        grid=(n_blocks,),
        in_specs=[pl.BlockSpec((MOE_BLK, d), lambda i, be, nu: (i, 0)),
                  pl.BlockSpec((1, 1, d, hid), lambda i, be, nu: (layer, be[i], 0, 0)),
                  pl.BlockSpec((1, 1, d, hid), lambda i, be, nu: (layer, be[i], 0, 0)),
                  pl.BlockSpec((1, 1, hid, d), lambda i, be, nu: (layer, be[i], 0, 0))],
        out_specs=pl.BlockSpec((MOE_BLK, d), lambda i, be, nu: (i, 0)),
        scratch_shapes=[pltpu.VMEM((d, hid), BF16), pltpu.VMEM((d, hid), BF16), pltpu.VMEM((hid, d), BF16)],
    )
    return pl.pallas_call(
        _moe_kernel,
        grid_spec=grid_spec,
        out_shape=jax.ShapeDtypeStruct((n_blocks * MOE_BLK, d), BF16),
        compiler_params=_cparams("arbitrary"),
        name="moe_experts",
    )(blk_expert, n_used, xs, w_gate, w_up, w_down)


def _moe(h2, logits, w_gate, w_up, w_down, layer):
    n_tok, d = h2.shape
    routed, fields, counts = _route(logits, 512)
    dest, row_tok, blk_expert, n_used, n_blocks = _dispatch(fields, counts, n_tok)
    rows = lambda a, idx: a.at[idx].get(mode='promise_in_bounds')
    xs = rows(h2, row_tok)
    ys = _moe_experts(xs, blk_expert, n_used, w_gate, w_up, w_down, layer, n_blocks)
    return rows(ys, dest.reshape(-1)), routed


def _rw_kernel(p_ref, mu_ref, pv_ref, wwa_ref, g2_ref, bd_ref, tri_ref, o_ref, st_ref, last_ref, *, tt):
    i = pl.program_id(1)
    L = RW_L
    W = RW_WIDTH

    @pl.when(i == 0)
    def _():
        st_ref[...] = jnp.zeros_like(st_ref)
        last_ref[...] = jnp.zeros_like(last_ref)

    p = p_ref[...].astype(F32)
    row = lax.broadcasted_iota(jnp.int32, p.shape, 0)
    prev = jnp.where(row == 0, jnp.broadcast_to(last_ref[0:1, :], p.shape), pltpu.roll(p, 1, axis=0))
    last_ref[0:1, :] = p[tt - 1:tt, :]
    p = p + (prev - p) * mu_ref[...]
    r = p[:, 0:W]
    k = p[:, W:2 * W]
    v = p[:, 2 * W:3 * W]
    lw = p[:, 3 * W:3 * W + LANES]
    xg = p[:, 3 * W + LANES:3 * W + 2 * LANES]
    w0, a0, k_k, k_a, r_k, ln_w, ln_b = (pv_ref[j:j + 1, :] for j in range(7))
    lane = lax.broadcasted_iota(jnp.int32, lw.shape, 1)
    wa = _bdot(jnp.where(lane < 64, jnp.tanh(lw), lw), wwa_ref[...])
    w = -_softplus(-(w0 + wa[:, :W])) - 0.5
    logd = -jnp.exp(w)
    lr = _sigmoid(a0 + wa[:, W:])
    gate = _bdot(_sigmoid(xg), g2_ref[...])
    bd = bd_ref[...]
    kk = k * k_k
    kk = kk / jnp.maximum(jnp.sqrt(_bdot(kk * kk, bd)), 1e-12)
    k = k * (1.0 + (lr - 1.0) * k_a)
    av = -kk
    bv = kk * lr

    tri = tri_ref[...]
    lane_p = lax.broadcasted_iota(jnp.int32, (L, LANES), 1)
    m0 = lane_p < RW_HEAD
    ri = lax.broadcasted_iota(jnp.int32, (2 * L, 2 * L), 0)
    ci = lax.broadcasted_iota(jnp.int32, (2 * L, 2 * L), 1)
    same = (ri // L) == (ci // L)
    mask_s = same & (ci < ri)
    mask_i = same & (ci <= ri)
    mask_s2 = jnp.concatenate([mask_s, mask_s], axis=1)
    mask_i2 = jnp.concatenate([mask_i, mask_i], axis=1)
    eye = (ri == ci).astype(F32)

    def stack(x):
        return jnp.concatenate([jnp.where(m0, x, 0.0), jnp.where(m0, 0.0, x)], axis=0)

    n_chunks = tt // L
    n_pairs = RW_HEADS // 2
    aa, rr, vv, bk2, bkg, gam = [], [], [], [], [], []
    for c in range(n_chunks):
        rows = slice(c * L, (c + 1) * L)
        ld = logd[rows]
        cs = _dot_exact_lhs(tri, ld)
        total = cs[L - 1:L, :]
        e_out = jnp.exp(-cs)
        e_end = jnp.exp(total - cs)
        at = av[rows] * jnp.exp(cs - ld)
        rt = r[rows] * jnp.exp(cs)
        bt = bv[rows] * e_out
        kt = k[rows] * e_out
        bg = bv[rows] * e_end
        kg = k[rows] * e_end
        g_all = jnp.exp(total)
        for hp in range(n_pairs):
            ls = slice(hp * LANES, (hp + 1) * LANES)
            aa.append(stack(at[:, ls]).astype(BF16))
            rr.append(stack(rt[:, ls]))
            vv.append(stack(v[rows, ls]).astype(BF16))
            bk2.append(jnp.concatenate([bt[:, ls], bt[:, ls], kt[:, ls], kt[:, ls]], axis=0).astype(BF16))
            bkg.append(jnp.concatenate([stack(bg[:, ls]), stack(kg[:, ls])], axis=0).astype(BF16))
            gam.append(g_all[:, ls])
    n_sys = len(aa)
    rng = range(n_sys)
    n_a = [jnp.where(mask_s2, _bdot_nt(aa[i], bk2[i]), 0.0) for i in rng]
    n_r = [jnp.where(mask_i2, _bdot_nt(rr[i], bk2[i]), 0.0).astype(BF16) for i in rng]
    n_ab = [n_a[i][:, :2 * L] for i in rng]
    n_ak = [n_a[i][:, 2 * L:].astype(BF16) for i in rng]
    n_rb = [n_r[i][:, :2 * L] for i in rng]
    tinv = [eye + n_ab[i] for i in rng]
    pw = n_ab
    for _ in range(int(math.log2(L)) - 1):
        pw = [_bdot(pw[i], pw[i]) for i in rng]
        tinv = [tinv[i] + _bdot(tinv[i], pw[i]) for i in rng]
    tinv = [tinv[i].astype(BF16) for i in rng]
    akv = [_bdot(n_ak[i], vv[i]).astype(BF16) for i in rng]
    tx = [_bdot(tinv[i], jnp.concatenate([akv[i], aa[i]], axis=1)).astype(BF16) for i in rng]
    tav = [tx[i][:, :LANES] for i in rng]
    ta = [tx[i][:, LANES:] for i in rng]
    uv = [jnp.concatenate([tav[i], vv[i]], axis=0) for i in rng]
    mg = [_bdot_tn(ta[i], bkg[i][:2 * L]).astype(BF16) for i in rng]
    ra = [(rr[i] + _bdot(n_rb[i], ta[i])).astype(BF16) for i in rng]
    y0 = [_bdot(n_r[i], uv[i]) for i in rng]
    s0 = [_bdot_tn(uv[i], bkg[i]) for i in rng]
    for c in range(n_chunks):
        rows = slice(c * L, (c + 1) * L)
        y_parts = []
        for hp in range(n_pairs):
            i = c * n_pairs + hp
            s = st_ref[hp]
            sb = s.astype(BF16)
            yst = _bdot_nt(ra[i], sb) + y0[i]
            y_parts.append(yst[:L] + yst[L:])
            st_ref[hp] = s * gam[i] + jnp.dot(sb, mg[i], preferred_element_type=F32) + s0[i]
        y = jnp.concatenate(y_parts, axis=1)
        mean = _bdot(y, bd) * (1.0 / RW_HEAD)
        yc = y - mean
        var = _bdot(yc * yc, bd) * (1.0 / RW_HEAD)
        yn = yc * lax.rsqrt(var + RW_LN_EPS) * ln_w + ln_b
        bonus = _bdot(r[rows] * k[rows] * r_k, bd) * v[rows]
        o_ref[rows, :] = ((yn + bonus) * gate[rows]).astype(o_ref.dtype)


def _rwkv7(p, mu, pv, wwa, g2, bsz, seq):
    tt = RW_TT
    nt = seq // tt
    L = RW_L
    hid = jnp.arange(RW_WIDTH) // RW_HEAD
    bd = (hid[:, None] == hid[None, :]).astype(BF16)
    tri = (jnp.arange(L)[:, None] >= jnp.arange(L)[None, :]).astype(BF16)
    fix = lambda b, i: (0, 0)
    return pl.pallas_call(
        functools.partial(_rw_kernel, tt=tt),
        grid=(bsz, nt),
        in_specs=[pl.BlockSpec((tt, RW_IN), lambda b, i: (b * nt + i, 0)),
                  pl.BlockSpec(mu.shape, fix), pl.BlockSpec(pv.shape, fix), pl.BlockSpec(wwa.shape, fix),
                  pl.BlockSpec(g2.shape, fix), pl.BlockSpec(bd.shape, fix), pl.BlockSpec(tri.shape, fix)],
        out_specs=pl.BlockSpec((tt, RW_WIDTH), lambda b, i: (b * nt + i, 0)),
        out_shape=jax.ShapeDtypeStruct((bsz * seq, RW_WIDTH), BF16),
        scratch_shapes=[pltpu.VMEM((RW_HEADS // 2, 2 * RW_HEAD, LANES), F32), pltpu.VMEM((8, RW_IN), F32)],
        compiler_params=_cparams("arbitrary", "arbitrary"),
        name="rwkv7",
    )(p, mu, pv, wwa, g2, bd, tri)


def _ssd_kernel(p_ref, cw_ref, cb_ref, hv_ref, dl_ref, nw_ref, tri_ref, o_ref, st_ref, tail_ref, *, L):
    i = pl.program_id(1)

    @pl.when(i == 0)
    def _():
        st_ref[...] = jnp.zeros_like(st_ref)
        tail_ref[...] = jnp.zeros_like(tail_ref)

    z = p_ref[:, 0:M2_INNER].astype(F32)
    xin = p_ref[:, M2_INNER:M2_INNER + M2_CONV_DIM].astype(F32)
    dt_raw = p_ref[:, M2_INNER + M2_CONV_DIM:M2_INNER + M2_CONV_DIM + LANES].astype(F32)
    tail = tail_ref[...]
    tail_ref[...] = xin[L - 8:L, :]
    row8 = lax.broadcasted_iota(jnp.int32, (8, M2_CONV_DIM), 0)
    conv = xin * cw_ref[M2_CONV - 1:M2_CONV, :] + cb_ref[...]
    for j in range(1, M2_CONV):
        rolled = pltpu.roll(xin, j, axis=0)
        head = jnp.where(row8 < j, pltpu.roll(tail, j, axis=0), rolled[0:8])
        shifted = jnp.concatenate([head, rolled[8:]], axis=0)
        conv = conv + shifted * cw_ref[M2_CONV - 1 - j:M2_CONV - j, :]
    xbc = _silu(conv)
    xs = xbc[:, 0:M2_INNER]
    dt = _softplus(dt_raw + hv_ref[0:1, :])
    adt = dt * hv_ref[1:2, :]
    cs = _dot_exact_lhs(tri_ref[...], adt)
    cs_t = jnp.transpose(cs)
    total = cs[L - 1:L, :]
    dec_in = jnp.exp(cs)
    dec_out = jnp.exp(total - cs)
    dec_all = jnp.exp(total)
    lane = lax.broadcasted_iota(jnp.int32, (L, LANES), 1)
    first = lane < 64
    rowp = lax.broadcasted_iota(jnp.int32, (LANES, M2_STATE), 0) < 64
    li = lax.broadcasted_iota(jnp.int32, (L, L), 0)
    si = lax.broadcasted_iota(jnp.int32, (L, L), 1)
    causal = li >= si
    y_parts = []
    for hp in range(M2_HEADS // 2):
        g = hp // 2
        h0, h1 = 2 * hp, 2 * hp + 1
        bm = xbc[:, M2_INNER + g * M2_STATE:M2_INNER + (g + 1) * M2_STATE]
        cm = xbc[:, M2_INNER + 2 * M2_STATE + g * M2_STATE:M2_INNER + 2 * M2_STATE + (g + 1) * M2_STATE]
        x_p = xs[:, hp * LANES:(hp + 1) * LANES]
        sel = lambda a: jnp.where(first, a[:, h0:h0 + 1], a[:, h1:h1 + 1])
        xd = x_p * sel(dt)
        cb = _bdot_nt(cm, bm)
        yd = []
        for h in (h0, h1):
            lmat = jnp.where(causal, jnp.exp(cs[:, h:h + 1] - cs_t[h:h + 1, :]), 0.0)
            yd.append(_bdot(cb * lmat, xd))
        s = st_ref[hp]
        y_off = _bdot_nt(cm, s) * sel(dec_in)
        y_parts.append(jnp.where(first, yd[0], yd[1]) + y_off)
        s_new = _bdot_tn(xd * sel(dec_out), bm)
        st_ref[hp] = s * jnp.where(rowp, dec_all[:, h0:h0 + 1], dec_all[:, h1:h1 + 1]) + s_new
    y = jnp.concatenate(y_parts, axis=1) + dl_ref[...] * xs
    y = y * _silu(z)
    half = M2_INNER // 2
    outs = []
    for g in range(2):
        yg = y[:, g * half:(g + 1) * half]
        outs.append(yg * lax.rsqrt(jnp.mean(yg * yg, axis=-1, keepdims=True) + RMS_EPS))
    o_ref[...] = (jnp.concatenate(outs, axis=1) * nw_ref[...]).astype(o_ref.dtype)


def _ssd(p, conv_w, conv_b, hv, d_lanes, norm_w, bsz, seq):
    L = M2_L
    nt = seq // L
    tri = (jnp.arange(L)[:, None] >= jnp.arange(L)[None, :]).astype(BF16)
    fix = lambda b, i: (0, 0)
    return pl.pallas_call(
        functools.partial(_ssd_kernel, L=L),
        grid=(bsz, nt),
        in_specs=[pl.BlockSpec((L, M2_IN_PAD), lambda b, i: (b * nt + i, 0)),
                  pl.BlockSpec(conv_w.shape, fix), pl.BlockSpec(conv_b.shape, fix), pl.BlockSpec(hv.shape, fix),
                  pl.BlockSpec(d_lanes.shape, fix), pl.BlockSpec(norm_w.shape, fix), pl.BlockSpec(tri.shape, fix)],
        out_specs=pl.BlockSpec((L, M2_INNER), lambda b, i: (b * nt + i, 0)),
        out_shape=jax.ShapeDtypeStruct((bsz * seq, M2_INNER), BF16),
        scratch_shapes=[pltpu.VMEM((M2_HEADS // 2, LANES, M2_STATE), F32), pltpu.VMEM((8, M2_CONV_DIM), F32)],
        compiler_params=_cparams("arbitrary", "arbitrary"),
        name="ssd",
    )(p, conv_w, conv_b, hv, d_lanes, norm_w, tri)


def _final_kernel(x_ref, ya_ref, yb_ref, g_ref, gate_ref, nw_ref, o_ref):
    g = g_ref[...]
    moe = g[:, 4:5] * ya_ref[...].astype(F32) + g[:, 5:6] * yb_ref[...].astype(F32)
    x = x_ref[...] + gate_ref[0] * moe
    ms = jnp.mean(x * x, axis=-1, keepdims=True)
    o_ref[...] = x * lax.rsqrt(ms + RMS_EPS) * nw_ref[...]


def _final(x, yy, g, gate, nw, seq, tm):
    n_tok, d = x.shape
    tpb = seq // tm
    row = lambda i: (i, 0)
    return pl.pallas_call(
        _final_kernel,
        grid=(n_tok // tm,),
        in_specs=[pl.BlockSpec((tm, d), row), pl.BlockSpec((tm, d), row),
                  pl.BlockSpec((tm, d), lambda i: (i + n_tok // tm, 0)),
                  pl.BlockSpec((tm, g.shape[1]), row), pl.BlockSpec((1, 1, d), lambda i: (i // tpb, 0, 0)),
                  pl.BlockSpec((1, d), lambda i: (0, 0))],
        out_specs=pl.BlockSpec((tm, d), row),
        out_shape=jax.ShapeDtypeStruct((n_tok, d), F32),
        compiler_params=_cparams("arbitrary"),
        name="final_norm",
    )(x, yy, yy, g, gate, nw)


def _router_weights(w_grp, b_grp, w_exp, b_exp):
    d = w_grp.shape[0]
    pad = LANES - MOE_GROUPS - MOE_EXPERTS
    w = jnp.concatenate([w_grp, w_exp, jnp.zeros((d, pad), F32)], axis=1)
    b = jnp.concatenate([b_grp, b_exp, jnp.zeros((pad,), F32)])[None, :]
    return w, b


def kernel(x, c, ada_w, ada_b, norm_mix_w, norm_ffn_w, even_w_in, even_w_out, s5_a_re, s5_a_im, s5_log_dt, s5_b_re, s5_b_im, s5_c_re, s5_c_im, s5_d, s5_w_glu, odd_w_in, odd_w_out, rw_mu, rw_w0, rw_w2, rw_a0, rw_a2, rw_g2, rw_k_k, rw_k_a, rw_r_k, rw_ln_w, rw_ln_b, m2_conv_w, m2_conv_b, m2_dt_bias, m2_a_log, m2_d, m2_norm_w, moe_w_grp, moe_b_grp, moe_w_exp, moe_b_exp, moe_w_gate, moe_w_up, moe_w_down, final_norm_w):
    bsz, seq, d = x.shape
    n_tok = bsz * seq
    tm = 512
    xt = x.reshape(n_tok, d)
    ada = _ada_params(c, ada_w, ada_b)
    mods = [[ada[i, :, j * d:(j + 1) * d].reshape(bsz, 1, d) for j in range(6)] for i in range(2)]

    sh1, sc1, g1, sh2, sc2, g2 = mods[0]
    u, q, k, v = _in_proj(xt, norm_mix_w[0][None, :], sh1, sc1, even_w_in[0].astype(BF16),
                          ((0, 512), (512, 1024), (1024, 1536), (1536, 2048)), (F32, BF16, BF16, BF16), seq, tm)
    tables = _s5_tables(s5_a_re[0], s5_a_im[0], s5_log_dt[0], s5_b_re[0], s5_b_im[0], s5_c_re[0], s5_c_im[0], s5_d[0])
    y_s5 = _s5_scan(u, tables, bsz, seq)
    y_sb = _sb_attention(q, k, v, bsz, seq)
    wr, br = _router_weights(moe_w_grp[0], moe_b_grp[0], moe_w_exp[0], moe_b_exp[0])
    x1, h2, logits = _mix_out(y_s5, y_sb, xt, g1, s5_w_glu[0].astype(BF16), even_w_out[0].astype(BF16),
                              norm_ffn_w[0][None, :], sh2, sc2, wr, br, seq, tm)
    yy, gates = _moe(h2, logits, moe_w_gate, moe_w_up, moe_w_down, 0)

    gate_prev = g2
    sh1, sc1, g1, sh2, sc2, g2 = mods[1]
    w_in = jnp.concatenate([odd_w_in[0], jnp.zeros((d, M2_IN_PAD - M2_IN), F32)], axis=1).astype(BF16)
    x2, p_rw, p_m2 = _in_proj(x1, norm_mix_w[1][None, :], sh1, sc1, w_in,
                              ((0, RW_IN), (RW_IN, RW_IN + M2_IN_PAD)), (BF16, BF16), seq, tm,
                              add=(yy, gates, gate_prev))
    pv = jnp.stack([rw_w0[0], rw_a0[0], rw_k_k[0], rw_k_a[0], rw_r_k[0].reshape(-1), rw_ln_w[0], rw_ln_b[0],
                    jnp.zeros((RW_WIDTH,), F32)])
    zl = jnp.zeros((64, RW_WIDTH), F32)
    wwa = jnp.concatenate([jnp.concatenate([rw_w2[0], zl], axis=1),
                           jnp.concatenate([zl, rw_a2[0]], axis=1)], axis=0).astype(BF16)
    y_rw = _rwkv7(p_rw, rw_mu[0][None, :], pv, wwa, rw_g2[0].astype(BF16), bsz, seq)
    hv = jnp.zeros((8, LANES), F32)
    hv = hv.at[0, :M2_HEADS].set(m2_dt_bias[0]).at[1, :M2_HEADS].set(-jnp.exp(m2_a_log[0]))
    y_m2 = _ssd(p_m2, m2_conv_w[0], m2_conv_b[0][None, :], hv, jnp.repeat(m2_d[0], 64)[None, :],
                m2_norm_w[0][None, :], bsz, seq)
    wr, br = _router_weights(moe_w_grp[1], moe_b_grp[1], moe_w_exp[1], moe_b_exp[1])
    x3, h2, logits = _mix_out(y_rw, y_m2, x2, g1, None, odd_w_out[0].astype(BF16),
                              norm_ffn_w[1][None, :], sh2, sc2, wr, br, seq, tm)
    yy, gates = _moe(h2, logits, moe_w_gate, moe_w_up, moe_w_down, 1)
    out = _final(x3, yy, gates, g2, final_norm_w[None, :], seq, tm)
    return out.reshape(bsz, seq, d)
```
